```python
import jax
import jax.numpy as jnp
from jax import lax
import numpy as np

D_MODEL = 1024
BATCH = 1
SEQ = 16384
DEPTH = 4
DEC_BATCH = 16
DEC_SEQ = 16
PAST_LEN = 1024

CHUNK = 64
N_A = DEPTH // 2
N_B = DEPTH - N_A
HEAD_DIM = 64
N_HEADS = D_MODEL // HEAD_DIM
D_DECAY_LORA = 64
D_AAA_LORA = 64
D_MV_LORA = 32
D_GATE_LORA = 128
N_EXPERTS = 16
N_GROUPS = 4
EXPERTS_PER_GROUP = N_EXPERTS // N_GROUPS
TOP_K = 2
D_EXPERT = 512
Q_BLOCK = 128
E_BLOCK = 128
DN_ALPHA = (2 * DEPTH) ** 0.25
DN_BETA = (8 * DEPTH) ** -0.25
LN_EPS = 1e-5
GN_EPS = 64e-5
NEG_INF = -1e30

kernel_name = 'rwkv7_fox_yoco_groupmoe_stream_step'


def _layer_norm(x, g, b):
    xf = x.astype(jnp.float32)
    mu = xf.mean(-1, keepdims=True)
    var = jnp.square(xf - mu).mean(-1, keepdims=True)
    return ((xf - mu) * lax.rsqrt(var + LN_EPS) * g + b).astype(x.dtype)


def _wkv7_scan(S0, r, w, k, v, a, b):
    def step(S, inp):
        r_t, w_t, k_t, v_t, a_t, b_t = inp
        sa = jnp.einsum('bhvk,bhk->bhv', S, a_t)
        S = S * w_t[:, :, None, :] + sa[..., None] * b_t[:, :, None, :] + v_t[..., None] * k_t[:, :, None, :]
        return S, jnp.einsum('bhvk,bhk->bhv', S, r_t)
    xs = tuple(jnp.swapaxes(t, 0, 1) for t in (r, w, k, v, a, b))
    S_T, y = lax.scan(step, S0, xs)
    return S_T, jnp.swapaxes(y, 0, 1)


def _rwkv7_time_mix(x, shift0, S0, v_first, vres, mix, w_rkv, w0, w1, w2, a0, a1, a2,
                    g1, g2, k_k, k_a, r_k, lnx_g, lnx_b, w_o):
    B, T, D = x.shape
    x_prev = jnp.concatenate([shift0[:, None, :].astype(x.dtype), x[:, :-1]], axis=1)
    xx = x_prev - x
    xr, xw, xk, xv, xa, xg = (x + xx * mix[i] for i in range(6))
    r, k, v = jnp.einsum('cbtd,cde->cbte', jnp.stack([xr, xk, xv]), w_rkv)
    w_log = -jax.nn.softplus(-(w0 + jnp.tanh(xw @ w1) @ w2).astype(jnp.float32)) - 0.5
    decay = jnp.exp(-jnp.exp(w_log))
    if vres is None:
        v_first = v
    else:
        v0, v1, v2 = vres
        v = v + (v_first - v) * jax.nn.sigmoid(v0 + (xv @ v1) @ v2)
    a = jax.nn.sigmoid(a0 + (xa @ a1) @ a2)
    g = jax.nn.sigmoid(xg @ g1) @ g2

    def heads(t):
        return t.reshape(B, T, N_HEADS, HEAD_DIM).astype(jnp.float32)

    kk = heads(k * k_k)
    kk = kk / jnp.maximum(jnp.linalg.norm(kk, axis=-1, keepdims=True), 1e-12)
    k = k * (1 + (a - 1) * k_a)
    rh, kh, vh, ah = heads(r), heads(k), heads(v), heads(a)
    S_T, y = _wkv7_scan(S0.astype(jnp.float32), rh, heads(decay), kh, vh, -kk, kk * ah)
    mu = y.mean(-1, keepdims=True)
    var = jnp.square(y - mu).mean(-1, keepdims=True)
    y = (y - mu) * lax.rsqrt(var + GN_EPS) * lnx_g.reshape(N_HEADS, HEAD_DIM) + lnx_b.reshape(N_HEADS, HEAD_DIM)
    y = y + (rh * kh * r_k).sum(-1, keepdims=True) * vh
    out = (y.reshape(B, T, D).astype(x.dtype) * g) @ w_o
    return out, x[:, -1], S_T, v_first


def _shared_kv(x, kv_w, kv_bf):
    B, T, D = x.shape
    proj = x @ kv_w
    k = proj[..., :D].reshape(B, T, N_HEADS, HEAD_DIM)
    v = proj[..., D:2 * D].reshape(B, T, N_HEADS, HEAD_DIM)
    logf = jax.nn.log_sigmoid(proj[..., 2 * D:].astype(jnp.float32) + kv_bf)
    return k, v, logf


def _fox_attn_prompt(q, k, v, logf):
    B, S, H, dh = q.shape
    cum = jnp.cumsum(logf, axis=1)
    scale = dh ** -0.5
    n_blk = S // Q_BLOCK
    offs = jnp.arange(Q_BLOCK)

    def q_block(i):
        q0 = i * Q_BLOCK
        qb = lax.dynamic_slice_in_dim(q, q0, Q_BLOCK, 1).astype(jnp.float32) * scale
        cq = lax.dynamic_slice_in_dim(cum, q0, Q_BLOCK, 1).transpose(0, 2, 1)[..., None]
        q_pos = q0 + offs

        def body(j, carry):
            m, l, acc = carry
            k0 = j * Q_BLOCK
            kb = lax.dynamic_slice_in_dim(k, k0, Q_BLOCK, 1).astype(jnp.float32)
            vb = lax.dynamic_slice_in_dim(v, k0, Q_BLOCK, 1).astype(jnp.float32)
            ck = lax.dynamic_slice_in_dim(cum, k0, Q_BLOCK, 1).transpose(0, 2, 1)[:, :, None, :]
            s = jnp.einsum('bqhd,bkhd->bhqk', qb, kb) + cq - ck
            s = jnp.where(q_pos[:, None] >= (k0 + offs)[None, :], s, NEG_INF)
            m_new = jnp.maximum(m, s.max(-1))
            p = jnp.exp(s - m_new[..., None])
            corr = jnp.exp(m - m_new)
            return (m_new, l * corr + p.sum(-1),
                    acc * corr[..., None] + jnp.einsum('bhqk,bkhd->bhqd', p, vb))

        init = (jnp.full((B, H, Q_BLOCK), NEG_INF, jnp.float32),
                jnp.zeros((B, H, Q_BLOCK), jnp.float32),
                jnp.zeros((B, H, Q_BLOCK, dh), jnp.float32))
        m, l, acc = lax.fori_loop(0, i + 1, body, init)
        return (acc / l[..., None]).transpose(0, 2, 1, 3)

    out = lax.map(q_block, jnp.arange(n_blk))
    return out.transpose(1, 0, 2, 3, 4).reshape(B, S, H, dh).astype(q.dtype)


def _fox_attn_sample(q, k_all, v_all, logf_all):
    B, T, H, dh = q.shape
    L = k_all.shape[1]
    past_len = L - T
    cum = jnp.cumsum(logf_all, axis=1)
    cq = cum[:, past_len:].transpose(0, 2, 1)[..., None]
    ck = cum.transpose(0, 2, 1)[:, :, None, :]
    s = jnp.einsum('bqhd,bkhd->bhqk', q.astype(jnp.float32) * dh ** -0.5, k_all.astype(jnp.float32)) + cq - ck
    causal = (past_len + jnp.arange(T))[:, None] >= jnp.arange(L)[None, :]
    p = jax.nn.softmax(jnp.where(causal, s, NEG_INF), axis=-1)
    return jnp.einsum('bhqk,bkhd->bqhd', p, v_all.astype(jnp.float32)).astype(q.dtype)


def _fox_layer(x, kv, past_kv, w_qg, w_o):
    B, T, D = x.shape
    qg = x @ w_qg
    q = qg[..., :D].reshape(B, T, N_HEADS, HEAD_DIM)
    gate = jax.nn.sigmoid(qg[..., D:])
    k_new, v_new, logf_new = kv
    if past_kv is None:
        o = _fox_attn_prompt(q, k_new, v_new, logf_new)
    else:
        ck, cv, cl = past_kv
        o = _fox_attn_sample(q, jnp.concatenate([ck, k_new], axis=1),
                             jnp.concatenate([cv, v_new], axis=1),
                             jnp.concatenate([cl.astype(jnp.float32), logf_new], axis=1))
    return (o.reshape(B, T, D) * gate) @ w_o


def _group_moe(x, router_w, router_b, w_gu, w_down):
    B, T, D = x.shape
    n = B * T
    xt = x.reshape(n, D)
    scores = jax.nn.softmax((xt @ router_w).astype(jnp.float32), axis=-1)
    sel = (scores + router_b).reshape(n, N_GROUPS, EXPERTS_PER_GROUP)
    grp_score = lax.top_k(sel, TOP_K)[0].sum(-1)
    g_idx = jnp.argmax(grp_score, axis=-1)
    in_grp = sel[jnp.arange(n), g_idx]
    _, loc = lax.top_k(in_grp, TOP_K)
    eid = g_idx[:, None] * EXPERTS_PER_GROUP + loc
    gate = jnp.take_along_axis(scores, eid, axis=-1)
    gate = gate / gate.sum(-1, keepdims=True)

    A = n * TOP_K
    flat_e = eid.reshape(A)
    flat_tok = jnp.broadcast_to(jnp.arange(n)[:, None], (n, TOP_K)).reshape(A)
    flat_g = gate.reshape(A)
    order = jnp.argsort(flat_e)
    se = flat_e[order]
    counts = jnp.bincount(flat_e, length=N_EXPERTS)
    padded = (counts + E_BLOCK - 1) // E_BLOCK * E_BLOCK
    pad_end = jnp.cumsum(padded)
    pad_start = pad_end - padded
    seg_start = jnp.cumsum(counts) - counts
    dest = pad_start[se] + jnp.arange(A) - seg_start[se]
    n_blk = (A + N_EXPERTS * (E_BLOCK - 1) + E_BLOCK - 1) // E_BLOCK
    slot_tok = jnp.full((n_blk * E_BLOCK,), n, jnp.int32).at[dest].set(flat_tok[order])
    slot_w = jnp.zeros((n_blk * E_BLOCK,), jnp.float32).at[dest].set(flat_g[order])
    blk_e = jnp.minimum(jnp.searchsorted(pad_end, jnp.arange(n_blk) * E_BLOCK, side='right'), N_EXPERTS - 1)
    xpad = jnp.concatenate([xt, jnp.zeros((1, D), xt.dtype)], axis=0)
    xb = xpad[slot_tok].reshape(n_blk, E_BLOCK, D)

    def expert_block(args):
        xe, e = args
        gu = xe @ w_gu[e]
        h = jax.nn.silu(gu[:, :D_EXPERT]) * gu[:, D_EXPERT:]
        return h @ w_down[e]

    yb = lax.map(expert_block, (xb, blk_e)).reshape(n_blk * E_BLOCK, D)
    y = jnp.zeros((n + 1, D), jnp.float32).at[slot_tok].add(yb.astype(jnp.float32) * slot_w[:, None])[:n]
    return y.reshape(B, T, D).astype(x.dtype)


def _trunk(x, shift0, wkv0, past_kv, P):
    new_shift, new_wkv = [], []
    v_first = None
    kv = None
    for l in range(DEPTH):
        if l < N_A:
            vres = None if l == 0 else (P['a_v0'][l - 1], P['a_v1'][l - 1], P['a_v2'][l - 1])
            h, sh, st, v_first = _rwkv7_time_mix(
                x, shift0[l], wkv0[l], v_first, vres, P['a_mix'][l], P['a_w_rkv'][l],
                P['a_w0'][l], P['a_w1'][l], P['a_w2'][l], P['a_a0'][l], P['a_a1'][l], P['a_a2'][l],
                P['a_g1'][l], P['a_g2'][l], P['a_k_k'][l], P['a_k_a'][l], P['a_r_k'][l],
                P['a_lnx_g'][l], P['a_lnx_b'][l], P['a_w_o'][l])
            new_shift.append(sh)
            new_wkv.append(st)
        else:
            h = _fox_layer(x, kv, past_kv, P['b_w_qg'][l - N_A], P['b_w_o'][l - N_A])
        x = _layer_norm(DN_ALPHA * x + h, P['ln1_g'][l], P['ln1_b'][l])
        m = _group_moe(x, P['router_w'], P['router_b'], P['moe_w_gu'][l], P['moe_w_down'][l])
        x = _layer_norm(DN_ALPHA * x + m, P['ln2_g'][l], P['ln2_b'][l])
        if l == N_A - 1:
            kv = _shared_kv(x, P['kv_w'], P['kv_bf'])
    k_new, v_new, logf_new = kv
    return x, jnp.stack(new_shift), jnp.stack(new_wkv), k_new, v_new, logf_new


def setup_inputs(seed: int = 0) -> dict:
    key = jax.random.key(seed)
    ks = iter(jax.random.split(key, 48))
    f32 = jnp.float32
    D, H, N, E, F = D_MODEL, N_HEADS, HEAD_DIM, N_EXPERTS, D_EXPERT

    def nrm(shape, scale):
        return jax.random.normal(next(ks), shape, f32) * scale

    forget_profile = jnp.linspace(1.0, 6.0, H, dtype=f32)
    decay_profile = jnp.linspace(-6.0, -1.0, D, dtype=f32)
    rkv_scale = jnp.array([1.0, 1.0, DN_BETA], f32)[None, :, None, None] * D ** -0.5
    kv_scale = jnp.concatenate([jnp.ones((D,), f32), jnp.full((D,), DN_BETA, f32), jnp.ones((H,), f32)]) * D ** -0.5
    return {
        'x_prompt': nrm((BATCH, SEQ, D), 1.0),
        'x_sample': nrm((DEC_BATCH, DEC_SEQ, D), 1.0),
        'state_shift': nrm((N_A, DEC_BATCH, D), 1.0),
        'state_wkv': nrm((N_A, DEC_BATCH, H, N, N), 0.3),
        'cache_k': nrm((DEC_BATCH, PAST_LEN, H, N), 1.0),
        'cache_v': nrm((DEC_BATCH, PAST_LEN, H, N), DN_BETA),
        'cache_logf': jax.nn.log_sigmoid(forget_profile + nrm((DEC_BATCH, PAST_LEN, H), 1.0)),
        'ln1_g': 1.0 + nrm((DEPTH, D), 0.02),
        'ln1_b': nrm((DEPTH, D), 0.02),
        'ln2_g': 1.0 + nrm((DEPTH, D), 0.02),
        'ln2_b': nrm((DEPTH, D), 0.02),
        'a_mix': jax.random.uniform(next(ks), (N_A, 6, D), f32),
        'a_w_rkv': nrm((N_A, 3, D, D), 1.0) * rkv_scale,
        'a_w0': decay_profile + nrm((N_A, D), 0.3),
        'a_w1': nrm((N_A, D, D_DECAY_LORA), D ** -0.5),
        'a_w2': nrm((N_A, D_DECAY_LORA, D), 0.5 * D_DECAY_LORA ** -0.5),
        'a_a0': nrm((N_A, D), 0.1),
        'a_a1': nrm((N_A, D, D_AAA_LORA), D ** -0.5),
        'a_a2': nrm((N_A, D_AAA_LORA, D), 0.5 * D_AAA_LORA ** -0.5),
        'a_v0': nrm((N_A - 1, D), 0.5),
        'a_v1': nrm((N_A - 1, D, D_MV_LORA), D ** -0.5),
        'a_v2': nrm((N_A - 1, D_MV_LORA, D), 0.5 * D_MV_LORA ** -0.5),
        'a_g1': nrm((N_A, D, D_GATE_LORA), D ** -0.5),
        'a_g2': nrm((N_A, D_GATE_LORA, D), D_GATE_LORA ** -0.5),
        'a_k_k': 0.85 + nrm((N_A, D), 0.05),
        'a_k_a': 1.0 + nrm((N_A, D), 0.05),
        'a_r_k': nrm((N_A, H, N), 0.1),
        'a_lnx_g': 1.0 + nrm((N_A, D), 0.02),
        'a_lnx_b': nrm((N_A, D), 0.02),
        'a_w_o': nrm((N_A, D, D), DN_BETA * D ** -0.5),
        'kv_w': nrm((D, 2 * D + H), 1.0) * kv_scale,
        'kv_bf': forget_profile + nrm((H,), 0.1),
        'b_w_qg': nrm((N_B, D, 2 * D), D ** -0.5),
        'b_w_o': nrm((N_B, D, D), DN_BETA * D ** -0.5),
        'router_w': nrm((D, E), D ** -0.5),
        'router_b': nrm((E,), 0.01),
        'moe_w_gu': nrm((DEPTH, E, D, 2 * F), D ** -0.5),
        'moe_w_down': nrm((DEPTH, E, F, D), DN_BETA * F ** -0.5),
    }


def reference(x_prompt, x_sample, state_shift, state_wkv, cache_k, cache_v, cache_logf,
              ln1_g, ln1_b, ln2_g, ln2_b, a_mix, a_w_rkv, a_w0, a_w1, a_w2, a_a0, a_a1, a_a2,
              a_v0, a_v1, a_v2, a_g1, a_g2, a_k_k, a_k_a, a_r_k, a_lnx_g, a_lnx_b, a_w_o,
              kv_w, kv_bf, b_w_qg, b_w_o, router_w, router_b, moe_w_gu, moe_w_down):
    P = dict(ln1_g=ln1_g, ln1_b=ln1_b, ln2_g=ln2_g, ln2_b=ln2_b, a_mix=a_mix, a_w_rkv=a_w_rkv,
             a_w0=a_w0, a_w1=a_w1, a_w2=a_w2, a_a0=a_a0, a_a1=a_a1, a_a2=a_a2,
             a_v0=a_v0, a_v1=a_v1, a_v2=a_v2, a_g1=a_g1, a_g2=a_g2, a_k_k=a_k_k, a_k_a=a_k_a,
             a_r_k=a_r_k, a_lnx_g=a_lnx_g, a_lnx_b=a_lnx_b, a_w_o=a_w_o, kv_w=kv_w, kv_bf=kv_bf,
             b_w_qg=b_w_qg, b_w_o=b_w_o, router_w=router_w, router_b=router_b,
             moe_w_gu=moe_w_gu, moe_w_down=moe_w_down)
    nb = x_prompt.shape[0]
    shift0 = jnp.zeros((N_A, nb, D_MODEL), x_prompt.dtype)
    wkv0 = jnp.zeros((N_A, nb, N_HEADS, HEAD_DIM, HEAD_DIM), jnp.float32)
    y_prompt, p_shift, p_wkv, p_k, p_v, p_logf = _trunk(x_prompt, shift0, wkv0, None, P)
    y_sample, s_shift, s_wkv, s_k, s_v, s_logf = _trunk(
        x_sample, state_shift, state_wkv, (cache_k, cache_v, cache_logf), P)
    return (y_prompt, y_sample, p_shift, p_wkv, p_k, p_v, p_logf, s_shift, s_wkv, s_k, s_v, s_logf)
```

```python
import functools
import math

import jax
import jax.numpy as jnp
import numpy as np
from jax import lax
from jax.experimental import pallas as pl
from jax.experimental.pallas import tpu as pltpu

F32 = jnp.float32
BF16 = jnp.bfloat16

HEAD_DIM = 64
PAIR = 2 * HEAD_DIM
N_EXPERTS = 16
N_GROUPS = 4
EXPERTS_PER_GROUP = 4
N_CLASSES = 24
E_BLOCK = 128
LN_EPS = 1e-5
GN_EPS = 64e-5
NEG_INF = -1e30
MIB = 2 ** 20

_PAIR_LO = (0, 0, 0, 1, 1, 2)
_PAIR_HI = (1, 2, 3, 2, 3, 3)


def _params(sem, vmem_mib):
    return pltpu.CompilerParams(dimension_semantics=sem, vmem_limit_bytes=vmem_mib * MIB)


def _dot(a, b):
    return jnp.dot(a, b, preferred_element_type=F32)


def _dot_nt(a, b):
    return lax.dot_general(a, b, (((1,), (1,)), ((), ())), preferred_element_type=F32)


def _dot_tn(a, b):
    return lax.dot_general(a, b, (((0,), (0,)), ((), ())), preferred_element_type=F32)


def _split(x):
    hi = x.astype(BF16)
    lo = (x - hi.astype(F32)).astype(BF16)
    return hi, lo


def _dot3(a, b, dot=_dot):
    ah, al = a
    bh, bl = b
    return dot(ah, bh) + (dot(ah, bl) + dot(al, bh))


def _dot_exact_rhs(x, m, dot=_dot):
    hi = x.astype(BF16)
    r1 = x - hi.astype(F32)
    mid = r1.astype(BF16)
    lo = (r1 - mid.astype(F32)).astype(BF16)
    return dot(hi, m) + (dot(mid, m) + dot(lo, m))


def _head_sum(x, bd_ref):
    bd = bd_ref[...]
    parts = []
    for j in range(x.shape[1] // 256):
        parts.append(_dot_exact_rhs(x[:, 256 * j:256 * (j + 1)], bd))
    return jnp.concatenate(parts, axis=1)


def _layer_norm(z, g, b):
    mu = jnp.mean(z, axis=-1, keepdims=True)
    zc = z - mu
    var = jnp.mean(zc * zc, axis=-1, keepdims=True)
    return zc * lax.rsqrt(var + LN_EPS) * g + b


def _row_tile(n):
    for t in (256, 128, 64, 32, 16, 8):
        if n % t == 0:
            return t
    raise ValueError(n)


def _const_spec(shape):
    nd = len(shape)
    return pl.BlockSpec(shape, lambda *_: (0,) * nd)


def _rwkv_pre_kernel(has_vres, *refs):
    if has_vres:
        (x_ref, xp_ref, vf_ref, mix_ref, wr_ref, wk_ref, wv_ref, w1_ref, w2_ref, w0_ref,
         a1_ref, a2_ref, a0_ref, g1_ref, g2_ref, v1_ref, v2_ref, v0_ref, kk_ref, ka_ref, bd_ref,
         r_out, lw_out, k_out, v_out, kn_out, b_out, g_out) = refs
    else:
        (x_ref, xp_ref, mix_ref, wr_ref, wk_ref, wv_ref, w1_ref, w2_ref, w0_ref,
         a1_ref, a2_ref, a0_ref, g1_ref, g2_ref, kk_ref, ka_ref, bd_ref,
         r_out, lw_out, k_out, v_out, kn_out, b_out, g_out) = refs
    x = x_ref[...]
    xx = xp_ref[...] - x

    def mixed(i):
        return (x + xx * mix_ref[i:i + 1, :]).astype(BF16)

    xr, xw, xk, xv, xa, xg = (mixed(i) for i in range(6))
    r = _dot(xr, wr_ref[...])
    k = _dot(xk, wk_ref[...])
    v = _dot(xv, wv_ref[...])
    zw = w0_ref[...] + _dot(jnp.tanh(_dot(xw, w1_ref[...])).astype(BF16), w2_ref[...])
    lw = (-math.exp(-0.5)) * jax.nn.sigmoid(zw)
    alpha = jax.nn.sigmoid(a0_ref[...] + _dot(_dot(xa, a1_ref[...]).astype(BF16), a2_ref[...]))
    g = _dot(jax.nn.sigmoid(_dot(xg, g1_ref[...])).astype(BF16), g2_ref[...])
    if has_vres:
        gate_v = jax.nn.sigmoid(v0_ref[...] + _dot(_dot(xv, v1_ref[...]).astype(BF16), v2_ref[...]))
        v = v + (vf_ref[...] - v) * gate_v
    kkr = k * kk_ref[...]
    norm = jnp.sqrt(_head_sum(kkr * kkr, bd_ref))
    kn = kkr / jnp.maximum(norm, 1e-12)
    k2 = k * (1.0 + (alpha - 1.0) * ka_ref[...])
    r_out[...] = r
    lw_out[...] = lw
    k_out[...] = k2
    v_out[...] = v
    kn_out[...] = kn
    b_out[...] = kn * alpha
    g_out[...] = g


def _rwkv_pre(x, xp, vfirst, w, tm):
    n, d = x.shape
    has_vres = vfirst is not None
    row = pl.BlockSpec((tm, d), lambda i: (i, 0))
    ins = [x, xp] + ([vfirst] if has_vres else [])
    specs = [row, row] + ([row] if has_vres else [])
    names = ['mix', 'wr', 'wk', 'wv', 'w1', 'w2', 'w0', 'a1', 'a2', 'a0', 'g1', 'g2']
    if has_vres:
        names += ['v1', 'v2', 'v0']
    names += ['k_k', 'k_a', 'bd']
    for nm in names:
        ins.append(w[nm])
        specs.append(_const_spec(w[nm].shape))
    out = jax.ShapeDtypeStruct((n, d), F32)
    return pl.pallas_call(
        functools.partial(_rwkv_pre_kernel, has_vres),
        grid=(n // tm,),
        in_specs=specs,
        out_specs=[row] * 7,
        out_shape=[out] * 7,
        compiler_params=_params(("parallel",), 56),
        name="rwkv_pre",
    )(*ins)


def _wkv_chunk_kernel(C, n_pairs, r_ref, lw_ref, k_ref, v_ref, kn_ref, b_ref,
                      rp_out, yp_out, m_out, n_out):
    C2 = 2 * C
    row = lax.broadcasted_iota(jnp.int32, (C2, C2), 0)
    col = lax.broadcasted_iota(jnp.int32, (C2, C2), 1)
    same = (row >= C) == (col >= C)
    strict = jnp.logical_and(same, col < row)
    incl = jnp.logical_and(same, col <= row)
    tri_incl = jnp.where(incl, 1.0, 0.0).astype(BF16)
    eye_c = jnp.where(row == col, 1.0, 0.0).astype(F32)
    r128 = lax.broadcasted_iota(jnp.int32, (PAIR, PAIR), 0)
    c128 = lax.broadcasted_iota(jnp.int32, (PAIR, PAIR), 1)
    eye_p = r128 == c128
    head0 = lax.broadcasted_iota(jnp.int32, (1, PAIR), 1) < HEAD_DIM

    def stack(t):
        return jnp.concatenate([jnp.where(head0, t, 0.0), jnp.where(head0, 0.0, t)], axis=0)

    n_sq = int(math.log2(C)) - 1
    for p in range(n_pairs):
        sl = slice(PAIR * p, PAIR * (p + 1))
        lws = stack(lw_ref[:, sl])
        L = _dot_exact_rhs_left(tri_incl, lws)
        lc = L[C - 1:C, :] + L[C2 - 1:C2, :]
        e_l = jnp.exp(L)
        kn = stack(kn_ref[:, sl])
        bs = stack(b_ref[:, sl])
        ks = stack(k_ref[:, sl])
        at = _split(-kn * jnp.exp(L - lws))
        rt_f = stack(r_ref[:, sl]) * e_l
        rt = _split(rt_f)
        e_nl = jnp.exp(-L)
        bb = _split(bs * e_nl)
        kb = _split(ks * e_nl)
        e_lc = jnp.exp(lc - L)
        bh = _split(bs * e_lc)
        kh = _split(ks * e_lc)
        vs = _split(stack(v_ref[:, sl]))

        a_ab = jnp.where(strict, _dot3(at, bb, _dot_nt), 0.0)
        a_ak = _split(jnp.where(strict, _dot3(at, kb, _dot_nt), 0.0))
        a_rb = _split(jnp.where(incl, _dot3(rt, bb, _dot_nt), 0.0))
        a_rk = _split(jnp.where(incl, _dot3(rt, kb, _dot_nt), 0.0))

        pw = a_ab
        tm = eye_c + pw
        for _ in range(n_sq):
            pws = _split(pw)
            pw = _dot3(pws, pws)
            tm = tm + _dot3(_split(pw), _split(tm))
        tms = _split(tm)
        at2 = _split(_dot3(tms, at))
        v2 = _split(_dot3(tms, _split(_dot3(a_ak, vs))))
        r2 = rt_f + _dot3(a_rb, at2)
        y2 = _dot3(a_rb, v2) + _dot3(a_rk, vs)
        rp_out[:, sl] = r2[:C, :] + r2[C:, :]
        yp_out[:, sl] = y2[:C, :] + y2[C:, :]
        m_out[0, p] = jnp.where(eye_p, jnp.exp(lc), 0.0) + _dot3(bh, at2, _dot_tn)
        n_out[0, p] = _dot3(bh, v2, _dot_tn) + _dot3(kh, vs, _dot_tn)


def _dot_exact_rhs_left(m, x):
    hi = x.astype(BF16)
    r1 = x - hi.astype(F32)
    mid = r1.astype(BF16)
    lo = (r1 - mid.astype(F32)).astype(BF16)
    return _dot(m, hi) + (_dot(m, mid) + _dot(m, lo))


def _wkv_chunks(arrs, C, row0, n_chunks, pairs_per_step):
    d = arrs[0].shape[1]
    n_pairs = d // PAIR
    assert row0 % C == 0 and n_pairs % pairs_per_step == 0
    blk0 = row0 // C
    lanes = pairs_per_step * PAIR
    in_spec = pl.BlockSpec((C, lanes), lambda c, q: (blk0 + c, q))
    out_row = pl.BlockSpec((C, lanes), lambda c, q: (c, q))
    out_mat = pl.BlockSpec((1, pairs_per_step, PAIR, PAIR), lambda c, q: (c, q, 0, 0))
    t = n_chunks * C
    return pl.pallas_call(
        functools.partial(_wkv_chunk_kernel, C, pairs_per_step),
        grid=(n_chunks, n_pairs // pairs_per_step),
        in_specs=[in_spec] * 6,
        out_specs=[out_row, out_row, out_mat, out_mat],
        out_shape=[jax.ShapeDtypeStruct((t, d), F32), jax.ShapeDtypeStruct((t, d), F32),
                   jax.ShapeDtypeStruct((n_chunks, n_pairs, PAIR, PAIR), F32),
                   jax.ShapeDtypeStruct((n_chunks, n_pairs, PAIR, PAIR), F32)],
        compiler_params=_params(("parallel", "parallel"), 32),
        name=f"wkv_chunks_c{C}",
    )(*arrs)


def _wkv_seq_kernel(n_pairs, n_steps, rp_ref, yp_ref, m_ref, n_ref, s0_ref, y_out, s_out, s_scr):
    j = pl.program_id(1)

    @pl.when(j == 0)
    def _():
        s_scr[...] = s0_ref[0]

    for p in range(n_pairs):
        sl = slice(PAIR * p, PAIR * (p + 1))
        ss = _split(s_scr[p])
        y_out[:, sl] = _dot3(_split(rp_ref[:, sl]), ss) + yp_ref[:, sl]
        s_scr[p] = _dot3(_split(m_ref[0, p]), ss) + n_ref[0, p]

    @pl.when(j == n_steps - 1)
    def _():
        s_out[0] = s_scr[...]


def _wkv_seq(rp, yp, m, nn, s0, C, n_seq, n_steps):
    t, d = rp.shape
    n_pairs = d // PAIR
    row = pl.BlockSpec((C, d), lambda s, j: (s * n_steps + j, 0))
    mat = pl.BlockSpec((1, n_pairs, PAIR, PAIR), lambda s, j: (s * n_steps + j, 0, 0, 0))
    st = pl.BlockSpec((1, n_pairs, PAIR, PAIR), lambda s, j: (s, 0, 0, 0))
    return pl.pallas_call(
        functools.partial(_wkv_seq_kernel, n_pairs, n_steps),
        grid=(n_seq, n_steps),
        in_specs=[row, row, mat, mat, st],
        out_specs=[row, st],
        out_shape=[jax.ShapeDtypeStruct((t, d), F32),
                   jax.ShapeDtypeStruct((n_seq, n_pairs, PAIR, PAIR), F32)],
        scratch_shapes=[pltpu.VMEM((n_pairs, PAIR, PAIR), F32)],
        compiler_params=_params(("arbitrary", "arbitrary"), 32),
        name=f"wkv_seq_c{C}",
    )(rp, yp, m, nn, s0)


def _state_to_blockdiag(s):
    b, h, n, _ = s.shape
    st = jnp.swapaxes(s, -1, -2).reshape(b, h // 2, 2, n, n)
    z = jnp.zeros_like(st[:, :, 0])
    top = jnp.concatenate([st[:, :, 0], z], axis=-1)
    bot = jnp.concatenate([z, st[:, :, 1]], axis=-1)
    return jnp.concatenate([top, bot], axis=-2)


def _blockdiag_to_state(bd):
    b, hp, _, _ = bd.shape
    n = HEAD_DIM
    s0 = bd[:, :, :n, :n]
    s1 = bd[:, :, n:, n:]
    st = jnp.stack([s0, s1], axis=2).reshape(b, hp * 2, n, n)
    return jnp.swapaxes(st, -1, -2)


def _route(x1, rw_ref, rb_ref):
    logits = _dot3(_split(rw_ref[...]), _split(x1), _dot_nt)
    mx = jnp.max(logits, axis=0, keepdims=True)
    ex = jnp.exp(logits - mx)
    scores = ex / jnp.sum(ex, axis=0, keepdims=True)
    sel = scores + rb_ref[...]
    rows = [sel[e:e + 1, :] for e in range(N_EXPERTS)]
    srow = [scores[e:e + 1, :] for e in range(N_EXPERTS)]

    def top2(vals):
        m1 = jnp.maximum(jnp.maximum(vals[0], vals[1]), jnp.maximum(vals[2], vals[3]))
        i1 = jnp.where(vals[0] == m1, 0, jnp.where(vals[1] == m1, 1, jnp.where(vals[2] == m1, 2, 3)))
        rest = [jnp.where(i1 == j, -jnp.inf, vals[j]) for j in range(4)]
        m2 = jnp.maximum(jnp.maximum(rest[0], rest[1]), jnp.maximum(rest[2], rest[3]))
        i2 = jnp.where(rest[0] == m2, 0, jnp.where(rest[1] == m2, 1, jnp.where(rest[2] == m2, 2, 3)))
        return m1, i1, m2, i2

    gscore = []
    for gidx in range(N_GROUPS):
        m1, _, m2, _ = top2(rows[4 * gidx:4 * gidx + 4])
        gscore.append(m1 + m2)
    gm = jnp.maximum(jnp.maximum(gscore[0], gscore[1]), jnp.maximum(gscore[2], gscore[3]))
    gi = jnp.where(gscore[0] == gm, 0, jnp.where(gscore[1] == gm, 1, jnp.where(gscore[2] == gm, 2, 3)))

    def pick(rws, j):
        return jnp.where(gi == 0, rws[j], jnp.where(gi == 1, rws[4 + j],
                                                    jnp.where(gi == 2, rws[8 + j], rws[12 + j])))

    in_grp = [pick(rows, j) for j in range(4)]
    sc_grp = [pick(srow, j) for j in range(4)]
    _, i1, _, i2 = top2(in_grp)

    def at(vals, idx):
        return jnp.where(idx == 0, vals[0], jnp.where(idx == 1, vals[1],
                                                      jnp.where(idx == 2, vals[2], vals[3])))

    ga = at(sc_grp, i1)
    gb = at(sc_grp, i2)
    tot = ga + gb
    ga = ga / tot
    gb = gb / tot
    lo = jnp.minimum(i1, i2)
    hi = jnp.maximum(i1, i2)
    g_lo = jnp.where(i1 < i2, ga, gb)
    g_hi = jnp.where(i1 < i2, gb, ga)
    pair = jnp.where(lo == 0, hi - 1, jnp.where(lo == 1, hi + 1, 5))
    cls = gi * 6 + pair
    return cls.astype(jnp.int32), g_lo, g_hi


def _mix_post_kernel(is_rwkv, alpha_dn, *refs):
    if is_rwkv:
        (x_ref, y_ref, r_ref, k_ref, v_ref, g_ref, lg_ref, lb_ref, rk_ref, bd_ref,
         wo_ref, n1g_ref, n1b_ref, rw_ref, rb_ref, x1_out, cls_out, gate_out) = refs
        y = y_ref[...]
        mu = _head_sum(y, bd_ref) * (1.0 / HEAD_DIM)
        yc = y - mu
        var = _head_sum(yc * yc, bd_ref) * (1.0 / HEAD_DIM)
        yn = yc * lax.rsqrt(var + GN_EPS) * lg_ref[...] + lb_ref[...]
        v = v_ref[...]
        bonus = _head_sum(r_ref[...] * k_ref[...] * rk_ref[...], bd_ref)
        z = (yn + bonus * v) * g_ref[...]
    else:
        (x_ref, o_ref, gt_ref, wo_ref, n1g_ref, n1b_ref, rw_ref, rb_ref,
         x1_out, cls_out, gate_out) = refs
        z = o_ref[...] * gt_ref[...]
    h = _dot(z.astype(BF16), wo_ref[...])
    x1 = _layer_norm(alpha_dn * x_ref[...] + h, n1g_ref[...], n1b_ref[...])
    x1_out[...] = x1
    cls, g_lo, g_hi = _route(x1, rw_ref, rb_ref)
    tm = x1.shape[0]
    cls_out[...] = jnp.broadcast_to(cls, (8, tm))
    gate_out[...] = jnp.concatenate([g_lo, g_hi, jnp.zeros((6, tm), F32)], axis=0)


def _mix_post(is_rwkv, alpha_dn, acts, consts, tm):
    n, d = acts[0].shape
    row = pl.BlockSpec((tm, d), lambda i: (i, 0))
    specs = [row] * len(acts) + [_const_spec(c.shape) for c in consts]
    lane = pl.BlockSpec((8, tm), lambda i: (0, i))
    return pl.pallas_call(
        functools.partial(_mix_post_kernel, is_rwkv, alpha_dn),
        grid=(n // tm,),
        in_specs=specs,
        out_specs=[row, lane, lane],
        out_shape=[jax.ShapeDtypeStruct((n, d), F32), jax.ShapeDtypeStruct((8, n), jnp.int32),
                   jax.ShapeDtypeStruct((8, n), F32)],
        compiler_params=_params(("parallel",), 48),
        name="mix_post_rwkv" if is_rwkv else "mix_post_fox",
    )(*acts, *consts)


def _moe_kernel(n_tok, d_exp, e1_ref, e2_ref, valid_ref, tok_ref, x_hbm, glo_ref, ghi_ref,
                wgu1_ref, wd1_ref, wgu2_ref, wd2_ref, out_hbm, xbuf, obuf, gsem, ssem):
    b = pl.program_id(0)
    base = b * E_BLOCK

    def gather_copy(i):
        tok = jnp.minimum(tok_ref[base + i], n_tok - 1)
        return pltpu.make_async_copy(x_hbm.at[pl.ds(tok, 1)], xbuf.at[pl.ds(i, 1)], gsem)

    def scatter_copy(tok, i):
        return pltpu.make_async_copy(obuf.at[pl.ds(i, 1)], out_hbm.at[pl.ds(tok, 1)], ssem)

    @pl.when(valid_ref[b] != 0)
    def _():
        def start_g(i, c):
            gather_copy(i).start()
            return c

        def wait_g(i, c):
            gather_copy(i).wait()
            return c

        lax.fori_loop(0, E_BLOCK, start_g, 0)
        lax.fori_loop(0, E_BLOCK, wait_g, 0)
        xb = xbuf[...].astype(BF16)

        def expert(wgu_ref, wd_ref):
            gu = _dot(xb, wgu_ref[0])
            hmid = jax.nn.silu(gu[:, :d_exp]) * gu[:, d_exp:]
            return _dot(hmid.astype(BF16), wd_ref[0])

        y1 = expert(wgu1_ref, wd1_ref)
        y2 = expert(wgu2_ref, wd2_ref)
        obuf[...] = y1 * glo_ref[...] + y2 * ghi_ref[...]

        def start_s(i, c):
            tok = tok_ref[base + i]

            @pl.when(tok < n_tok)
            def _():
                scatter_copy(tok, i).start()
            return c

        def wait_s(i, c):
            tok = tok_ref[base + i]

            @pl.when(tok < n_tok)
            def _():
                scatter_copy(tok, i).wait()
            return c

        lax.fori_loop(0, E_BLOCK, start_s, 0)
        lax.fori_loop(0, E_BLOCK, wait_s, 0)


def _moe(x1, cls, g_lo, g_hi, w_gu, w_down):
    n, d = x1.shape
    d_exp = w_down.shape[1]
    n_blk = (n + N_CLASSES * (E_BLOCK - 1) + E_BLOCK - 1) // E_BLOCK
    n_slot = n_blk * E_BLOCK
    counts = jnp.bincount(cls, length=N_CLASSES)
    padded = (counts + E_BLOCK - 1) // E_BLOCK * E_BLOCK
    pad_end = jnp.cumsum(padded)
    pad_start = pad_end - padded
    seg_start = jnp.cumsum(counts) - counts
    order = jnp.argsort(cls, stable=True).astype(jnp.int32)
    sc = cls[order]
    dest = pad_start[sc] + jnp.arange(n, dtype=jnp.int32) - seg_start[sc]
    slot_tok = jnp.full((n_slot,), n, jnp.int32).at[dest].set(order)
    slot_lo = jnp.zeros((n_slot,), F32).at[dest].set(g_lo[order]).reshape(n_slot, 1)
    slot_hi = jnp.zeros((n_slot,), F32).at[dest].set(g_hi[order]).reshape(n_slot, 1)
    blk_start = jnp.arange(n_blk, dtype=jnp.int32) * E_BLOCK
    blk_cls = jnp.minimum(jnp.searchsorted(pad_end, blk_start, side='right'), N_CLASSES - 1).astype(jnp.int32)
    blk_valid = (blk_start < pad_end[-1]).astype(jnp.int32)
    last_cls = blk_cls[jnp.maximum(jnp.sum(blk_valid) - 1, 0)]
    blk_cls = jnp.where(blk_valid != 0, blk_cls, last_cls)
    pair_lo = jnp.asarray(_PAIR_LO, jnp.int32)
    pair_hi = jnp.asarray(_PAIR_HI, jnp.int32)
    blk_e1 = (blk_cls // 6) * EXPERTS_PER_GROUP + pair_lo[blk_cls % 6]
    blk_e2 = (blk_cls // 6) * EXPERTS_PER_GROUP + pair_hi[blk_cls % 6]

    gate_spec = pl.BlockSpec((E_BLOCK, 1), lambda b, e1, e2, vl, tk: (b, 0))
    gu1 = pl.BlockSpec((1, d, 2 * d_exp), lambda b, e1, e2, vl, tk: (e1[b], 0, 0))
    dn1 = pl.BlockSpec((1, d_exp, d), lambda b, e1, e2, vl, tk: (e1[b], 0, 0))
    gu2 = pl.BlockSpec((1, d, 2 * d_exp), lambda b, e1, e2, vl, tk: (e2[b], 0, 0))
    dn2 = pl.BlockSpec((1, d_exp, d), lambda b, e1, e2, vl, tk: (e2[b], 0, 0))
    grid_spec = pltpu.PrefetchScalarGridSpec(
        num_scalar_prefetch=4,
        grid=(n_blk,),
        in_specs=[pl.BlockSpec(memory_space=pl.ANY), gate_spec, gate_spec, gu1, dn1, gu2, dn2],
        out_specs=pl.BlockSpec(memory_space=pl.ANY),
        scratch_shapes=[pltpu.VMEM((E_BLOCK, d), F32), pltpu.VMEM((E_BLOCK, d), F32),
                        pltpu.SemaphoreType.DMA(()), pltpu.SemaphoreType.DMA(())],
    )
    return pl.pallas_call(
        functools.partial(_moe_kernel, n, d_exp),
        grid_spec=grid_spec,
        out_shape=jax.ShapeDtypeStruct((n, d), F32),
        compiler_params=_params(("arbitrary",), 40),
        name="moe_experts",
    )(blk_e1, blk_e2, blk_valid, slot_tok, x1, slot_lo, slot_hi, w_gu, w_down, w_gu, w_down)


def _ln2_kernel(alpha_dn, with_kv, *refs):
    if with_kv:
        (x1_ref, m_ref, g_ref, b_ref, wk_ref, wv_ref, wf_ref, bf_ref, tri_ref,
         x2_out, k_out, v_out, kb_out, vb_out, lf_out, cum_out, carry) = refs
    else:
        x1_ref, m_ref, g_ref, b_ref, x2_out = refs
    x2 = _layer_norm(alpha_dn * x1_ref[...] + m_ref[...], g_ref[...], b_ref[...])
    x2_out[...] = x2
    if with_kv:
        xb = x2.astype(BF16)
        k = _dot(xb, wk_ref[...])
        v = _dot(xb, wv_ref[...])
        k_out[...] = k
        v_out[...] = v
        kb_out[...] = k.astype(BF16)
        vb_out[...] = v.astype(BF16)
        z = _dot3(_split(wf_ref[...]), _split(x2), _dot_nt) + bf_ref[...]
        lf = jnp.minimum(z, 0.0) - jnp.log1p(jnp.exp(-jnp.abs(z)))
        lf_out[...] = lf

        @pl.when(pl.program_id(0) == 0)
        def _():
            carry[...] = jnp.zeros_like(carry)

        cum = _dot_exact_rhs(lf, tri_ref[...]) + carry[...]
        cum_out[...] = cum
        carry[...] = cum[:, -1:]


def _ln2(alpha_dn, x1, m, g, b, kv, tm):
    n, d = x1.shape
    row = pl.BlockSpec((tm, d), lambda i: (i, 0))
    with_kv = kv is not None
    ins = [x1, m, g, b]
    specs = [row, row, _const_spec(g.shape), _const_spec(b.shape)]
    outs = [jax.ShapeDtypeStruct((n, d), F32)]
    ospecs = [row]
    scratch = []
    if with_kv:
        nh = kv['wf_t'].shape[0]
        tri = (jnp.arange(tm)[:, None] <= jnp.arange(tm)[None, :]).astype(BF16)
        extra = [kv['wk'], kv['wv'], kv['wf_t'], kv['bf'], tri]
        ins += extra
        specs += [_const_spec(e.shape) for e in extra]
        lane = pl.BlockSpec((nh, tm), lambda i: (0, i))
        outs += [jax.ShapeDtypeStruct((n, d), F32), jax.ShapeDtypeStruct((n, d), F32),
                 jax.ShapeDtypeStruct((n, d), BF16), jax.ShapeDtypeStruct((n, d), BF16),
                 jax.ShapeDtypeStruct((nh, n), F32), jax.ShapeDtypeStruct((nh, n), F32)]
        ospecs += [row, row, row, row, lane, lane]
        scratch = [pltpu.VMEM((nh, 1), F32)]
    res = pl.pallas_call(
        functools.partial(_ln2_kernel, alpha_dn, with_kv),
        grid=(n // tm,),
        in_specs=specs,
        out_specs=ospecs,
        out_shape=outs,
        scratch_shapes=scratch,
        compiler_params=_params(("arbitrary",), 48),
        name="ln2_kv" if with_kv else "ln2",
    )(*ins)
    return res


def _fox_qg_kernel(x_ref, wq_ref, wg_ref, q_out, gate_out):
    xb = x_ref[...].astype(BF16)
    q_out[...] = (_dot(xb, wq_ref[...]) * (HEAD_DIM ** -0.5)).astype(BF16)
    gate_out[...] = jax.nn.sigmoid(_dot(xb, wg_ref[...]))


def _fox_qg(x, wq, wg, tm):
    n, d = x.shape
    row = pl.BlockSpec((tm, d), lambda i: (i, 0))
    return pl.pallas_call(
        _fox_qg_kernel,
        grid=(n // tm,),
        in_specs=[row, _const_spec(wq.shape), _const_spec(wg.shape)],
        out_specs=[row, row],
        out_shape=[jax.ShapeDtypeStruct((n, d), BF16), jax.ShapeDtypeStruct((n, d), F32)],
        compiler_params=_params(("parallel",), 40),
        name="fox_qg",
    )(x, wq, wg)


def _fox_prompt_kernel(tq, q_ref, k_ref, v_ref, cumc_ref, cumr_ref, o_out):
    p = pl.program_id(0)
    i = pl.program_id(1)
    lane = lax.broadcasted_iota(jnp.int32, (1, PAIR), 1)
    head0 = lane < HEAD_DIM
    q = q_ref[...]
    zero = jnp.zeros_like(q)
    qs = (jnp.where(head0, q, zero), jnp.where(head0, zero, q))
    hl = lax.broadcasted_iota(jnp.int32, (1, cumc_ref.shape[1]), 1)
    cumc = cumc_ref[...]
    cq = tuple(jnp.sum(jnp.where(hl == 2 * p + h, cumc, 0.0), axis=1, keepdims=True) for h in range(2))
    rr = lax.broadcasted_iota(jnp.int32, (tq, tq), 0)
    cc = lax.broadcasted_iota(jnp.int32, (tq, tq), 1)
    causal = rr >= cc

    def block(j, carry, diag):
        k0 = pl.multiple_of(j * tq, tq)
        kc = k_ref[pl.ds(k0, tq), :]
        vc = v_ref[pl.ds(k0, tq), :]
        new = []
        for h in range(2):
            m, l, acc = carry[h]
            ck = cumr_ref[pl.ds(2 * p + h, 1), pl.ds(k0, tq)]
            s = _dot_nt(qs[h], kc) + cq[h] - ck
            if diag:
                s = jnp.where(causal, s, NEG_INF)
            m_new = jnp.maximum(m, jnp.max(s, axis=1, keepdims=True))
            pr = jnp.exp(s - m_new)
            corr = jnp.exp(m - m_new)
            l_new = l * corr + jnp.sum(pr, axis=1, keepdims=True)
            acc_new = acc * corr + _dot(pr.astype(BF16), vc)
            new.append((m_new, l_new, acc_new))
        return tuple(new)

    init = tuple((jnp.full((tq, 1), NEG_INF, F32), jnp.zeros((tq, 1), F32), jnp.zeros((tq, PAIR), F32))
                 for _ in range(2))
    carry = lax.fori_loop(0, i, lambda j, c: block(j, c, False), init)
    carry = block(i, carry, True)
    o0 = carry[0][2] / carry[0][1]
    o1 = carry[1][2] / carry[1][1]
    o_out[...] = jnp.where(head0, o0, o1)


def _fox_prompt(qb, kb, vb, cum_c, cum_r, t, tq):
    d = qb.shape[1]
    n_pairs = d // PAIR
    nh = cum_r.shape[0]
    return pl.pallas_call(
        functools.partial(_fox_prompt_kernel, tq),
        grid=(n_pairs, t // tq),
        in_specs=[pl.BlockSpec((tq, PAIR), lambda p, i: (i, p)),
                  pl.BlockSpec((t, PAIR), lambda p, i: (0, p)),
                  pl.BlockSpec((t, PAIR), lambda p, i: (0, p)),
                  pl.BlockSpec((tq, nh), lambda p, i: (i, 0)),
                  pl.BlockSpec((nh, t), lambda p, i: (0, 0))],
        out_specs=pl.BlockSpec((tq, PAIR), lambda p, i: (i, p)),
        out_shape=jax.ShapeDtypeStruct((t, d), F32),
        compiler_params=_params(("parallel", "parallel"), 48),
        name="fox_prompt_attn",
    )(qb, kb, vb, cum_c, cum_r)


def _fox_sample_kernel(n_pairs, q_ref, kn_ref, vn_ref, lfn_ref, kc_ref, vc_ref, lfc_ref,
                       triu_ref, o_out):
    t = q_ref.shape[0]
    past = kc_ref.shape[1]
    lfc = lfc_ref[0]
    lfn = lfn_ref[0]
    nh = lfc.shape[1]
    lfc_s = _split3(lfc)
    cum_c_r = _dot_tn3(lfc_s, triu_ref[...])
    ones_row = jnp.ones((1, past), BF16)
    tot_r = _dot_m3(ones_row, lfc_s)
    tot_c = cum_c_r[:, past - 1:past]
    rr = lax.broadcasted_iota(jnp.int32, (t, t), 0)
    cc = lax.broadcasted_iota(jnp.int32, (t, t), 1)
    causal = rr >= cc
    tril = jnp.where(causal, 1.0, 0.0).astype(BF16)
    lfn_s = _split3(lfn)
    cum_n_c = _dot_m3(tril, lfn_s) + tot_r
    cum_n_r = _dot_tn3(lfn_s, jnp.where(rr <= cc, 1.0, 0.0).astype(BF16)) + tot_c
    lane = lax.broadcasted_iota(jnp.int32, (1, PAIR), 1)
    head0 = lane < HEAD_DIM
    hl = lax.broadcasted_iota(jnp.int32, (1, nh), 1)
    for p in range(n_pairs):
        sl = slice(PAIR * p, PAIR * (p + 1))
        q = q_ref[:, sl]
        zero = jnp.zeros_like(q)
        kc = kc_ref[0, :, sl].astype(BF16)
        vc = vc_ref[0, :, sl].astype(BF16)
        kn = kn_ref[:, sl].astype(BF16)
        vn = vn_ref[:, sl].astype(BF16)
        outs = []
        for h in range(2):
            hh = 2 * p + h
            qh = jnp.where(head0, q, zero) if h == 0 else jnp.where(head0, zero, q)
            cq = jnp.sum(jnp.where(hl == hh, cum_n_c, 0.0), axis=1, keepdims=True)
            s_c = _dot_nt(qh, kc) + cq - cum_c_r[hh:hh + 1, :]
            s_n = _dot_nt(qh, kn) + cq - cum_n_r[hh:hh + 1, :]
            s_n = jnp.where(causal, s_n, NEG_INF)
            m = jnp.maximum(jnp.max(s_c, axis=1, keepdims=True), jnp.max(s_n, axis=1, keepdims=True))
            p_c = jnp.exp(s_c - m)
            p_n = jnp.exp(s_n - m)
            l = jnp.sum(p_c, axis=1, keepdims=True) + jnp.sum(p_n, axis=1, keepdims=True)
            acc = _dot(p_c.astype(BF16), vc) + _dot(p_n.astype(BF16), vn)
            outs.append(acc / l)
        o_out[:, sl] = jnp.where(head0, outs[0], outs[1])


def _split3(x):
    hi = x.astype(BF16)
    r1 = x - hi.astype(F32)
    mid = r1.astype(BF16)
    lo = (r1 - mid.astype(F32)).astype(BF16)
    return hi, mid, lo


def _dot_tn3(xs, m):
    return _dot_tn(xs[0], m) + (_dot_tn(xs[1], m) + _dot_tn(xs[2], m))


def _dot_m3(m, xs):
    return _dot(m, xs[0]) + (_dot(m, xs[1]) + _dot(m, xs[2]))


def _fox_sample(qb, k, v, lf_new, cache_k, cache_v, cache_lf, row0, n_stream, t):
    d = qb.shape[1]
    n_pairs = d // PAIR
    past = cache_k.shape[1]
    nh = cache_lf.shape[2]
    blk0 = row0 // t
    triu = (jnp.arange(past)[:, None] <= jnp.arange(past)[None, :]).astype(BF16)
    row = pl.BlockSpec((t, d), lambda b: (blk0 + b, 0))
    return pl.pallas_call(
        functools.partial(_fox_sample_kernel, n_pairs),
        grid=(n_stream,),
        in_specs=[row, row, row,
                  pl.BlockSpec((1, t, nh), lambda b: (b, 0, 0)),
                  pl.BlockSpec((1, past, d), lambda b: (b, 0, 0)),
                  pl.BlockSpec((1, past, d), lambda b: (b, 0, 0)),
                  pl.BlockSpec((1, past, nh), lambda b: (b, 0, 0)),
                  _const_spec(triu.shape)],
        out_specs=pl.BlockSpec((t, d), lambda b: (b, 0)),
        out_shape=jax.ShapeDtypeStruct((n_stream * t, d), F32),
        compiler_params=_params(("parallel",), 48),
        name="fox_sample_attn",
    )(qb, k, v, lf_new, cache_k, cache_v, cache_lf, triu)


def kernel(x_prompt, x_sample, state_shift, state_wkv, cache_k, cache_v, cache_logf, ln1_g, ln1_b, ln2_g, ln2_b, a_mix, a_w_rkv, a_w0, a_w1, a_w2, a_a0, a_a1, a_a2, a_v0, a_v1, a_v2, a_g1, a_g2, a_k_k, a_k_a, a_r_k, a_lnx_g, a_lnx_b, a_w_o, kv_w, kv_bf, b_w_qg, b_w_o, router_w, router_b, moe_w_gu, moe_w_down):
    nb, seq, d = x_prompt.shape
    db, dt, _ = x_sample.shape
    assert nb == 1
    n_heads = d // HEAD_DIM
    depth = ln1_g.shape[0]
    n_a = a_mix.shape[0]
    past = cache_k.shape[1]
    t_p = nb * seq
    t_s = db * dt
    n = t_p + t_s
    tm = _row_tile(n)
    c_p = 64
    c_s = dt
    assert seq % c_p == 0 and t_p % c_s == 0 and (c_s & (c_s - 1)) == 0
    alpha_dn = (2 * depth) ** 0.25
    row2 = lambda a: a.reshape(1, -1)

    bd = (jnp.arange(256)[:, None] // HEAD_DIM == jnp.arange(256)[None, :] // HEAD_DIM).astype(BF16)
    rw_t = router_w.T
    rb_c = router_b.reshape(-1, 1)

    x = jnp.concatenate([x_prompt.reshape(t_p, d), x_sample.reshape(t_s, d)], axis=0)
    new_shift_p, new_shift_s, new_wkv_p, new_wkv_s = [], [], [], []
    v_first = None
    kv = None
    for l in range(depth):
        if l < n_a:
            xs = x[t_p:].reshape(db, dt, d)
            new_shift_p.append(x[t_p - 1:t_p].reshape(nb, d))
            new_shift_s.append(xs[:, -1])
            xp = jnp.concatenate([
                jnp.zeros((1, d), F32), x[:t_p - 1],
                jnp.concatenate([state_shift[l][:, None, :], xs[:, :-1]], axis=1).reshape(t_s, d)], axis=0)
            w = dict(mix=jnp.concatenate([a_mix[l], jnp.zeros((2, d), F32)], axis=0),
                     wr=a_w_rkv[l, 0].astype(BF16), wk=a_w_rkv[l, 1].astype(BF16), wv=a_w_rkv[l, 2].astype(BF16),
                     w1=a_w1[l].astype(BF16), w2=a_w2[l].astype(BF16), w0=row2(a_w0[l]),
                     a1=a_a1[l].astype(BF16), a2=a_a2[l].astype(BF16), a0=row2(a_a0[l]),
                     g1=a_g1[l].astype(BF16), g2=a_g2[l].astype(BF16),
                     k_k=row2(a_k_k[l]), k_a=row2(a_k_a[l]), bd=bd)
            if l > 0:
                w.update(v1=a_v1[l - 1].astype(BF16), v2=a_v2[l - 1].astype(BF16), v0=row2(a_v0[l - 1]))
            r, lw, k, v, kn, b, g = _rwkv_pre(x, xp, v_first if l > 0 else None, w, tm)
            if l == 0:
                v_first = v
            scan_in = (r, lw, k, v, kn, b)
            rp, yp, mm, nn = _wkv_chunks(scan_in, c_p, 0, t_p // c_p, 4)
            s0 = jnp.zeros((1, n_heads // 2, PAIR, PAIR), F32)
            y_p, sf_p = _wkv_seq(rp, yp, mm, nn, s0, c_p, 1, t_p // c_p)
            rp, yp, mm, nn = _wkv_chunks(scan_in, c_s, t_p, db, 4)
            y_s, sf_s = _wkv_seq(rp, yp, mm, nn, _state_to_blockdiag(state_wkv[l]), c_s, db, 1)
            new_wkv_p.append(_blockdiag_to_state(sf_p))
            new_wkv_s.append(_blockdiag_to_state(sf_s))
            y = jnp.concatenate([y_p, y_s], axis=0)
            consts = [row2(a_lnx_g[l]), row2(a_lnx_b[l]), row2(a_r_k[l]), bd, a_w_o[l].astype(BF16),
                      row2(ln1_g[l]), row2(ln1_b[l]), rw_t, rb_c]
            x1, cls, gates = _mix_post(True, alpha_dn, [x, y, r, k, v, g], consts, tm)
        else:
            lb = l - n_a
            qb, gate = _fox_qg(x, b_w_qg[lb][:, :d].astype(BF16), b_w_qg[lb][:, d:].astype(BF16), tm)
            o_p = _fox_prompt(qb, kv['kb'], kv['vb'], kv['cum_c'], kv['cum_r'], t_p, 256)
            o_s = _fox_sample(qb, kv['k'], kv['v'], kv['lf_s'], cache_k.reshape(db, past, d),
                              cache_v.reshape(db, past, d), cache_logf, t_p, db, dt)
            o = jnp.concatenate([o_p, o_s], axis=0)
            consts = [b_w_o[lb].astype(BF16), row2(ln1_g[l]), row2(ln1_b[l]), rw_t, rb_c]
            x1, cls, gates = _mix_post(False, alpha_dn, [x, o, gate], consts, tm)
        m = _moe(x1, cls[0], gates[0], gates[1], moe_w_gu[l].astype(BF16), moe_w_down[l].astype(BF16))
        if l == n_a - 1:
            kvw = dict(wk=kv_w[:, :d].astype(BF16), wv=kv_w[:, d:2 * d].astype(BF16),
                       wf_t=kv_w[:, 2 * d:].T, bf=kv_bf.reshape(-1, 1))
            x, k_all, v_all, kb_all, vb_all, lf_r, cum_r = _ln2(
                alpha_dn, x1, m, row2(ln2_g[l]), row2(ln2_b[l]), kvw, tm)
            kv = dict(k=k_all, v=v_all, kb=kb_all, vb=vb_all, cum_r=cum_r[:, :t_p],
                      cum_c=cum_r[:, :t_p].T, lf_p=lf_r[:, :t_p].T, lf_s=lf_r[:, t_p:].T.reshape(db, dt, n_heads))
        else:
            x = _ln2(alpha_dn, x1, m, row2(ln2_g[l]), row2(ln2_b[l]), None, tm)[0]

    y_prompt = x[:t_p].reshape(nb, seq, d)
    y_sample = x[t_p:].reshape(db, dt, d)
    p_k = kv['k'][:t_p].reshape(nb, seq, n_heads, HEAD_DIM)
    p_v = kv['v'][:t_p].reshape(nb, seq, n_heads, HEAD_DIM)
    s_k = kv['k'][t_p:].reshape(db, dt, n_heads, HEAD_DIM)
    s_v = kv['v'][t_p:].reshape(db, dt, n_heads, HEAD_DIM)
    return (y_prompt, y_sample, jnp.stack(new_shift_p), jnp.stack(new_wkv_p), p_k, p_v,
            kv['lf_p'].reshape(nb, seq, n_heads), jnp.stack(new_shift_s), jnp.stack(new_wkv_s),
            s_k, s_v, kv['lf_s'])
```

```python
import functools
import math

import jax
import jax.numpy as jnp
import numpy as np
from jax import lax
from jax.experimental import pallas as pl
from jax.experimental.pallas import tpu as pltpu

F32 = jnp.float32
BF16 = jnp.bfloat16

HEAD_DIM = 64
PAIR = 2 * HEAD_DIM
N_EXPERTS = 16
N_GROUPS = 4
EXPERTS_PER_GROUP = 4
N_CLASSES = 24
E_BLOCK = 128
LN_EPS = 1e-5
GN_EPS = 64e-5
NEG_INF = -1e30
LOG2E = 1.4426950408889634
MIB = 2 ** 20

_PAIR_LO = (0, 0, 0, 1, 1, 2)
_PAIR_HI = (1, 2, 3, 2, 3, 3)


def _params(sem, vmem_mib):
    return pltpu.CompilerParams(dimension_semantics=sem, vmem_limit_bytes=vmem_mib * MIB)


def _dot(a, b):
    return jnp.dot(a, b, preferred_element_type=F32)


def _dot_nt(a, b):
    return lax.dot_general(a, b, (((1,), (1,)), ((), ())), preferred_element_type=F32)


def _dot_tn(a, b):
    return lax.dot_general(a, b, (((0,), (0,)), ((), ())), preferred_element_type=F32)


def _split(x):
    hi = x.astype(BF16)
    lo = (x - hi.astype(F32)).astype(BF16)
    return hi, lo


def _dot3(a, b, dot=_dot):
    ah, al = a
    bh, bl = b
    return dot(ah, bh) + (dot(ah, bl) + dot(al, bh))


def _dot_exact_rhs(x, m, dot=_dot):
    hi = x.astype(BF16)
    r1 = x - hi.astype(F32)
    mid = r1.astype(BF16)
    lo = (r1 - mid.astype(F32)).astype(BF16)
    return dot(hi, m) + (dot(mid, m) + dot(lo, m))


def _head_sum(x, bd_ref):
    bd = bd_ref[...]
    parts = []
    for j in range(x.shape[1] // 256):
        parts.append(_dot_exact_rhs(x[:, 256 * j:256 * (j + 1)], bd))
    return jnp.concatenate(parts, axis=1)


def _layer_norm(z, g, b):
    mu = jnp.mean(z, axis=-1, keepdims=True)
    zc = z - mu
    var = jnp.mean(zc * zc, axis=-1, keepdims=True)
    return zc * lax.rsqrt(var + LN_EPS) * g + b


def _row_tile(n):
    for t in (256, 128, 64, 32, 16, 8):
        if n % t == 0:
            return t
    raise ValueError(n)


def _const_spec(shape):
    nd = len(shape)
    return pl.BlockSpec(shape, lambda *_: (0,) * nd)


def _rwkv_pre_kernel(has_vres, *refs):
    if has_vres:
        (x_ref, xp_ref, vf_ref, mix_ref, wr_ref, wk_ref, wv_ref, w1_ref, w2_ref, w0_ref,
         a1_ref, a2_ref, a0_ref, g1_ref, g2_ref, v1_ref, v2_ref, v0_ref, kk_ref, ka_ref, bd_ref,
         r_out, lw_out, k_out, v_out, kn_out, b_out, g_out) = refs
    else:
        (x_ref, xp_ref, mix_ref, wr_ref, wk_ref, wv_ref, w1_ref, w2_ref, w0_ref,
         a1_ref, a2_ref, a0_ref, g1_ref, g2_ref, kk_ref, ka_ref, bd_ref,
         r_out, lw_out, k_out, v_out, kn_out, b_out, g_out) = refs
    x = x_ref[...]
    xx = xp_ref[...] - x

    def mixed(i):
        return (x + xx * mix_ref[i:i + 1, :]).astype(BF16)

    xr, xw, xk, xv, xa, xg = (mixed(i) for i in range(6))
    r = _dot(xr, wr_ref[...])
    k = _dot(xk, wk_ref[...])
    v = _dot(xv, wv_ref[...])
    zw = w0_ref[...] + _dot(jnp.tanh(_dot(xw, w1_ref[...])).astype(BF16), w2_ref[...])
    lw = (-math.exp(-0.5)) * jax.nn.sigmoid(zw)
    alpha = jax.nn.sigmoid(a0_ref[...] + _dot(_dot(xa, a1_ref[...]).astype(BF16), a2_ref[...]))
    g = _dot(jax.nn.sigmoid(_dot(xg, g1_ref[...])).astype(BF16), g2_ref[...])
    if has_vres:
        gate_v = jax.nn.sigmoid(v0_ref[...] + _dot(_dot(xv, v1_ref[...]).astype(BF16), v2_ref[...]))
        v = v + (vf_ref[...] - v) * gate_v
    kkr = k * kk_ref[...]
    norm = jnp.sqrt(_head_sum(kkr * kkr, bd_ref))
    kn = kkr / jnp.maximum(norm, 1e-12)
    k2 = k * (1.0 + (alpha - 1.0) * ka_ref[...])
    r_out[...] = r
    lw_out[...] = lw
    k_out[...] = k2
    v_out[...] = v
    kn_out[...] = kn
    b_out[...] = kn * alpha
    g_out[...] = g


def _rwkv_pre(x, xp, vfirst, w, tm):
    n, d = x.shape
    has_vres = vfirst is not None
    row = pl.BlockSpec((tm, d), lambda i: (i, 0))
    ins = [x, xp] + ([vfirst] if has_vres else [])
    specs = [row, row] + ([row] if has_vres else [])
    names = ['mix', 'wr', 'wk', 'wv', 'w1', 'w2', 'w0', 'a1', 'a2', 'a0', 'g1', 'g2']
    if has_vres:
        names += ['v1', 'v2', 'v0']
    names += ['k_k', 'k_a', 'bd']
    for nm in names:
        ins.append(w[nm])
        specs.append(_const_spec(w[nm].shape))
    out = jax.ShapeDtypeStruct((n, d), F32)
    return pl.pallas_call(
        functools.partial(_rwkv_pre_kernel, has_vres),
        grid=(n // tm,),
        in_specs=specs,
        out_specs=[row] * 7,
        out_shape=[out] * 7,
        compiler_params=_params(("parallel",), 56),
        name="rwkv_pre",
    )(*ins)


def _wkv_chunk_kernel(C, n_pairs, r_ref, lw_ref, k_ref, v_ref, kn_ref, b_ref,
                      rp_out, yp_out, m_out, n_out):
    C2 = 2 * C
    row = lax.broadcasted_iota(jnp.int32, (C2, C2), 0)
    col = lax.broadcasted_iota(jnp.int32, (C2, C2), 1)
    same = (row >= C) == (col >= C)
    strict = jnp.logical_and(same, col < row)
    incl = jnp.logical_and(same, col <= row)
    tri_incl = jnp.where(incl, 1.0, 0.0).astype(BF16)
    eye_c = jnp.where(row == col, 1.0, 0.0).astype(F32)
    r128 = lax.broadcasted_iota(jnp.int32, (PAIR, PAIR), 0)
    c128 = lax.broadcasted_iota(jnp.int32, (PAIR, PAIR), 1)
    eye_p = r128 == c128
    head0 = lax.broadcasted_iota(jnp.int32, (1, PAIR), 1) < HEAD_DIM

    def stack(t):
        return jnp.concatenate([jnp.where(head0, t, 0.0), jnp.where(head0, 0.0, t)], axis=0)

    n_sq = int(math.log2(C)) - 1
    sls = [slice(PAIR * p, PAIR * (p + 1)) for p in range(n_pairs)]

    def each(f, *lists):
        return [f(*a) for a in zip(*lists)]

    def load(ref):
        return [stack(ref[:, sl]) for sl in sls]

    lws = load(lw_ref)
    L = each(lambda x: _dot_exact_rhs_left(tri_incl, x), lws)
    lc = each(lambda l: l[C - 1:C, :] + l[C2 - 1:C2, :], L)
    kn, bs, ks = load(kn_ref), load(b_ref), load(k_ref)
    at = each(lambda n, l, w: _split(-n * jnp.exp(l - w)), kn, L, lws)
    rt_f = each(lambda r, l: r * jnp.exp(l), load(r_ref), L)
    rt = each(_split, rt_f)
    e_nl = each(lambda l: jnp.exp(-l), L)
    bb = each(lambda b, e: _split(b * e), bs, e_nl)
    kb = each(lambda k, e: _split(k * e), ks, e_nl)
    e_lc = each(lambda c, l: jnp.exp(c - l), lc, L)
    bh = each(lambda b, e: _split(b * e), bs, e_lc)
    kh = each(lambda k, e: _split(k * e), ks, e_lc)
    vs = each(_split, load(v_ref))

    a_ab = each(lambda a, b: jnp.where(strict, _dot3(a, b, _dot_nt), 0.0), at, bb)
    a_ak = each(lambda a, b: _split(jnp.where(strict, _dot3(a, b, _dot_nt), 0.0)), at, kb)
    a_rb = each(lambda a, b: _split(jnp.where(incl, _dot3(a, b, _dot_nt), 0.0)), rt, bb)
    a_rk = each(lambda a, b: _split(jnp.where(incl, _dot3(a, b, _dot_nt), 0.0)), rt, kb)

    pws = each(_split, a_ab)
    tm = each(lambda a: eye_c + a, a_ab)
    for _ in range(n_sq):
        pws = each(lambda s: _split(_dot3(s, s)), pws)
        tm = each(lambda t, s: t + _dot3(s, _split(t)), tm, pws)
    tms = each(_split, tm)
    at2 = each(lambda t, a: _split(_dot3(t, a)), tms, at)
    w1 = each(lambda a, v: _split(_dot3(a, v)), a_ak, vs)
    v2 = each(lambda t, w: _split(_dot3(t, w)), tms, w1)
    r2 = each(lambda r, a, x: r + _dot3(a, x), rt_f, a_rb, at2)
    y2 = each(lambda a, x, c, v: _dot3(a, x) + _dot3(c, v), a_rb, v2, a_rk, vs)
    mm = each(lambda c, b, x: jnp.where(eye_p, jnp.exp(c), 0.0) + _dot3(b, x, _dot_tn), lc, bh, at2)
    nn = each(lambda b, x, k, v: _dot3(b, x, _dot_tn) + _dot3(k, v, _dot_tn), bh, v2, kh, vs)
    rp_out[...] = jnp.concatenate([x[:C, :] + x[C:, :] for x in r2], axis=1)
    yp_out[...] = jnp.concatenate([x[:C, :] + x[C:, :] for x in y2], axis=1)
    m_out[0] = jnp.stack(mm, axis=0)
    n_out[0] = jnp.stack(nn, axis=0)


def _dot_exact_rhs_left(m, x):
    hi = x.astype(BF16)
    r1 = x - hi.astype(F32)
    mid = r1.astype(BF16)
    lo = (r1 - mid.astype(F32)).astype(BF16)
    return _dot(m, hi) + (_dot(m, mid) + _dot(m, lo))


def _wkv_chunks(arrs, C, row0, n_chunks, pairs_per_step):
    d = arrs[0].shape[1]
    n_pairs = d // PAIR
    assert row0 % C == 0 and n_pairs % pairs_per_step == 0
    blk0 = row0 // C
    lanes = pairs_per_step * PAIR
    in_spec = pl.BlockSpec((C, lanes), lambda c, q: (blk0 + c, q))
    out_row = pl.BlockSpec((C, lanes), lambda c, q: (c, q))
    out_mat = pl.BlockSpec((1, pairs_per_step, PAIR, PAIR), lambda c, q: (c, q, 0, 0))
    t = n_chunks * C
    return pl.pallas_call(
        functools.partial(_wkv_chunk_kernel, C, pairs_per_step),
        grid=(n_chunks, n_pairs // pairs_per_step),
        in_specs=[in_spec] * 6,
        out_specs=[out_row, out_row, out_mat, out_mat],
        out_shape=[jax.ShapeDtypeStruct((t, d), F32), jax.ShapeDtypeStruct((t, d), F32),
                   jax.ShapeDtypeStruct((n_chunks, n_pairs, PAIR, PAIR), F32),
                   jax.ShapeDtypeStruct((n_chunks, n_pairs, PAIR, PAIR), F32)],
        compiler_params=_params(("parallel", "parallel"), 32),
        name=f"wkv_chunks_c{C}",
    )(*arrs)


def _wkv_seq_kernel(n_pairs, n_steps, rp_ref, yp_ref, m_ref, n_ref, s0_ref, y_out, s_out, s_scr):
    j = pl.program_id(1)

    @pl.when(j == 0)
    def _():
        s_scr[...] = s0_ref[0]

    ys, new_s = [], []
    for p in range(n_pairs):
        sl = slice(PAIR * p, PAIR * (p + 1))
        ss = _split(s_scr[p])
        ys.append(_dot3(_split(rp_ref[:, sl]), ss) + yp_ref[:, sl])
        new_s.append(_dot3(_split(m_ref[0, p]), ss) + n_ref[0, p])
    y_out[...] = jnp.concatenate(ys, axis=1)
    s_scr[...] = jnp.stack(new_s, axis=0)

    @pl.when(j == n_steps - 1)
    def _():
        s_out[0] = s_scr[...]


def _wkv_seq(rp, yp, m, nn, s0, C, n_seq, n_steps):
    t, d = rp.shape
    n_pairs = d // PAIR
    row = pl.BlockSpec((C, d), lambda s, j: (s * n_steps + j, 0))
    mat = pl.BlockSpec((1, n_pairs, PAIR, PAIR), lambda s, j: (s * n_steps + j, 0, 0, 0))
    st = pl.BlockSpec((1, n_pairs, PAIR, PAIR), lambda s, j: (s, 0, 0, 0))
    return pl.pallas_call(
        functools.partial(_wkv_seq_kernel, n_pairs, n_steps),
        grid=(n_seq, n_steps),
        in_specs=[row, row, mat, mat, st],
        out_specs=[row, st],
        out_shape=[jax.ShapeDtypeStruct((t, d), F32),
                   jax.ShapeDtypeStruct((n_seq, n_pairs, PAIR, PAIR), F32)],
        scratch_shapes=[pltpu.VMEM((n_pairs, PAIR, PAIR), F32)],
        compiler_params=_params(("arbitrary", "arbitrary"), 32),
        name=f"wkv_seq_c{C}",
    )(rp, yp, m, nn, s0)


def _state_to_blockdiag(s):
    b, h, n, _ = s.shape
    st = jnp.swapaxes(s, -1, -2).reshape(b, h // 2, 2, n, n)
    z = jnp.zeros_like(st[:, :, 0])
    top = jnp.concatenate([st[:, :, 0], z], axis=-1)
    bot = jnp.concatenate([z, st[:, :, 1]], axis=-1)
    return jnp.concatenate([top, bot], axis=-2)


def _blockdiag_to_state(bd):
    b, hp, _, _ = bd.shape
    n = HEAD_DIM
    s0 = bd[:, :, :n, :n]
    s1 = bd[:, :, n:, n:]
    st = jnp.stack([s0, s1], axis=2).reshape(b, hp * 2, n, n)
    return jnp.swapaxes(st, -1, -2)


def _route(x1, rw_ref, rb_ref):
    logits = _dot3(_split(rw_ref[...]), _split(x1), _dot_nt)
    mx = jnp.max(logits, axis=0, keepdims=True)
    ex = jnp.exp(logits - mx)
    scores = ex / jnp.sum(ex, axis=0, keepdims=True)
    sel = scores + rb_ref[...]
    rows = [sel[e:e + 1, :] for e in range(N_EXPERTS)]
    srow = [scores[e:e + 1, :] for e in range(N_EXPERTS)]

    def top2(vals):
        m1 = jnp.maximum(jnp.maximum(vals[0], vals[1]), jnp.maximum(vals[2], vals[3]))
        i1 = jnp.where(vals[0] == m1, 0, jnp.where(vals[1] == m1, 1, jnp.where(vals[2] == m1, 2, 3)))
        rest = [jnp.where(i1 == j, -jnp.inf, vals[j]) for j in range(4)]
        m2 = jnp.maximum(jnp.maximum(rest[0], rest[1]), jnp.maximum(rest[2], rest[3]))
        i2 = jnp.where(rest[0] == m2, 0, jnp.where(rest[1] == m2, 1, jnp.where(rest[2] == m2, 2, 3)))
        return m1, i1, m2, i2

    gscore = []
    for gidx in range(N_GROUPS):
        m1, _, m2, _ = top2(rows[4 * gidx:4 * gidx + 4])
        gscore.append(m1 + m2)
    gm = jnp.maximum(jnp.maximum(gscore[0], gscore[1]), jnp.maximum(gscore[2], gscore[3]))
    gi = jnp.where(gscore[0] == gm, 0, jnp.where(gscore[1] == gm, 1, jnp.where(gscore[2] == gm, 2, 3)))

    def pick(rws, j):
        return jnp.where(gi == 0, rws[j], jnp.where(gi == 1, rws[4 + j],
                                                    jnp.where(gi == 2, rws[8 + j], rws[12 + j])))

    in_grp = [pick(rows, j) for j in range(4)]
    sc_grp = [pick(srow, j) for j in range(4)]
    _, i1, _, i2 = top2(in_grp)

    def at(vals, idx):
        return jnp.where(idx == 0, vals[0], jnp.where(idx == 1, vals[1],
                                                      jnp.where(idx == 2, vals[2], vals[3])))

    ga = at(sc_grp, i1)
    gb = at(sc_grp, i2)
    tot = ga + gb
    ga = ga / tot
    gb = gb / tot
    lo = jnp.minimum(i1, i2)
    hi = jnp.maximum(i1, i2)
    g_lo = jnp.where(i1 < i2, ga, gb)
    g_hi = jnp.where(i1 < i2, gb, ga)
    pair = jnp.where(lo == 0, hi - 1, jnp.where(lo == 1, hi + 1, 5))
    cls = gi * 6 + pair
    return cls.astype(jnp.int32), g_lo, g_hi


def _mix_post_kernel(is_rwkv, alpha_dn, *refs):
    if is_rwkv:
        (x_ref, y_ref, r_ref, k_ref, v_ref, g_ref, lg_ref, lb_ref, rk_ref, bd_ref,
         wo_ref, n1g_ref, n1b_ref, rw_ref, rb_ref, x1_out, cls_out, gate_out) = refs
        y = y_ref[...]
        mu = _head_sum(y, bd_ref) * (1.0 / HEAD_DIM)
        yc = y - mu
        var = _head_sum(yc * yc, bd_ref) * (1.0 / HEAD_DIM)
        yn = yc * lax.rsqrt(var + GN_EPS) * lg_ref[...] + lb_ref[...]
        v = v_ref[...]
        bonus = _head_sum(r_ref[...] * k_ref[...] * rk_ref[...], bd_ref)
        z = (yn + bonus * v) * g_ref[...]
    else:
        (x_ref, o_ref, gt_ref, wo_ref, n1g_ref, n1b_ref, rw_ref, rb_ref,
         x1_out, cls_out, gate_out) = refs
        z = o_ref[...] * gt_ref[...]
    h = _dot(z.astype(BF16), wo_ref[...])
    x1 = _layer_norm(alpha_dn * x_ref[...] + h, n1g_ref[...], n1b_ref[...])
    x1_out[...] = x1
    cls, g_lo, g_hi = _route(x1, rw_ref, rb_ref)
    tm = x1.shape[0]
    cls_out[...] = jnp.broadcast_to(cls, (8, tm))
    gate_out[...] = jnp.concatenate([g_lo, g_hi, jnp.zeros((6, tm), F32)], axis=0)


def _mix_post(is_rwkv, alpha_dn, acts, consts, tm):
    n, d = acts[0].shape
    row = pl.BlockSpec((tm, d), lambda i: (i, 0))
    specs = [row] * len(acts) + [_const_spec(c.shape) for c in consts]
    lane = pl.BlockSpec((8, tm), lambda i: (0, i))
    return pl.pallas_call(
        functools.partial(_mix_post_kernel, is_rwkv, alpha_dn),
        grid=(n // tm,),
        in_specs=specs,
        out_specs=[row, lane, lane],
        out_shape=[jax.ShapeDtypeStruct((n, d), F32), jax.ShapeDtypeStruct((8, n), jnp.int32),
                   jax.ShapeDtypeStruct((8, n), F32)],
        compiler_params=_params(("parallel",), 48),
        name="mix_post_rwkv" if is_rwkv else "mix_post_fox",
    )(*acts, *consts)


def _moe_kernel(n_tok, d_exp, e1_ref, e2_ref, valid_ref, tok_ref, x_hbm, glo_ref, ghi_ref,
                wgu1_ref, wd1_ref, wgu2_ref, wd2_ref, out_hbm, xbuf, obuf, gsem, ssem):
    b = pl.program_id(0)
    base = b * E_BLOCK

    def gather_copy(i):
        tok = jnp.minimum(tok_ref[base + i], n_tok - 1)
        return pltpu.make_async_copy(x_hbm.at[pl.ds(tok, 1)], xbuf.at[pl.ds(i, 1)], gsem)

    def scatter_copy(tok, i):
        return pltpu.make_async_copy(obuf.at[pl.ds(i, 1)], out_hbm.at[pl.ds(tok, 1)], ssem)

    @pl.when(valid_ref[b] != 0)
    def _():
        def start_g(i, c):
            gather_copy(i).start()
            return c

        def wait_g(i, c):
            gather_copy(i).wait()
            return c

        lax.fori_loop(0, E_BLOCK, start_g, 0)
        lax.fori_loop(0, E_BLOCK, wait_g, 0)
        xb = xbuf[...].astype(BF16)

        def expert(wgu_ref, wd_ref):
            gu = _dot(xb, wgu_ref[0])
            hmid = jax.nn.silu(gu[:, :d_exp]) * gu[:, d_exp:]
            return _dot(hmid.astype(BF16), wd_ref[0])

        y1 = expert(wgu1_ref, wd1_ref)
        y2 = expert(wgu2_ref, wd2_ref)
        obuf[...] = y1 * glo_ref[...] + y2 * ghi_ref[...]

        def start_s(i, c):
            tok = tok_ref[base + i]

            @pl.when(tok < n_tok)
            def _():
                scatter_copy(tok, i).start()
            return c

        def wait_s(i, c):
            tok = tok_ref[base + i]

            @pl.when(tok < n_tok)
            def _():
                scatter_copy(tok, i).wait()
            return c

        lax.fori_loop(0, E_BLOCK, start_s, 0)
        lax.fori_loop(0, E_BLOCK, wait_s, 0)


def _moe(x1, cls, g_lo, g_hi, w_gu, w_down):
    n, d = x1.shape
    d_exp = w_down.shape[1]
    n_blk = (n + N_CLASSES * (E_BLOCK - 1) + E_BLOCK - 1) // E_BLOCK
    n_slot = n_blk * E_BLOCK
    counts = jnp.bincount(cls, length=N_CLASSES)
    padded = (counts + E_BLOCK - 1) // E_BLOCK * E_BLOCK
    pad_end = jnp.cumsum(padded)
    pad_start = pad_end - padded
    seg_start = jnp.cumsum(counts) - counts
    order = jnp.argsort(cls, stable=True).astype(jnp.int32)
    sc = cls[order]
    dest = pad_start[sc] + jnp.arange(n, dtype=jnp.int32) - seg_start[sc]
    slot_tok = jnp.full((n_slot,), n, jnp.int32).at[dest].set(order)
    slot_lo = jnp.zeros((n_slot,), F32).at[dest].set(g_lo[order]).reshape(n_slot, 1)
    slot_hi = jnp.zeros((n_slot,), F32).at[dest].set(g_hi[order]).reshape(n_slot, 1)
    blk_start = jnp.arange(n_blk, dtype=jnp.int32) * E_BLOCK
    blk_cls = jnp.minimum(jnp.searchsorted(pad_end, blk_start, side='right'), N_CLASSES - 1).astype(jnp.int32)
    blk_valid = (blk_start < pad_end[-1]).astype(jnp.int32)
    last_cls = blk_cls[jnp.maximum(jnp.sum(blk_valid) - 1, 0)]
    blk_cls = jnp.where(blk_valid != 0, blk_cls, last_cls)
    pair_lo = jnp.asarray(_PAIR_LO, jnp.int32)
    pair_hi = jnp.asarray(_PAIR_HI, jnp.int32)
    blk_e1 = (blk_cls // 6) * EXPERTS_PER_GROUP + pair_lo[blk_cls % 6]
    blk_e2 = (blk_cls // 6) * EXPERTS_PER_GROUP + pair_hi[blk_cls % 6]

    gate_spec = pl.BlockSpec((E_BLOCK, 1), lambda b, e1, e2, vl, tk: (b, 0))
    gu1 = pl.BlockSpec((1, d, 2 * d_exp), lambda b, e1, e2, vl, tk: (e1[b], 0, 0))
    dn1 = pl.BlockSpec((1, d_exp, d), lambda b, e1, e2, vl, tk: (e1[b], 0, 0))
    gu2 = pl.BlockSpec((1, d, 2 * d_exp), lambda b, e1, e2, vl, tk: (e2[b], 0, 0))
    dn2 = pl.BlockSpec((1, d_exp, d), lambda b, e1, e2, vl, tk: (e2[b], 0, 0))
    grid_spec = pltpu.PrefetchScalarGridSpec(
        num_scalar_prefetch=4,
        grid=(n_blk,),
        in_specs=[pl.BlockSpec(memory_space=pl.ANY), gate_spec, gate_spec, gu1, dn1, gu2, dn2],
        out_specs=pl.BlockSpec(memory_space=pl.ANY),
        scratch_shapes=[pltpu.VMEM((E_BLOCK, d), F32), pltpu.VMEM((E_BLOCK, d), F32),
                        pltpu.SemaphoreType.DMA(()), pltpu.SemaphoreType.DMA(())],
    )
    return pl.pallas_call(
        functools.partial(_moe_kernel, n, d_exp),
        grid_spec=grid_spec,
        out_shape=jax.ShapeDtypeStruct((n, d), F32),
        compiler_params=_params(("arbitrary",), 40),
        name="moe_experts",
    )(blk_e1, blk_e2, blk_valid, slot_tok, x1, slot_lo, slot_hi, w_gu, w_down, w_gu, w_down)


def _ln2_kernel(alpha_dn, with_kv, *refs):
    if with_kv:
        (x1_ref, m_ref, g_ref, b_ref, wk_ref, wv_ref, wf_ref, bf_ref, tri_ref,
         x2_out, k_out, v_out, kb_out, vb_out, lf_out, cum_out, carry) = refs
    else:
        x1_ref, m_ref, g_ref, b_ref, x2_out = refs
    x2 = _layer_norm(alpha_dn * x1_ref[...] + m_ref[...], g_ref[...], b_ref[...])
    x2_out[...] = x2
    if with_kv:
        xb = x2.astype(BF16)
        k = _dot(xb, wk_ref[...])
        v = _dot(xb, wv_ref[...])
        k_out[...] = k
        v_out[...] = v
        kb_out[...] = k.astype(BF16)
        vb_out[...] = v.astype(BF16)
        z = _dot3(_split(wf_ref[...]), _split(x2), _dot_nt) + bf_ref[...]
        lf = jnp.minimum(z, 0.0) - jnp.log1p(jnp.exp(-jnp.abs(z)))
        lf_out[...] = lf

        @pl.when(pl.program_id(0) == 0)
        def _():
            carry[...] = jnp.zeros_like(carry)

        cum = _dot_exact_rhs(lf, tri_ref[...]) + carry[...]
        cum_out[...] = cum
        carry[...] = cum[:, -1:]


def _ln2(alpha_dn, x1, m, g, b, kv, tm):
    n, d = x1.shape
    row = pl.BlockSpec((tm, d), lambda i: (i, 0))
    with_kv = kv is not None
    ins = [x1, m, g, b]
    specs = [row, row, _const_spec(g.shape), _const_spec(b.shape)]
    outs = [jax.ShapeDtypeStruct((n, d), F32)]
    ospecs = [row]
    scratch = []
    if with_kv:
        nh = kv['wf_t'].shape[0]
        tri = (jnp.arange(tm)[:, None] <= jnp.arange(tm)[None, :]).astype(BF16)
        extra = [kv['wk'], kv['wv'], kv['wf_t'], kv['bf'], tri]
        ins += extra
        specs += [_const_spec(e.shape) for e in extra]
        lane = pl.BlockSpec((nh, tm), lambda i: (0, i))
        outs += [jax.ShapeDtypeStruct((n, d), F32), jax.ShapeDtypeStruct((n, d), F32),
                 jax.ShapeDtypeStruct((n, d), BF16), jax.ShapeDtypeStruct((n, d), BF16),
                 jax.ShapeDtypeStruct((nh, n), F32), jax.ShapeDtypeStruct((nh, n), F32)]
        ospecs += [row, row, row, row, lane, lane]
        scratch = [pltpu.VMEM((nh, 1), F32)]
    res = pl.pallas_call(
        functools.partial(_ln2_kernel, alpha_dn, with_kv),
        grid=(n // tm,),
        in_specs=specs,
        out_specs=ospecs,
        out_shape=outs,
        scratch_shapes=scratch,
        compiler_params=_params(("arbitrary",), 48),
        name="ln2_kv" if with_kv else "ln2",
    )(*ins)
    return res


def _fox_qg_kernel(x_ref, wq_ref, wg_ref, q_out, gate_out):
    xb = x_ref[...].astype(BF16)
    q_out[...] = (_dot(xb, wq_ref[...]) * (LOG2E * HEAD_DIM ** -0.5)).astype(BF16)
    gate_out[...] = jax.nn.sigmoid(_dot(xb, wg_ref[...]))


def _fox_qg(x, wq, wg, tm):
    n, d = x.shape
    row = pl.BlockSpec((tm, d), lambda i: (i, 0))
    return pl.pallas_call(
        _fox_qg_kernel,
        grid=(n // tm,),
        in_specs=[row, _const_spec(wq.shape), _const_spec(wg.shape)],
        out_specs=[row, row],
        out_shape=[jax.ShapeDtypeStruct((n, d), BF16), jax.ShapeDtypeStruct((n, d), F32)],
        compiler_params=_params(("parallel",), 40),
        name="fox_qg",
    )(x, wq, wg)


V_ROWS = 80


def _fox_prompt_kernel(tq, hps, q_ref, k_ref, vt_ref, o_out, s_scr, p_scr):
    i = pl.program_id(1)
    rr = lax.broadcasted_iota(jnp.int32, (tq, tq), 0)
    cc = lax.broadcasted_iota(jnp.int32, (tq, tq), 1)
    causal = rr <= cc

    def scores(j, h):
        k0 = pl.multiple_of(j * tq, tq)
        lanes = slice(PAIR * h, PAIR * (h + 1))
        return _dot_nt(k_ref[pl.ds(k0, tq), lanes], q_ref[:, lanes])

    def softmax(s, m):
        m_new = jnp.maximum(m, jnp.max(s, axis=0, keepdims=True))
        return jnp.exp2(s - m_new).astype(BF16), jnp.exp2(m - m_new), m_new

    def accum(acc, corr, p, j, h):
        k0 = pl.multiple_of(j * tq, tq)
        return acc * corr + _dot(vt_ref[h, :, pl.ds(k0, tq)], p)

    for h in range(hps):
        s_scr[h] = scores(0, h)
        p_scr[h] = jnp.zeros((tq, tq), BF16)

    def body(j, carry):
        new = []
        for h in range(hps):
            m, acc, corr_prev = carry[h]
            acc = accum(acc, corr_prev, p_scr[h], jnp.maximum(j - 1, 0), h)
            p, corr, m = softmax(s_scr[h], m)
            p_scr[h] = p
            s_scr[h] = scores(j + 1, h)
            new.append((m, acc, corr))
        return tuple(new)

    init = tuple((jnp.full((1, tq), NEG_INF, F32), jnp.zeros((V_ROWS, tq), F32), jnp.ones((1, tq), F32))
                 for _ in range(hps))
    carry = lax.fori_loop(0, i, body, init)
    outs = []
    for h in range(hps):
        m, acc, corr_prev = carry[h]
        acc = accum(acc, corr_prev, p_scr[h], jnp.maximum(i - 1, 0), h)
        p, corr, m = softmax(jnp.where(causal, s_scr[h], NEG_INF), m)
        acc = accum(acc, corr, p, i, h)
        outs.append(acc[:HEAD_DIM, :] / acc[HEAD_DIM:HEAD_DIM + 1, :])
    o_out[...] = jnp.concatenate(outs, axis=0)


def _fox_prompt(q_aug, k_aug, vt_aug, tq, hps):
    t = q_aug.shape[0]
    nh = vt_aug.shape[0]
    return pl.pallas_call(
        functools.partial(_fox_prompt_kernel, tq, hps),
        grid=(nh // hps, t // tq),
        in_specs=[pl.BlockSpec((tq, hps * PAIR), lambda g, i: (i, g)),
                  pl.BlockSpec((t, hps * PAIR), lambda g, i: (0, g), pipeline_mode=pl.Buffered(1)),
                  pl.BlockSpec((hps, V_ROWS, t), lambda g, i: (g, 0, 0), pipeline_mode=pl.Buffered(1))],
        out_specs=pl.BlockSpec((hps * HEAD_DIM, tq), lambda g, i: (g, i)),
        out_shape=jax.ShapeDtypeStruct((nh * HEAD_DIM, t), F32),
        scratch_shapes=[pltpu.VMEM((hps, tq, tq), F32), pltpu.VMEM((hps, tq, tq), BF16)],
        compiler_params=_params(("parallel", "parallel"), 48),
        name="fox_prompt_attn",
    )(q_aug, k_aug, vt_aug)


def _augment_q(qb, cum2):
    t, d = qb.shape
    nh = d // HEAD_DIM
    hi, mid, lo = _split3(cum2)
    one = jnp.ones((t, nh), BF16)
    extra = jnp.stack([one, one, one, hi, mid, lo], axis=-1)
    pad = jnp.zeros((t, nh, HEAD_DIM - 6), BF16)
    return jnp.concatenate([qb.reshape(t, nh, HEAD_DIM), extra, pad], axis=-1).reshape(t, nh * PAIR)


def _augment_k(kb, cum2):
    t, d = kb.shape
    nh = d // HEAD_DIM
    hi, mid, lo = _split3(-cum2)
    one = jnp.ones((t, nh), BF16)
    extra = jnp.stack([hi, mid, lo, one, one, one], axis=-1)
    pad = jnp.zeros((t, nh, HEAD_DIM - 6), BF16)
    return jnp.concatenate([kb.reshape(t, nh, HEAD_DIM), extra, pad], axis=-1).reshape(t, nh * PAIR)


def _augment_vt(vb):
    t, d = vb.shape
    nh = d // HEAD_DIM
    vt = vb.reshape(t, nh, HEAD_DIM).transpose(1, 2, 0)
    return jnp.concatenate([vt, jnp.ones((nh, 1, t), BF16),
                            jnp.zeros((nh, V_ROWS - HEAD_DIM - 1, t), BF16)], axis=1)


def _fox_sample_kernel(n_pairs, q_ref, kn_ref, vn_ref, lfn_ref, kc_ref, vc_ref, lfc_ref,
                       triu_ref, o_out):
    t = q_ref.shape[0]
    past = kc_ref.shape[1]
    lfc = lfc_ref[0]
    lfn = lfn_ref[0]
    nh = lfc.shape[1]
    lfc_s = _split3(lfc)
    cum_c_r = _dot_tn3(lfc_s, triu_ref[...])
    ones_row = jnp.ones((1, past), BF16)
    tot_r = _dot_m3(ones_row, lfc_s)
    tot_c = cum_c_r[:, past - 1:past]
    rr = lax.broadcasted_iota(jnp.int32, (t, t), 0)
    cc = lax.broadcasted_iota(jnp.int32, (t, t), 1)
    causal = rr >= cc
    tril = jnp.where(causal, 1.0, 0.0).astype(BF16)
    lfn_s = _split3(lfn)
    cum_n_c = (_dot_m3(tril, lfn_s) + tot_r) * LOG2E
    cum_n_r = (_dot_tn3(lfn_s, jnp.where(rr <= cc, 1.0, 0.0).astype(BF16)) + tot_c) * LOG2E
    cum_c_r = cum_c_r * LOG2E
    lane = lax.broadcasted_iota(jnp.int32, (1, PAIR), 1)
    head0 = lane < HEAD_DIM
    hl = lax.broadcasted_iota(jnp.int32, (1, nh), 1)
    for p in range(n_pairs):
        sl = slice(PAIR * p, PAIR * (p + 1))
        q = q_ref[:, sl]
        zero = jnp.zeros_like(q)
        kc = kc_ref[0, :, sl].astype(BF16)
        vc = vc_ref[0, :, sl].astype(BF16)
        kn = kn_ref[:, sl].astype(BF16)
        vn = vn_ref[:, sl].astype(BF16)
        outs = []
        for h in range(2):
            hh = 2 * p + h
            qh = jnp.where(head0, q, zero) if h == 0 else jnp.where(head0, zero, q)
            cq = jnp.sum(jnp.where(hl == hh, cum_n_c, 0.0), axis=1, keepdims=True)
            s_c = _dot_nt(qh, kc) + cq - cum_c_r[hh:hh + 1, :]
            s_n = _dot_nt(qh, kn) + cq - cum_n_r[hh:hh + 1, :]
            s_n = jnp.where(causal, s_n, NEG_INF)
            m = jnp.maximum(jnp.max(s_c, axis=1, keepdims=True), jnp.max(s_n, axis=1, keepdims=True))
            p_c = jnp.exp2(s_c - m)
            p_n = jnp.exp2(s_n - m)
            l = jnp.sum(p_c, axis=1, keepdims=True) + jnp.sum(p_n, axis=1, keepdims=True)
            acc = _dot(p_c.astype(BF16), vc) + _dot(p_n.astype(BF16), vn)
            outs.append(acc / l)
        o_out[:, sl] = jnp.where(head0, outs[0], outs[1])


def _split3(x):
    hi = x.astype(BF16)
    r1 = x - hi.astype(F32)
    mid = r1.astype(BF16)
    lo = (r1 - mid.astype(F32)).astype(BF16)
    return hi, mid, lo


def _dot_tn3(xs, m):
    return _dot_tn(xs[0], m) + (_dot_tn(xs[1], m) + _dot_tn(xs[2], m))


def _dot_m3(m, xs):
    return _dot(m, xs[0]) + (_dot(m, xs[1]) + _dot(m, xs[2]))


def _fox_sample(qb, k, v, lf_new, cache_k, cache_v, cache_lf, row0, n_stream, t):
    d = qb.shape[1]
    n_pairs = d // PAIR
    past = cache_k.shape[1]
    nh = cache_lf.shape[2]
    blk0 = row0 // t
    triu = (jnp.arange(past)[:, None] <= jnp.arange(past)[None, :]).astype(BF16)
    row = pl.BlockSpec((t, d), lambda b: (blk0 + b, 0))
    return pl.pallas_call(
        functools.partial(_fox_sample_kernel, n_pairs),
        grid=(n_stream,),
        in_specs=[row, row, row,
                  pl.BlockSpec((1, t, nh), lambda b: (b, 0, 0)),
                  pl.BlockSpec((1, past, d), lambda b: (b, 0, 0)),
                  pl.BlockSpec((1, past, d), lambda b: (b, 0, 0)),
                  pl.BlockSpec((1, past, nh), lambda b: (b, 0, 0)),
                  _const_spec(triu.shape)],
        out_specs=pl.BlockSpec((t, d), lambda b: (b, 0)),
        out_shape=jax.ShapeDtypeStruct((n_stream * t, d), F32),
        compiler_params=_params(("parallel",), 48),
        name="fox_sample_attn",
    )(qb, k, v, lf_new, cache_k, cache_v, cache_lf, triu)


def kernel(x_prompt, x_sample, state_shift, state_wkv, cache_k, cache_v, cache_logf, ln1_g, ln1_b, ln2_g, ln2_b, a_mix, a_w_rkv, a_w0, a_w1, a_w2, a_a0, a_a1, a_a2, a_v0, a_v1, a_v2, a_g1, a_g2, a_k_k, a_k_a, a_r_k, a_lnx_g, a_lnx_b, a_w_o, kv_w, kv_bf, b_w_qg, b_w_o, router_w, router_b, moe_w_gu, moe_w_down):
    nb, seq, d = x_prompt.shape
    db, dt, _ = x_sample.shape
    assert nb == 1
    n_heads = d // HEAD_DIM
    depth = ln1_g.shape[0]
    n_a = a_mix.shape[0]
    past = cache_k.shape[1]
    t_p = nb * seq
    t_s = db * dt
    n = t_p + t_s
    tm = _row_tile(n)
    c_p = 64
    c_s = dt
    assert seq % c_p == 0 and t_p % c_s == 0 and (c_s & (c_s - 1)) == 0
    alpha_dn = (2 * depth) ** 0.25
    row2 = lambda a: a.reshape(1, -1)

    bd = (jnp.arange(256)[:, None] // HEAD_DIM == jnp.arange(256)[None, :] // HEAD_DIM).astype(BF16)
    rw_t = router_w.T
    rb_c = router_b.reshape(-1, 1)

    x = jnp.concatenate([x_prompt.reshape(t_p, d), x_sample.reshape(t_s, d)], axis=0)
    new_shift_p, new_shift_s, new_wkv_p, new_wkv_s = [], [], [], []
    v_first = None
    kv = None
    for l in range(depth):
        if l < n_a:
            xs = x[t_p:].reshape(db, dt, d)
            new_shift_p.append(x[t_p - 1:t_p].reshape(nb, d))
            new_shift_s.append(xs[:, -1])
            xp = jnp.concatenate([
                jnp.zeros((1, d), F32), x[:t_p - 1],
                jnp.concatenate([state_shift[l][:, None, :], xs[:, :-1]], axis=1).reshape(t_s, d)], axis=0)
            w = dict(mix=jnp.concatenate([a_mix[l], jnp.zeros((2, d), F32)], axis=0),
                     wr=a_w_rkv[l, 0].astype(BF16), wk=a_w_rkv[l, 1].astype(BF16), wv=a_w_rkv[l, 2].astype(BF16),
                     w1=a_w1[l].astype(BF16), w2=a_w2[l].astype(BF16), w0=row2(a_w0[l]),
                     a1=a_a1[l].astype(BF16), a2=a_a2[l].astype(BF16), a0=row2(a_a0[l]),
                     g1=a_g1[l].astype(BF16), g2=a_g2[l].astype(BF16),
                     k_k=row2(a_k_k[l]), k_a=row2(a_k_a[l]), bd=bd)
            if l > 0:
                w.update(v1=a_v1[l - 1].astype(BF16), v2=a_v2[l - 1].astype(BF16), v0=row2(a_v0[l - 1]))
            r, lw, k, v, kn, b, g = _rwkv_pre(x, xp, v_first if l > 0 else None, w, tm)
            if l == 0:
                v_first = v
            scan_in = (r, lw, k, v, kn, b)
            rp, yp, mm, nn = _wkv_chunks(scan_in, c_p, 0, t_p // c_p, n_heads // 2)
            s0 = jnp.zeros((1, n_heads // 2, PAIR, PAIR), F32)
            y_p, sf_p = _wkv_seq(rp, yp, mm, nn, s0, c_p, 1, t_p // c_p)
            rp, yp, mm, nn = _wkv_chunks(scan_in, c_s, t_p, db, n_heads // 2)
            y_s, sf_s = _wkv_seq(rp, yp, mm, nn, _state_to_blockdiag(state_wkv[l]), c_s, db, 1)
            new_wkv_p.append(_blockdiag_to_state(sf_p))
            new_wkv_s.append(_blockdiag_to_state(sf_s))
            y = jnp.concatenate([y_p, y_s], axis=0)
            consts = [row2(a_lnx_g[l]), row2(a_lnx_b[l]), row2(a_r_k[l]), bd, a_w_o[l].astype(BF16),
                      row2(ln1_g[l]), row2(ln1_b[l]), rw_t, rb_c]
            x1, cls, gates = _mix_post(True, alpha_dn, [x, y, r, k, v, g], consts, tm)
        else:
            lb = l - n_a
            qb, gate = _fox_qg(x, b_w_qg[lb][:, :d].astype(BF16), b_w_qg[lb][:, d:].astype(BF16), tm)
            o_p = _fox_prompt(_augment_q(qb[:t_p], kv['cum2']), kv['k_aug'], kv['vt_aug'], 256, 4).T
            o_s = _fox_sample(qb, kv['k'], kv['v'], kv['lf_s'], cache_k.reshape(db, past, d),
                              cache_v.reshape(db, past, d), cache_logf, t_p, db, dt)
            o = jnp.concatenate([o_p, o_s], axis=0)
            consts = [b_w_o[lb].astype(BF16), row2(ln1_g[l]), row2(ln1_b[l]), rw_t, rb_c]
            x1, cls, gates = _mix_post(False, alpha_dn, [x, o, gate], consts, tm)
        m = _moe(x1, cls[0], gates[0], gates[1], moe_w_gu[l].astype(BF16), moe_w_down[l].astype(BF16))
        if l == n_a - 1:
            kvw = dict(wk=kv_w[:, :d].astype(BF16), wv=kv_w[:, d:2 * d].astype(BF16),
                       wf_t=kv_w[:, 2 * d:].T, bf=kv_bf.reshape(-1, 1))
            x, k_all, v_all, kb_all, vb_all, lf_r, cum_r = _ln2(
                alpha_dn, x1, m, row2(ln2_g[l]), row2(ln2_b[l]), kvw, tm)
            cum2 = cum_r[:, :t_p].T * LOG2E
            kv = dict(k=k_all, v=v_all, cum2=cum2, k_aug=_augment_k(kb_all[:t_p], cum2),
                      vt_aug=_augment_vt(vb_all[:t_p]), lf_p=lf_r[:, :t_p].T,
                      lf_s=lf_r[:, t_p:].T.reshape(db, dt, n_heads))
        else:
            x = _ln2(alpha_dn, x1, m, row2(ln2_g[l]), row2(ln2_b[l]), None, tm)[0]

    y_prompt = x[:t_p].reshape(nb, seq, d)
    y_sample = x[t_p:].reshape(db, dt, d)
    p_k = kv['k'][:t_p].reshape(nb, seq, n_heads, HEAD_DIM)
    p_v = kv['v'][:t_p].reshape(nb, seq, n_heads, HEAD_DIM)
    s_k = kv['k'][t_p:].reshape(db, dt, n_heads, HEAD_DIM)
    s_v = kv['v'][t_p:].reshape(db, dt, n_heads, HEAD_DIM)
    return (y_prompt, y_sample, jnp.stack(new_shift_p), jnp.stack(new_wkv_p), p_k, p_v,
            kv['lf_p'].reshape(nb, seq, n_heads), jnp.stack(new_shift_s), jnp.stack(new_wkv_s),
            s_k, s_v, kv['lf_s'])
```

```python
import functools
import math

import jax
import jax.numpy as jnp
import numpy as np
from jax import lax
from jax.experimental import pallas as pl
from jax.experimental.pallas import tpu as pltpu

F32 = jnp.float32
BF16 = jnp.bfloat16

HEAD_DIM = 64
PAIR = 2 * HEAD_DIM
N_EXPERTS = 16
N_GROUPS = 4
EXPERTS_PER_GROUP = 4
N_CLASSES = 24
E_BLOCK = 128
LN_EPS = 1e-5
GN_EPS = 64e-5
NEG_INF = -1e30
LOG2E = 1.4426950408889634
MIB = 2 ** 20

_PAIR_LO = (0, 0, 0, 1, 1, 2)
_PAIR_HI = (1, 2, 3, 2, 3, 3)


def _params(sem, vmem_mib):
    return pltpu.CompilerParams(dimension_semantics=sem, vmem_limit_bytes=vmem_mib * MIB)


def _dot(a, b):
    return jnp.dot(a, b, preferred_element_type=F32)


def _dot_nt(a, b):
    return lax.dot_general(a, b, (((1,), (1,)), ((), ())), preferred_element_type=F32)


def _dot_tn(a, b):
    return lax.dot_general(a, b, (((0,), (0,)), ((), ())), preferred_element_type=F32)


def _split(x):
    hi = x.astype(BF16)
    lo = (x - hi.astype(F32)).astype(BF16)
    return hi, lo


def _dot3(a, b, dot=_dot):
    return _dot3_multi(a, [b], dot)[0]


def _dot3_multi(a, bs, dot=_dot):
    ah, al = a
    ax = 0 if dot is _dot_nt else 1
    ns = [b[0].shape[ax] for b in bs]
    if any(n % 128 for n in ns):
        return [dot(ah, bh) + (dot(ah, bl) + dot(al, bh)) for bh, bl in bs]
    r1 = dot(ah, jnp.concatenate([x for b in bs for x in b], axis=ax))
    r2 = dot(al, jnp.concatenate([b[0] for b in bs], axis=ax)) if len(bs) > 1 else dot(al, bs[0][0])
    outs, o1, o2 = [], 0, 0
    for n in ns:
        outs.append((r1[:, o1:o1 + n] + r1[:, o1 + n:o1 + 2 * n]) + r2[:, o2:o2 + n])
        o1 += 2 * n
        o2 += n
    return outs


def _dot_exact_rhs(x, m, dot=_dot):
    hi = x.astype(BF16)
    r1 = x - hi.astype(F32)
    mid = r1.astype(BF16)
    lo = (r1 - mid.astype(F32)).astype(BF16)
    return dot(hi, m) + (dot(mid, m) + dot(lo, m))


def _head_sum(x, bd_ref):
    bd = bd_ref[...]
    parts = []
    for j in range(x.shape[1] // 256):
        parts.append(_dot_exact_rhs(x[:, 256 * j:256 * (j + 1)], bd))
    return jnp.concatenate(parts, axis=1)


def _layer_norm(z, g, b):
    mu = jnp.mean(z, axis=-1, keepdims=True)
    zc = z - mu
    var = jnp.mean(zc * zc, axis=-1, keepdims=True)
    return zc * lax.rsqrt(var + LN_EPS) * g + b


def _row_tile(n):
    for t in (256, 128, 64, 32, 16, 8):
        if n % t == 0:
            return t
    raise ValueError(n)


def _const_spec(shape):
    nd = len(shape)
    return pl.BlockSpec(shape, lambda *_: (0,) * nd)


def _rwkv_pre_kernel(has_vres, *refs):
    if has_vres:
        (x_ref, xp_ref, vf_ref, mix_ref, wr_ref, wk_ref, wv_ref, w1_ref, w2_ref, w0_ref,
         a1_ref, a2_ref, a0_ref, g1_ref, g2_ref, v1_ref, v2_ref, v0_ref, kk_ref, ka_ref, bd_ref,
         r_out, lw_out, k_out, v_out, kn_out, b_out, g_out) = refs
    else:
        (x_ref, xp_ref, mix_ref, wr_ref, wk_ref, wv_ref, w1_ref, w2_ref, w0_ref,
         a1_ref, a2_ref, a0_ref, g1_ref, g2_ref, kk_ref, ka_ref, bd_ref,
         r_out, lw_out, k_out, v_out, kn_out, b_out, g_out) = refs
    x = x_ref[...]
    xx = xp_ref[...] - x

    def mixed(i):
        return (x + xx * mix_ref[i:i + 1, :]).astype(BF16)

    xr, xw, xk, xv, xa, xg = (mixed(i) for i in range(6))
    r = _dot(xr, wr_ref[...])
    k = _dot(xk, wk_ref[...])
    v = _dot(xv, wv_ref[...])
    zw = w0_ref[...] + _dot(jnp.tanh(_dot(xw, w1_ref[...])).astype(BF16), w2_ref[...])
    lw = (-math.exp(-0.5)) * jax.nn.sigmoid(zw)
    alpha = jax.nn.sigmoid(a0_ref[...] + _dot(_dot(xa, a1_ref[...]).astype(BF16), a2_ref[...]))
    g = _dot(jax.nn.sigmoid(_dot(xg, g1_ref[...])).astype(BF16), g2_ref[...])
    if has_vres:
        gate_v = jax.nn.sigmoid(v0_ref[...] + _dot(_dot(xv, v1_ref[...]).astype(BF16), v2_ref[...]))
        v = v + (vf_ref[...] - v) * gate_v
    kkr = k * kk_ref[...]
    norm = jnp.sqrt(_head_sum(kkr * kkr, bd_ref))
    kn = kkr / jnp.maximum(norm, 1e-12)
    k2 = k * (1.0 + (alpha - 1.0) * ka_ref[...])
    r_out[...] = r
    lw_out[...] = lw
    k_out[...] = k2
    v_out[...] = v
    kn_out[...] = kn
    b_out[...] = kn * alpha
    g_out[...] = g


def _rwkv_pre(x, xp, vfirst, w, tm):
    n, d = x.shape
    has_vres = vfirst is not None
    row = pl.BlockSpec((tm, d), lambda i: (i, 0))
    ins = [x, xp] + ([vfirst] if has_vres else [])
    specs = [row, row] + ([row] if has_vres else [])
    names = ['mix', 'wr', 'wk', 'wv', 'w1', 'w2', 'w0', 'a1', 'a2', 'a0', 'g1', 'g2']
    if has_vres:
        names += ['v1', 'v2', 'v0']
    names += ['k_k', 'k_a', 'bd']
    for nm in names:
        ins.append(w[nm])
        specs.append(_const_spec(w[nm].shape))
    out = jax.ShapeDtypeStruct((n, d), F32)
    return pl.pallas_call(
        functools.partial(_rwkv_pre_kernel, has_vres),
        grid=(n // tm,),
        in_specs=specs,
        out_specs=[row] * 7,
        out_shape=[out] * 7,
        compiler_params=_params(("parallel",), 56),
        name="rwkv_pre",
    )(*ins)


def _wkv_chunk_kernel(C, n_pairs, r_ref, lw_ref, k_ref, v_ref, kn_ref, b_ref,
                      rp_out, yp_out, m_out, n_out):
    C2 = 2 * C
    row = lax.broadcasted_iota(jnp.int32, (C2, C2), 0)
    col = lax.broadcasted_iota(jnp.int32, (C2, C2), 1)
    same = (row >= C) == (col >= C)
    strict = jnp.logical_and(same, col < row)
    incl = jnp.logical_and(same, col <= row)
    tri_incl = jnp.where(incl, 1.0, 0.0).astype(BF16)
    eye_c = jnp.where(row == col, 1.0, 0.0).astype(F32)
    r128 = lax.broadcasted_iota(jnp.int32, (PAIR, PAIR), 0)
    c128 = lax.broadcasted_iota(jnp.int32, (PAIR, PAIR), 1)
    eye_p = r128 == c128
    head0 = lax.broadcasted_iota(jnp.int32, (1, PAIR), 1) < HEAD_DIM

    def stack(t):
        return jnp.concatenate([jnp.where(head0, t, 0.0), jnp.where(head0, 0.0, t)], axis=0)

    n_sq = int(math.log2(C)) - 1
    sls = [slice(PAIR * p, PAIR * (p + 1)) for p in range(n_pairs)]

    def each(f, *lists):
        return [f(*a) for a in zip(*lists)]

    def load(ref):
        return [stack(ref[:, sl]) for sl in sls]

    lws = load(lw_ref)
    L = each(lambda x: _dot_exact_rhs_left(tri_incl, x), lws)
    lc = each(lambda l: l[C - 1:C, :] + l[C2 - 1:C2, :], L)
    kn, bs, ks = load(kn_ref), load(b_ref), load(k_ref)
    at = each(lambda n, l, w: _split(-n * jnp.exp(l - w)), kn, L, lws)
    rt_f = each(lambda r, l: r * jnp.exp(l), load(r_ref), L)
    rt = each(_split, rt_f)
    e_nl = each(lambda l: jnp.exp(-l), L)
    bb = each(lambda b, e: _split(b * e), bs, e_nl)
    kb = each(lambda k, e: _split(k * e), ks, e_nl)
    e_lc = each(lambda c, l: jnp.exp(c - l), lc, L)
    bh = each(lambda b, e: _split(b * e), bs, e_lc)
    kh = each(lambda k, e: _split(k * e), ks, e_lc)
    vs = each(_split, load(v_ref))

    g_a = each(lambda a, b, k: _dot3_multi(a, [b, k], _dot_nt), at, bb, kb)
    g_r = each(lambda a, b, k: _dot3_multi(a, [b, k], _dot_nt), rt, bb, kb)
    a_ab = [jnp.where(strict, g[0], 0.0) for g in g_a]
    a_ak = [_split(jnp.where(strict, g[1], 0.0)) for g in g_a]
    a_rb = [_split(jnp.where(incl, g[0], 0.0)) for g in g_r]
    a_rk = [_split(jnp.where(incl, g[1], 0.0)) for g in g_r]

    pws = each(lambda a: _split(_dot3(a, a)), each(_split, a_ab))
    tm = each(lambda a: eye_c + a, a_ab)
    for step in range(n_sq):
        if step < n_sq - 1:
            res = each(lambda s, t: _dot3_multi(s, [_split(t), s]), pws, tm)
            tm = [t + r[0] for t, r in zip(tm, res)]
            pws = [_split(r[1]) for r in res]
        else:
            tm = each(lambda t, s: t + _dot3(s, _split(t)), tm, pws)
    tms = each(_split, tm)
    w1 = each(lambda a, v: _split(_dot3(a, v)), a_ak, vs)
    tx = each(lambda t, a, w: _dot3_multi(t, [a, w]), tms, at, w1)
    at2 = [_split(r[0]) for r in tx]
    v2 = [_split(r[1]) for r in tx]
    ax = each(lambda a, x, v: _dot3_multi(a, [x, v]), a_rb, at2, v2)
    r2 = [r + x[0] for r, x in zip(rt_f, ax)]
    y2 = [x[1] + _dot3(c, v) for x, c, v in zip(ax, a_rk, vs)]
    bx = each(lambda b, x, v: _dot3_multi(b, [x, v], _dot_tn), bh, at2, v2)
    mm = [jnp.where(eye_p, jnp.exp(c), 0.0) + x[0] for c, x in zip(lc, bx)]
    nn = [x[1] + _dot3(k, v, _dot_tn) for x, k, v in zip(bx, kh, vs)]
    rp_out[...] = jnp.concatenate([x[:C, :] + x[C:, :] for x in r2], axis=1)
    yp_out[...] = jnp.concatenate([x[:C, :] + x[C:, :] for x in y2], axis=1)
    m_out[0] = jnp.stack(mm, axis=0)
    n_out[0] = jnp.stack(nn, axis=0)


def _dot_exact_rhs_left(m, x):
    hi = x.astype(BF16)
    r1 = x - hi.astype(F32)
    mid = r1.astype(BF16)
    lo = (r1 - mid.astype(F32)).astype(BF16)
    return _dot(m, hi) + (_dot(m, mid) + _dot(m, lo))


def _wkv_chunks(arrs, C, row0, n_chunks, pairs_per_step):
    d = arrs[0].shape[1]
    n_pairs = d // PAIR
    assert row0 % C == 0 and n_pairs % pairs_per_step == 0
    blk0 = row0 // C
    lanes = pairs_per_step * PAIR
    in_spec = pl.BlockSpec((C, lanes), lambda c, q: (blk0 + c, q))
    out_row = pl.BlockSpec((C, lanes), lambda c, q: (c, q))
    out_mat = pl.BlockSpec((1, pairs_per_step, PAIR, PAIR), lambda c, q: (c, q, 0, 0))
    t = n_chunks * C
    return pl.pallas_call(
        functools.partial(_wkv_chunk_kernel, C, pairs_per_step),
        grid=(n_chunks, n_pairs // pairs_per_step),
        in_specs=[in_spec] * 6,
        out_specs=[out_row, out_row, out_mat, out_mat],
        out_shape=[jax.ShapeDtypeStruct((t, d), F32), jax.ShapeDtypeStruct((t, d), F32),
                   jax.ShapeDtypeStruct((n_chunks, n_pairs, PAIR, PAIR), F32),
                   jax.ShapeDtypeStruct((n_chunks, n_pairs, PAIR, PAIR), F32)],
        compiler_params=_params(("parallel", "parallel"), 32),
        name=f"wkv_chunks_c{C}",
    )(*arrs)


def _wkv_seq_kernel(n_pairs, n_steps, rp_ref, yp_ref, m_ref, n_ref, s0_ref, y_out, s_out, s_scr):
    j = pl.program_id(1)

    @pl.when(j == 0)
    def _():
        s_scr[...] = s0_ref[0]

    ys, new_s = [], []
    for p in range(n_pairs):
        sl = slice(PAIR * p, PAIR * (p + 1))
        ss = _split(s_scr[p])
        ys.append(_dot3(_split(rp_ref[:, sl]), ss) + yp_ref[:, sl])
        new_s.append(_dot3(_split(m_ref[0, p]), ss) + n_ref[0, p])
    y_out[...] = jnp.concatenate(ys, axis=1)
    s_scr[...] = jnp.stack(new_s, axis=0)

    @pl.when(j == n_steps - 1)
    def _():
        s_out[0] = s_scr[...]


def _wkv_seq(rp, yp, m, nn, s0, C, n_seq, n_steps):
    t, d = rp.shape
    n_pairs = d // PAIR
    row = pl.BlockSpec((C, d), lambda s, j: (s * n_steps + j, 0))
    mat = pl.BlockSpec((1, n_pairs, PAIR, PAIR), lambda s, j: (s * n_steps + j, 0, 0, 0))
    st = pl.BlockSpec((1, n_pairs, PAIR, PAIR), lambda s, j: (s, 0, 0, 0))
    return pl.pallas_call(
        functools.partial(_wkv_seq_kernel, n_pairs, n_steps),
        grid=(n_seq, n_steps),
        in_specs=[row, row, mat, mat, st],
        out_specs=[row, st],
        out_shape=[jax.ShapeDtypeStruct((t, d), F32),
                   jax.ShapeDtypeStruct((n_seq, n_pairs, PAIR, PAIR), F32)],
        scratch_shapes=[pltpu.VMEM((n_pairs, PAIR, PAIR), F32)],
        compiler_params=_params(("arbitrary", "arbitrary"), 32),
        name=f"wkv_seq_c{C}",
    )(rp, yp, m, nn, s0)


def _state_to_blockdiag(s):
    b, h, n, _ = s.shape
    st = jnp.swapaxes(s, -1, -2).reshape(b, h // 2, 2, n, n)
    z = jnp.zeros_like(st[:, :, 0])
    top = jnp.concatenate([st[:, :, 0], z], axis=-1)
    bot = jnp.concatenate([z, st[:, :, 1]], axis=-1)
    return jnp.concatenate([top, bot], axis=-2)


def _blockdiag_to_state(bd):
    b, hp, _, _ = bd.shape
    n = HEAD_DIM
    s0 = bd[:, :, :n, :n]
    s1 = bd[:, :, n:, n:]
    st = jnp.stack([s0, s1], axis=2).reshape(b, hp * 2, n, n)
    return jnp.swapaxes(st, -1, -2)


def _route(x1, rw_ref, rb_ref):
    logits = _dot3(_split(rw_ref[...]), _split(x1), _dot_nt)
    mx = jnp.max(logits, axis=0, keepdims=True)
    ex = jnp.exp(logits - mx)
    scores = ex / jnp.sum(ex, axis=0, keepdims=True)
    sel = scores + rb_ref[...]
    rows = [sel[e:e + 1, :] for e in range(N_EXPERTS)]
    srow = [scores[e:e + 1, :] for e in range(N_EXPERTS)]

    def top2(vals):
        m1 = jnp.maximum(jnp.maximum(vals[0], vals[1]), jnp.maximum(vals[2], vals[3]))
        i1 = jnp.where(vals[0] == m1, 0, jnp.where(vals[1] == m1, 1, jnp.where(vals[2] == m1, 2, 3)))
        rest = [jnp.where(i1 == j, -jnp.inf, vals[j]) for j in range(4)]
        m2 = jnp.maximum(jnp.maximum(rest[0], rest[1]), jnp.maximum(rest[2], rest[3]))
        i2 = jnp.where(rest[0] == m2, 0, jnp.where(rest[1] == m2, 1, jnp.where(rest[2] == m2, 2, 3)))
        return m1, i1, m2, i2

    gscore = []
    for gidx in range(N_GROUPS):
        m1, _, m2, _ = top2(rows[4 * gidx:4 * gidx + 4])
        gscore.append(m1 + m2)
    gm = jnp.maximum(jnp.maximum(gscore[0], gscore[1]), jnp.maximum(gscore[2], gscore[3]))
    gi = jnp.where(gscore[0] == gm, 0, jnp.where(gscore[1] == gm, 1, jnp.where(gscore[2] == gm, 2, 3)))

    def pick(rws, j):
        return jnp.where(gi == 0, rws[j], jnp.where(gi == 1, rws[4 + j],
                                                    jnp.where(gi == 2, rws[8 + j], rws[12 + j])))

    in_grp = [pick(rows, j) for j in range(4)]
    sc_grp = [pick(srow, j) for j in range(4)]
    _, i1, _, i2 = top2(in_grp)

    def at(vals, idx):
        return jnp.where(idx == 0, vals[0], jnp.where(idx == 1, vals[1],
                                                      jnp.where(idx == 2, vals[2], vals[3])))

    ga = at(sc_grp, i1)
    gb = at(sc_grp, i2)
    tot = ga + gb
    ga = ga / tot
    gb = gb / tot
    lo = jnp.minimum(i1, i2)
    hi = jnp.maximum(i1, i2)
    g_lo = jnp.where(i1 < i2, ga, gb)
    g_hi = jnp.where(i1 < i2, gb, ga)
    pair = jnp.where(lo == 0, hi - 1, jnp.where(lo == 1, hi + 1, 5))
    cls = gi * 6 + pair
    return cls.astype(jnp.int32), g_lo, g_hi


def _mix_post_kernel(is_rwkv, alpha_dn, *refs):
    if is_rwkv:
        (x_ref, y_ref, r_ref, k_ref, v_ref, g_ref, lg_ref, lb_ref, rk_ref, bd_ref,
         wo_ref, n1g_ref, n1b_ref, rw_ref, rb_ref, x1_out, cls_out, gate_out) = refs
        y = y_ref[...]
        mu = _head_sum(y, bd_ref) * (1.0 / HEAD_DIM)
        yc = y - mu
        var = _head_sum(yc * yc, bd_ref) * (1.0 / HEAD_DIM)
        yn = yc * lax.rsqrt(var + GN_EPS) * lg_ref[...] + lb_ref[...]
        v = v_ref[...]
        bonus = _head_sum(r_ref[...] * k_ref[...] * rk_ref[...], bd_ref)
        z = (yn + bonus * v) * g_ref[...]
    else:
        (x_ref, o_ref, gt_ref, wo_ref, n1g_ref, n1b_ref, rw_ref, rb_ref,
         x1_out, cls_out, gate_out) = refs
        z = o_ref[...] * gt_ref[...]
    h = _dot(z.astype(BF16), wo_ref[...])
    x1 = _layer_norm(alpha_dn * x_ref[...] + h, n1g_ref[...], n1b_ref[...])
    x1_out[...] = x1
    cls, g_lo, g_hi = _route(x1, rw_ref, rb_ref)
    tm = x1.shape[0]
    cls_out[...] = jnp.broadcast_to(cls, (8, tm))
    gate_out[...] = jnp.concatenate([g_lo, g_hi, jnp.zeros((6, tm), F32)], axis=0)


def _mix_post(is_rwkv, alpha_dn, acts, consts, tm):
    n, d = acts[0].shape
    row = pl.BlockSpec((tm, d), lambda i: (i, 0))
    specs = [row] * len(acts) + [_const_spec(c.shape) for c in consts]
    lane = pl.BlockSpec((8, tm), lambda i: (0, i))
    return pl.pallas_call(
        functools.partial(_mix_post_kernel, is_rwkv, alpha_dn),
        grid=(n // tm,),
        in_specs=specs,
        out_specs=[row, lane, lane],
        out_shape=[jax.ShapeDtypeStruct((n, d), F32), jax.ShapeDtypeStruct((8, n), jnp.int32),
                   jax.ShapeDtypeStruct((8, n), F32)],
        compiler_params=_params(("parallel",), 48),
        name="mix_post_rwkv" if is_rwkv else "mix_post_fox",
    )(*acts, *consts)


def _moe_kernel(n_tok, d_exp, e1_ref, e2_ref, valid_ref, tok_ref, x_hbm, glo_ref, ghi_ref,
                wgu1_ref, wd1_ref, wgu2_ref, wd2_ref, out_hbm, xbuf, obuf, gsem, ssem):
    b = pl.program_id(0)
    base = b * E_BLOCK

    def gather_copy(i):
        tok = jnp.minimum(tok_ref[base + i], n_tok - 1)
        return pltpu.make_async_copy(x_hbm.at[pl.ds(tok, 1)], xbuf.at[pl.ds(i, 1)], gsem)

    def scatter_copy(tok, i):
        return pltpu.make_async_copy(obuf.at[pl.ds(i, 1)], out_hbm.at[pl.ds(tok, 1)], ssem)

    @pl.when(valid_ref[b] != 0)
    def _():
        def start_g(i, c):
            gather_copy(i).start()
            return c

        def wait_g(i, c):
            gather_copy(i).wait()
            return c

        lax.fori_loop(0, E_BLOCK, start_g, 0)
        lax.fori_loop(0, E_BLOCK, wait_g, 0)
        xb = xbuf[...].astype(BF16)

        def expert(wgu_ref, wd_ref):
            gu = _dot(xb, wgu_ref[0])
            hmid = jax.nn.silu(gu[:, :d_exp]) * gu[:, d_exp:]
            return _dot(hmid.astype(BF16), wd_ref[0])

        y1 = expert(wgu1_ref, wd1_ref)
        y2 = expert(wgu2_ref, wd2_ref)
        obuf[...] = y1 * glo_ref[...] + y2 * ghi_ref[...]

        def start_s(i, c):
            tok = tok_ref[base + i]

            @pl.when(tok < n_tok)
            def _():
                scatter_copy(tok, i).start()
            return c

        def wait_s(i, c):
            tok = tok_ref[base + i]

            @pl.when(tok < n_tok)
            def _():
                scatter_copy(tok, i).wait()
            return c

        lax.fori_loop(0, E_BLOCK, start_s, 0)
        lax.fori_loop(0, E_BLOCK, wait_s, 0)


def _moe(x1, cls, g_lo, g_hi, w_gu, w_down):
    n, d = x1.shape
    d_exp = w_down.shape[1]
    n_blk = (n + N_CLASSES * (E_BLOCK - 1) + E_BLOCK - 1) // E_BLOCK
    n_slot = n_blk * E_BLOCK
    counts = jnp.bincount(cls, length=N_CLASSES)
    padded = (counts + E_BLOCK - 1) // E_BLOCK * E_BLOCK
    pad_end = jnp.cumsum(padded)
    pad_start = pad_end - padded
    seg_start = jnp.cumsum(counts) - counts
    order = jnp.argsort(cls, stable=True).astype(jnp.int32)
    sc = cls[order]
    dest = pad_start[sc] + jnp.arange(n, dtype=jnp.int32) - seg_start[sc]
    slot_tok = jnp.full((n_slot,), n, jnp.int32).at[dest].set(order)
    slot_lo = jnp.zeros((n_slot,), F32).at[dest].set(g_lo[order]).reshape(n_slot, 1)
    slot_hi = jnp.zeros((n_slot,), F32).at[dest].set(g_hi[order]).reshape(n_slot, 1)
    blk_start = jnp.arange(n_blk, dtype=jnp.int32) * E_BLOCK
    blk_cls = jnp.minimum(jnp.searchsorted(pad_end, blk_start, side='right'), N_CLASSES - 1).astype(jnp.int32)
    blk_valid = (blk_start < pad_end[-1]).astype(jnp.int32)
    last_cls = blk_cls[jnp.maximum(jnp.sum(blk_valid) - 1, 0)]
    blk_cls = jnp.where(blk_valid != 0, blk_cls, last_cls)
    pair_lo = jnp.asarray(_PAIR_LO, jnp.int32)
    pair_hi = jnp.asarray(_PAIR_HI, jnp.int32)
    blk_e1 = (blk_cls // 6) * EXPERTS_PER_GROUP + pair_lo[blk_cls % 6]
    blk_e2 = (blk_cls // 6) * EXPERTS_PER_GROUP + pair_hi[blk_cls % 6]

    gate_spec = pl.BlockSpec((E_BLOCK, 1), lambda b, e1, e2, vl, tk: (b, 0))
    gu1 = pl.BlockSpec((1, d, 2 * d_exp), lambda b, e1, e2, vl, tk: (e1[b], 0, 0))
    dn1 = pl.BlockSpec((1, d_exp, d), lambda b, e1, e2, vl, tk: (e1[b], 0, 0))
    gu2 = pl.BlockSpec((1, d, 2 * d_exp), lambda b, e1, e2, vl, tk: (e2[b], 0, 0))
    dn2 = pl.BlockSpec((1, d_exp, d), lambda b, e1, e2, vl, tk: (e2[b], 0, 0))
    grid_spec = pltpu.PrefetchScalarGridSpec(
        num_scalar_prefetch=4,
        grid=(n_blk,),
        in_specs=[pl.BlockSpec(memory_space=pl.ANY), gate_spec, gate_spec, gu1, dn1, gu2, dn2],
        out_specs=pl.BlockSpec(memory_space=pl.ANY),
        scratch_shapes=[pltpu.VMEM((E_BLOCK, d), F32), pltpu.VMEM((E_BLOCK, d), F32),
                        pltpu.SemaphoreType.DMA(()), pltpu.SemaphoreType.DMA(())],
    )
    return pl.pallas_call(
        functools.partial(_moe_kernel, n, d_exp),
        grid_spec=grid_spec,
        out_shape=jax.ShapeDtypeStruct((n, d), F32),
        compiler_params=_params(("arbitrary",), 40),
        name="moe_experts",
    )(blk_e1, blk_e2, blk_valid, slot_tok, x1, slot_lo, slot_hi, w_gu, w_down, w_gu, w_down)


def _ln2_kernel(alpha_dn, with_kv, *refs):
    if with_kv:
        (x1_ref, m_ref, g_ref, b_ref, wk_ref, wv_ref, wf_ref, bf_ref, tri_ref,
         x2_out, k_out, v_out, kb_out, vb_out, lf_out, cum_out, carry) = refs
    else:
        x1_ref, m_ref, g_ref, b_ref, x2_out = refs
    x2 = _layer_norm(alpha_dn * x1_ref[...] + m_ref[...], g_ref[...], b_ref[...])
    x2_out[...] = x2
    if with_kv:
        xb = x2.astype(BF16)
        k = _dot(xb, wk_ref[...])
        v = _dot(xb, wv_ref[...])
        k_out[...] = k
        v_out[...] = v
        kb_out[...] = k.astype(BF16)
        vb_out[...] = v.astype(BF16)
        z = _dot3(_split(wf_ref[...]), _split(x2), _dot_nt) + bf_ref[...]
        lf = jnp.minimum(z, 0.0) - jnp.log1p(jnp.exp(-jnp.abs(z)))
        lf_out[...] = lf

        @pl.when(pl.program_id(0) == 0)
        def _():
            carry[...] = jnp.zeros_like(carry)

        cum = _dot_exact_rhs(lf, tri_ref[...]) + carry[...]
        cum_out[...] = cum
        carry[...] = cum[:, -1:]


def _ln2(alpha_dn, x1, m, g, b, kv, tm):
    n, d = x1.shape
    row = pl.BlockSpec((tm, d), lambda i: (i, 0))
    with_kv = kv is not None
    ins = [x1, m, g, b]
    specs = [row, row, _const_spec(g.shape), _const_spec(b.shape)]
    outs = [jax.ShapeDtypeStruct((n, d), F32)]
    ospecs = [row]
    scratch = []
    if with_kv:
        nh = kv['wf_t'].shape[0]
        tri = (jnp.arange(tm)[:, None] <= jnp.arange(tm)[None, :]).astype(BF16)
        extra = [kv['wk'], kv['wv'], kv['wf_t'], kv['bf'], tri]
        ins += extra
        specs += [_const_spec(e.shape) for e in extra]
        lane = pl.BlockSpec((nh, tm), lambda i: (0, i))
        outs += [jax.ShapeDtypeStruct((n, d), F32), jax.ShapeDtypeStruct((n, d), F32),
                 jax.ShapeDtypeStruct((n, d), BF16), jax.ShapeDtypeStruct((n, d), BF16),
                 jax.ShapeDtypeStruct((nh, n), F32), jax.ShapeDtypeStruct((nh, n), F32)]
        ospecs += [row, row, row, row, lane, lane]
        scratch = [pltpu.VMEM((nh, 1), F32)]
    res = pl.pallas_call(
        functools.partial(_ln2_kernel, alpha_dn, with_kv),
        grid=(n // tm,),
        in_specs=specs,
        out_specs=ospecs,
        out_shape=outs,
        scratch_shapes=scratch,
        compiler_params=_params(("arbitrary",), 48),
        name="ln2_kv" if with_kv else "ln2",
    )(*ins)
    return res


def _fox_qg_kernel(x_ref, wq_ref, wg_ref, q_out, gate_out):
    xb = x_ref[...].astype(BF16)
    q_out[...] = (_dot(xb, wq_ref[...]) * (LOG2E * HEAD_DIM ** -0.5)).astype(BF16)
    gate_out[...] = jax.nn.sigmoid(_dot(xb, wg_ref[...]))


def _fox_qg(x, wq, wg, tm):
    n, d = x.shape
    row = pl.BlockSpec((tm, d), lambda i: (i, 0))
    return pl.pallas_call(
        _fox_qg_kernel,
        grid=(n // tm,),
        in_specs=[row, _const_spec(wq.shape), _const_spec(wg.shape)],
        out_specs=[row, row],
        out_shape=[jax.ShapeDtypeStruct((n, d), BF16), jax.ShapeDtypeStruct((n, d), F32)],
        compiler_params=_params(("parallel",), 40),
        name="fox_qg",
    )(x, wq, wg)


V_ROWS = 80


def _fox_prompt_kernel(tq, hps, q_ref, k_ref, vt_ref, o_out, s_scr, p_scr):
    i = pl.program_id(1)
    rr = lax.broadcasted_iota(jnp.int32, (tq, tq), 0)
    cc = lax.broadcasted_iota(jnp.int32, (tq, tq), 1)
    causal = rr <= cc

    def scores(j, h):
        k0 = pl.multiple_of(j * tq, tq)
        lanes = slice(PAIR * h, PAIR * (h + 1))
        return _dot_nt(k_ref[pl.ds(k0, tq), lanes], q_ref[:, lanes])

    def softmax(s, m):
        m_new = jnp.maximum(m, jnp.max(s, axis=0, keepdims=True))
        return jnp.exp2(s - m_new).astype(BF16), jnp.exp2(m - m_new), m_new

    def accum(acc, corr, p, j, h):
        k0 = pl.multiple_of(j * tq, tq)
        return acc * corr + _dot(vt_ref[h, :, pl.ds(k0, tq)], p)

    for h in range(hps):
        s_scr[h] = scores(0, h)
        p_scr[h] = jnp.zeros((tq, tq), BF16)

    def body(j, carry):
        new = []
        for h in range(hps):
            m, acc, corr_prev = carry[h]
            acc = accum(acc, corr_prev, p_scr[h], jnp.maximum(j - 1, 0), h)
            p, corr, m = softmax(s_scr[h], m)
            p_scr[h] = p
            s_scr[h] = scores(j + 1, h)
            new.append((m, acc, corr))
        return tuple(new)

    init = tuple((jnp.full((1, tq), NEG_INF, F32), jnp.zeros((V_ROWS, tq), F32), jnp.ones((1, tq), F32))
                 for _ in range(hps))
    carry = lax.fori_loop(0, i, body, init)
    outs = []
    for h in range(hps):
        m, acc, corr_prev = carry[h]
        acc = accum(acc, corr_prev, p_scr[h], jnp.maximum(i - 1, 0), h)
        p, corr, m = softmax(jnp.where(causal, s_scr[h], NEG_INF), m)
        acc = accum(acc, corr, p, i, h)
        outs.append(acc[:HEAD_DIM, :] / acc[HEAD_DIM:HEAD_DIM + 1, :])
    o_out[...] = jnp.concatenate(outs, axis=0)


def _fox_prompt(q_aug, k_aug, vt_aug, tq, hps):
    t = q_aug.shape[0]
    nh = vt_aug.shape[0]
    return pl.pallas_call(
        functools.partial(_fox_prompt_kernel, tq, hps),
        grid=(nh // hps, t // tq),
        in_specs=[pl.BlockSpec((tq, hps * PAIR), lambda g, i: (i, g)),
                  pl.BlockSpec((t, hps * PAIR), lambda g, i: (0, g), pipeline_mode=pl.Buffered(1)),
                  pl.BlockSpec((hps, V_ROWS, t), lambda g, i: (g, 0, 0), pipeline_mode=pl.Buffered(1))],
        out_specs=pl.BlockSpec((hps * HEAD_DIM, tq), lambda g, i: (g, i)),
        out_shape=jax.ShapeDtypeStruct((nh * HEAD_DIM, t), F32),
        scratch_shapes=[pltpu.VMEM((hps, tq, tq), F32), pltpu.VMEM((hps, tq, tq), BF16)],
        compiler_params=_params(("parallel", "parallel"), 48),
        name="fox_prompt_attn",
    )(q_aug, k_aug, vt_aug)


def _augment_q(qb, cum2):
    t, d = qb.shape
    nh = d // HEAD_DIM
    hi, mid, lo = _split3(cum2)
    one = jnp.ones((t, nh), BF16)
    extra = jnp.stack([one, one, one, hi, mid, lo], axis=-1)
    pad = jnp.zeros((t, nh, HEAD_DIM - 6), BF16)
    return jnp.concatenate([qb.reshape(t, nh, HEAD_DIM), extra, pad], axis=-1).reshape(t, nh * PAIR)


def _augment_k(kb, cum2):
    t, d = kb.shape
    nh = d // HEAD_DIM
    hi, mid, lo = _split3(-cum2)
    one = jnp.ones((t, nh), BF16)
    extra = jnp.stack([hi, mid, lo, one, one, one], axis=-1)
    pad = jnp.zeros((t, nh, HEAD_DIM - 6), BF16)
    return jnp.concatenate([kb.reshape(t, nh, HEAD_DIM), extra, pad], axis=-1).reshape(t, nh * PAIR)


def _augment_vt(vb):
    t, d = vb.shape
    nh = d // HEAD_DIM
    vt = vb.reshape(t, nh, HEAD_DIM).transpose(1, 2, 0)
    return jnp.concatenate([vt, jnp.ones((nh, 1, t), BF16),
                            jnp.zeros((nh, V_ROWS - HEAD_DIM - 1, t), BF16)], axis=1)


def _fox_sample_kernel(n_pairs, q_ref, kn_ref, vn_ref, lfn_ref, kc_ref, vc_ref, lfc_ref,
                       triu_ref, o_out):
    t = q_ref.shape[0]
    past = kc_ref.shape[1]
    lfc = lfc_ref[0]
    lfn = lfn_ref[0]
    nh = lfc.shape[1]
    lfc_s = _split3(lfc)
    cum_c_r = _dot_tn3(lfc_s, triu_ref[...])
    ones_row = jnp.ones((1, past), BF16)
    tot_r = _dot_m3(ones_row, lfc_s)
    tot_c = cum_c_r[:, past - 1:past]
    rr = lax.broadcasted_iota(jnp.int32, (t, t), 0)
    cc = lax.broadcasted_iota(jnp.int32, (t, t), 1)
    causal = rr >= cc
    tril = jnp.where(causal, 1.0, 0.0).astype(BF16)
    lfn_s = _split3(lfn)
    cum_n_c = (_dot_m3(tril, lfn_s) + tot_r) * LOG2E
    cum_n_r = (_dot_tn3(lfn_s, jnp.where(rr <= cc, 1.0, 0.0).astype(BF16)) + tot_c) * LOG2E
    cum_c_r = cum_c_r * LOG2E
    lane = lax.broadcasted_iota(jnp.int32, (1, PAIR), 1)
    head0 = lane < HEAD_DIM
    hl = lax.broadcasted_iota(jnp.int32, (1, nh), 1)
    for p in range(n_pairs):
        sl = slice(PAIR * p, PAIR * (p + 1))
        q = q_ref[:, sl]
        zero = jnp.zeros_like(q)
        kc = kc_ref[0, :, sl].astype(BF16)
        vc = vc_ref[0, :, sl].astype(BF16)
        kn = kn_ref[:, sl].astype(BF16)
        vn = vn_ref[:, sl].astype(BF16)
        outs = []
        for h in range(2):
            hh = 2 * p + h
            qh = jnp.where(head0, q, zero) if h == 0 else jnp.where(head0, zero, q)
            cq = jnp.sum(jnp.where(hl == hh, cum_n_c, 0.0), axis=1, keepdims=True)
            s_c = _dot_nt(qh, kc) + cq - cum_c_r[hh:hh + 1, :]
            s_n = _dot_nt(qh, kn) + cq - cum_n_r[hh:hh + 1, :]
            s_n = jnp.where(causal, s_n, NEG_INF)
            m = jnp.maximum(jnp.max(s_c, axis=1, keepdims=True), jnp.max(s_n, axis=1, keepdims=True))
            p_c = jnp.exp2(s_c - m)
            p_n = jnp.exp2(s_n - m)
            l = jnp.sum(p_c, axis=1, keepdims=True) + jnp.sum(p_n, axis=1, keepdims=True)
            acc = _dot(p_c.astype(BF16), vc) + _dot(p_n.astype(BF16), vn)
            outs.append(acc / l)
        o_out[:, sl] = jnp.where(head0, outs[0], outs[1])


def _split3(x):
    hi = x.astype(BF16)
    r1 = x - hi.astype(F32)
    mid = r1.astype(BF16)
    lo = (r1 - mid.astype(F32)).astype(BF16)
    return hi, mid, lo


def _dot_tn3(xs, m):
    return _dot_tn(xs[0], m) + (_dot_tn(xs[1], m) + _dot_tn(xs[2], m))


def _dot_m3(m, xs):
    return _dot(m, xs[0]) + (_dot(m, xs[1]) + _dot(m, xs[2]))


def _fox_sample(qb, k, v, lf_new, cache_k, cache_v, cache_lf, row0, n_stream, t):
    d = qb.shape[1]
    n_pairs = d // PAIR
    past = cache_k.shape[1]
    nh = cache_lf.shape[2]
    blk0 = row0 // t
    triu = (jnp.arange(past)[:, None] <= jnp.arange(past)[None, :]).astype(BF16)
    row = pl.BlockSpec((t, d), lambda b: (blk0 + b, 0))
    return pl.pallas_call(
        functools.partial(_fox_sample_kernel, n_pairs),
        grid=(n_stream,),
        in_specs=[row, row, row,
                  pl.BlockSpec((1, t, nh), lambda b: (b, 0, 0)),
                  pl.BlockSpec((1, past, d), lambda b: (b, 0, 0)),
                  pl.BlockSpec((1, past, d), lambda b: (b, 0, 0)),
                  pl.BlockSpec((1, past, nh), lambda b: (b, 0, 0)),
                  _const_spec(triu.shape)],
        out_specs=pl.BlockSpec((t, d), lambda b: (b, 0)),
        out_shape=jax.ShapeDtypeStruct((n_stream * t, d), F32),
        compiler_params=_params(("parallel",), 48),
        name="fox_sample_attn",
    )(qb, k, v, lf_new, cache_k, cache_v, cache_lf, triu)


def kernel(x_prompt, x_sample, state_shift, state_wkv, cache_k, cache_v, cache_logf, ln1_g, ln1_b, ln2_g, ln2_b, a_mix, a_w_rkv, a_w0, a_w1, a_w2, a_a0, a_a1, a_a2, a_v0, a_v1, a_v2, a_g1, a_g2, a_k_k, a_k_a, a_r_k, a_lnx_g, a_lnx_b, a_w_o, kv_w, kv_bf, b_w_qg, b_w_o, router_w, router_b, moe_w_gu, moe_w_down):
    nb, seq, d = x_prompt.shape
    db, dt, _ = x_sample.shape
    assert nb == 1
    n_heads = d // HEAD_DIM
    depth = ln1_g.shape[0]
    n_a = a_mix.shape[0]
    past = cache_k.shape[1]
    t_p = nb * seq
    t_s = db * dt
    n = t_p + t_s
    tm = _row_tile(n)
    c_p = 64
    c_s = dt
    assert seq % c_p == 0 and t_p % c_s == 0 and (c_s & (c_s - 1)) == 0
    alpha_dn = (2 * depth) ** 0.25
    row2 = lambda a: a.reshape(1, -1)

    bd = (jnp.arange(256)[:, None] // HEAD_DIM == jnp.arange(256)[None, :] // HEAD_DIM).astype(BF16)
    rw_t = router_w.T
    rb_c = router_b.reshape(-1, 1)

    x = jnp.concatenate([x_prompt.reshape(t_p, d), x_sample.reshape(t_s, d)], axis=0)
    new_shift_p, new_shift_s, new_wkv_p, new_wkv_s = [], [], [], []
    v_first = None
    kv = None
    for l in range(depth):
        if l < n_a:
            xs = x[t_p:].reshape(db, dt, d)
            new_shift_p.append(x[t_p - 1:t_p].reshape(nb, d))
            new_shift_s.append(xs[:, -1])
            xp = jnp.concatenate([
                jnp.zeros((1, d), F32), x[:t_p - 1],
                jnp.concatenate([state_shift[l][:, None, :], xs[:, :-1]], axis=1).reshape(t_s, d)], axis=0)
            w = dict(mix=jnp.concatenate([a_mix[l], jnp.zeros((2, d), F32)], axis=0),
                     wr=a_w_rkv[l, 0].astype(BF16), wk=a_w_rkv[l, 1].astype(BF16), wv=a_w_rkv[l, 2].astype(BF16),
                     w1=a_w1[l].astype(BF16), w2=a_w2[l].astype(BF16), w0=row2(a_w0[l]),
                     a1=a_a1[l].astype(BF16), a2=a_a2[l].astype(BF16), a0=row2(a_a0[l]),
                     g1=a_g1[l].astype(BF16), g2=a_g2[l].astype(BF16),
                     k_k=row2(a_k_k[l]), k_a=row2(a_k_a[l]), bd=bd)
            if l > 0:
                w.update(v1=a_v1[l - 1].astype(BF16), v2=a_v2[l - 1].astype(BF16), v0=row2(a_v0[l - 1]))
            r, lw, k, v, kn, b, g = _rwkv_pre(x, xp, v_first if l > 0 else None, w, tm)
            if l == 0:
                v_first = v
            scan_in = (r, lw, k, v, kn, b)
            rp, yp, mm, nn = _wkv_chunks(scan_in, c_p, 0, t_p // c_p, n_heads // 2)
            s0 = jnp.zeros((1, n_heads // 2, PAIR, PAIR), F32)
            y_p, sf_p = _wkv_seq(rp, yp, mm, nn, s0, c_p, 1, t_p // c_p)
            rp, yp, mm, nn = _wkv_chunks(scan_in, c_s, t_p, db, n_heads // 2)
            y_s, sf_s = _wkv_seq(rp, yp, mm, nn, _state_to_blockdiag(state_wkv[l]), c_s, db, 1)
            new_wkv_p.append(_blockdiag_to_state(sf_p))
            new_wkv_s.append(_blockdiag_to_state(sf_s))
            y = jnp.concatenate([y_p, y_s], axis=0)
            consts = [row2(a_lnx_g[l]), row2(a_lnx_b[l]), row2(a_r_k[l]), bd, a_w_o[l].astype(BF16),
                      row2(ln1_g[l]), row2(ln1_b[l]), rw_t, rb_c]
            x1, cls, gates = _mix_post(True, alpha_dn, [x, y, r, k, v, g], consts, tm)
        else:
            lb = l - n_a
            qb, gate = _fox_qg(x, b_w_qg[lb][:, :d].astype(BF16), b_w_qg[lb][:, d:].astype(BF16), tm)
            o_p = _fox_prompt(_augment_q(qb[:t_p], kv['cum2']), kv['k_aug'], kv['vt_aug'], 512, 4).T
            o_s = _fox_sample(qb, kv['k'], kv['v'], kv['lf_s'], cache_k.reshape(db, past, d),
                              cache_v.reshape(db, past, d), cache_logf, t_p, db, dt)
            o = jnp.concatenate([o_p, o_s], axis=0)
            consts = [b_w_o[lb].astype(BF16), row2(ln1_g[l]), row2(ln1_b[l]), rw_t, rb_c]
            x1, cls, gates = _mix_post(False, alpha_dn, [x, o, gate], consts, tm)
        m = _moe(x1, cls[0], gates[0], gates[1], moe_w_gu[l].astype(BF16), moe_w_down[l].astype(BF16))
        if l == n_a - 1:
            kvw = dict(wk=kv_w[:, :d].astype(BF16), wv=kv_w[:, d:2 * d].astype(BF16),
                       wf_t=kv_w[:, 2 * d:].T, bf=kv_bf.reshape(-1, 1))
            x, k_all, v_all, kb_all, vb_all, lf_r, cum_r = _ln2(
                alpha_dn, x1, m, row2(ln2_g[l]), row2(ln2_b[l]), kvw, tm)
            cum2 = cum_r[:, :t_p].T * LOG2E
            kv = dict(k=k_all, v=v_all, cum2=cum2, k_aug=_augment_k(kb_all[:t_p], cum2),
                      vt_aug=_augment_vt(vb_all[:t_p]), lf_p=lf_r[:, :t_p].T,
                      lf_s=lf_r[:, t_p:].T.reshape(db, dt, n_heads))
        else:
            x = _ln2(alpha_dn, x1, m, row2(ln2_g[l]), row2(ln2_b[l]), None, tm)[0]

    y_prompt = x[:t_p].reshape(nb, seq, d)
    y_sample = x[t_p:].reshape(db, dt, d)
    p_k = kv['k'][:t_p].reshape(nb, seq, n_heads, HEAD_DIM)
    p_v = kv['v'][:t_p].reshape(nb, seq, n_heads, HEAD_DIM)
    s_k = kv['k'][t_p:].reshape(db, dt, n_heads, HEAD_DIM)
    s_v = kv['v'][t_p:].reshape(db, dt, n_heads, HEAD_DIM)
    return (y_prompt, y_sample, jnp.stack(new_shift_p), jnp.stack(new_wkv_p), p_k, p_v,
            kv['lf_p'].reshape(nb, seq, n_heads), jnp.stack(new_shift_s), jnp.stack(new_wkv_s),
            s_k, s_v, kv['lf_s'])
```

```python
import functools
import math

import jax
import jax.numpy as jnp
import numpy as np
from jax import lax
from jax.experimental import pallas as pl
from jax.experimental.pallas import tpu as pltpu

F32 = jnp.float32
BF16 = jnp.bfloat16

HEAD_DIM = 64
PAIR = 2 * HEAD_DIM
N_EXPERTS = 16
N_GROUPS = 4
EXPERTS_PER_GROUP = 4
N_CLASSES = 24
E_BLOCK = 128
GATE_LANES = 128
LN_EPS = 1e-5
GN_EPS = 64e-5
NEG_INF = -1e30
LOG2E = 1.4426950408889634
MIB = 2 ** 20

_PAIR_LO = (0, 0, 0, 1, 1, 2)
_PAIR_HI = (1, 2, 3, 2, 3, 3)


def _params(sem, vmem_mib):
    return pltpu.CompilerParams(dimension_semantics=sem, vmem_limit_bytes=vmem_mib * MIB)


def _dot(a, b):
    return jnp.dot(a, b, preferred_element_type=F32)


def _dot_nt(a, b):
    return lax.dot_general(a, b, (((1,), (1,)), ((), ())), preferred_element_type=F32)


def _dot_tn(a, b):
    return lax.dot_general(a, b, (((0,), (0,)), ((), ())), preferred_element_type=F32)


def _split(x):
    hi = x.astype(BF16)
    lo = (x - hi.astype(F32)).astype(BF16)
    return hi, lo


def _dot3(a, b, dot=_dot):
    return _dot3_multi(a, [b], dot)[0]


def _dot3_multi(a, bs, dot=_dot):
    ah, al = a
    ax = 0 if dot is _dot_nt else 1
    ns = [b[0].shape[ax] for b in bs]
    if any(n % 128 for n in ns):
        return [dot(ah, bh) + (dot(ah, bl) + dot(al, bh)) for bh, bl in bs]
    r1 = dot(ah, jnp.concatenate([x for b in bs for x in b], axis=ax))
    r2 = dot(al, jnp.concatenate([b[0] for b in bs], axis=ax)) if len(bs) > 1 else dot(al, bs[0][0])
    outs, o1, o2 = [], 0, 0
    for n in ns:
        outs.append((r1[:, o1:o1 + n] + r1[:, o1 + n:o1 + 2 * n]) + r2[:, o2:o2 + n])
        o1 += 2 * n
        o2 += n
    return outs


def _dot_exact_rhs(x, m, dot=_dot):
    hi = x.astype(BF16)
    r1 = x - hi.astype(F32)
    mid = r1.astype(BF16)
    lo = (r1 - mid.astype(F32)).astype(BF16)
    return dot(hi, m) + (dot(mid, m) + dot(lo, m))


def _head_sum(x, bd_ref):
    bd = bd_ref[...]
    parts = []
    for j in range(x.shape[1] // 256):
        parts.append(_dot_exact_rhs(x[:, 256 * j:256 * (j + 1)], bd))
    return jnp.concatenate(parts, axis=1)


def _layer_norm(z, g, b):
    mu = jnp.mean(z, axis=-1, keepdims=True)
    zc = z - mu
    var = jnp.mean(zc * zc, axis=-1, keepdims=True)
    return zc * lax.rsqrt(var + LN_EPS) * g + b


def _row_tile(n):
    for t in (256, 128, 64, 32, 16, 8):
        if n % t == 0:
            return t
    raise ValueError(n)


def _const_spec(shape):
    nd = len(shape)
    return pl.BlockSpec(shape, lambda *_: (0,) * nd)


def _rwkv_pre_kernel(has_vres, *refs):
    if has_vres:
        (x_ref, xp_ref, vf_ref, mix_ref, wr_ref, wk_ref, wv_ref, w1_ref, w2_ref, w0_ref,
         a1_ref, a2_ref, a0_ref, g1_ref, g2_ref, v1_ref, v2_ref, v0_ref, kk_ref, ka_ref, bd_ref,
         r_out, lw_out, k_out, v_out, kn_out, b_out, g_out) = refs
    else:
        (x_ref, xp_ref, mix_ref, wr_ref, wk_ref, wv_ref, w1_ref, w2_ref, w0_ref,
         a1_ref, a2_ref, a0_ref, g1_ref, g2_ref, kk_ref, ka_ref, bd_ref,
         r_out, lw_out, k_out, v_out, kn_out, b_out, g_out) = refs
    x = x_ref[...]
    xx = xp_ref[...] - x

    def mixed(i):
        return (x + xx * mix_ref[i:i + 1, :]).astype(BF16)

    xr, xw, xk, xv, xa, xg = (mixed(i) for i in range(6))
    r = _dot(xr, wr_ref[...])
    k = _dot(xk, wk_ref[...])
    v = _dot(xv, wv_ref[...])
    zw = w0_ref[...] + _dot(jnp.tanh(_dot(xw, w1_ref[...])).astype(BF16), w2_ref[...])
    lw = (-math.exp(-0.5)) * jax.nn.sigmoid(zw)
    alpha = jax.nn.sigmoid(a0_ref[...] + _dot(_dot(xa, a1_ref[...]).astype(BF16), a2_ref[...]))
    g = _dot(jax.nn.sigmoid(_dot(xg, g1_ref[...])).astype(BF16), g2_ref[...])
    if has_vres:
        gate_v = jax.nn.sigmoid(v0_ref[...] + _dot(_dot(xv, v1_ref[...]).astype(BF16), v2_ref[...]))
        v = v + (vf_ref[...] - v) * gate_v
    kkr = k * kk_ref[...]
    norm = jnp.sqrt(_head_sum(kkr * kkr, bd_ref))
    kn = kkr / jnp.maximum(norm, 1e-12)
    k2 = k * (1.0 + (alpha - 1.0) * ka_ref[...])
    r_out[...] = r
    lw_out[...] = lw
    k_out[...] = k2
    v_out[...] = v
    kn_out[...] = kn
    b_out[...] = kn * alpha
    g_out[...] = g


def _rwkv_pre(x, xp, vfirst, w, tm):
    n, d = x.shape
    has_vres = vfirst is not None
    row = pl.BlockSpec((tm, d), lambda i: (i, 0))
    ins = [x, xp] + ([vfirst] if has_vres else [])
    specs = [row, row] + ([row] if has_vres else [])
    names = ['mix', 'wr', 'wk', 'wv', 'w1', 'w2', 'w0', 'a1', 'a2', 'a0', 'g1', 'g2']
    if has_vres:
        names += ['v1', 'v2', 'v0']
    names += ['k_k', 'k_a', 'bd']
    for nm in names:
        ins.append(w[nm])
        specs.append(_const_spec(w[nm].shape))
    out = jax.ShapeDtypeStruct((n, d), F32)
    return pl.pallas_call(
        functools.partial(_rwkv_pre_kernel, has_vres),
        grid=(n // tm,),
        in_specs=specs,
        out_specs=[row] * 7,
        out_shape=[out] * 7,
        compiler_params=_params(("parallel",), 56),
        name="rwkv_pre",
    )(*ins)


def _wkv_chunk_kernel(C, n_pairs, r_ref, lw_ref, k_ref, v_ref, kn_ref, b_ref,
                      rp_out, yp_out, m_out, n_out):
    C2 = 2 * C
    row = lax.broadcasted_iota(jnp.int32, (C2, C2), 0)
    col = lax.broadcasted_iota(jnp.int32, (C2, C2), 1)
    same = (row >= C) == (col >= C)
    strict = jnp.logical_and(same, col < row)
    incl = jnp.logical_and(same, col <= row)
    tri_incl = jnp.where(incl, 1.0, 0.0).astype(BF16)
    eye_c = jnp.where(row == col, 1.0, 0.0).astype(F32)
    r128 = lax.broadcasted_iota(jnp.int32, (PAIR, PAIR), 0)
    c128 = lax.broadcasted_iota(jnp.int32, (PAIR, PAIR), 1)
    eye_p = r128 == c128
    head0 = lax.broadcasted_iota(jnp.int32, (1, PAIR), 1) < HEAD_DIM

    def stack(t):
        return jnp.concatenate([jnp.where(head0, t, 0.0), jnp.where(head0, 0.0, t)], axis=0)

    n_sq = int(math.log2(C)) - 1
    sls = [slice(PAIR * p, PAIR * (p + 1)) for p in range(n_pairs)]

    def each(f, *lists):
        return [f(*a) for a in zip(*lists)]

    def load(ref):
        return [stack(ref[:, sl]) for sl in sls]

    lws = load(lw_ref)
    L = each(lambda x: _dot_exact_rhs_left(tri_incl, x), lws)
    lc = each(lambda l: l[C - 1:C, :] + l[C2 - 1:C2, :], L)
    kn, bs, ks = load(kn_ref), load(b_ref), load(k_ref)
    at = each(lambda n, l, w: _split(-n * jnp.exp(l - w)), kn, L, lws)
    rt_f = each(lambda r, l: r * jnp.exp(l), load(r_ref), L)
    rt = each(_split, rt_f)
    e_nl = each(lambda l: jnp.exp(-l), L)
    bb = each(lambda b, e: _split(b * e), bs, e_nl)
    kb = each(lambda k, e: _split(k * e), ks, e_nl)
    e_lc = each(lambda c, l: jnp.exp(c - l), lc, L)
    bh = each(lambda b, e: _split(b * e), bs, e_lc)
    kh = each(lambda k, e: _split(k * e), ks, e_lc)
    vs = each(_split, load(v_ref))

    g_a = each(lambda a, b, k: _dot3_multi(a, [b, k], _dot_nt), at, bb, kb)
    g_r = each(lambda a, b, k: _dot3_multi(a, [b, k], _dot_nt), rt, bb, kb)
    a_ab = [jnp.where(strict, g[0], 0.0) for g in g_a]
    a_ak = [_split(jnp.where(strict, g[1], 0.0)) for g in g_a]
    a_rb = [_split(jnp.where(incl, g[0], 0.0)) for g in g_r]
    a_rk = [_split(jnp.where(incl, g[1], 0.0)) for g in g_r]

    pws = each(lambda a: _split(_dot3(a, a)), each(_split, a_ab))
    tm = each(lambda a: eye_c + a, a_ab)
    for step in range(n_sq):
        if step < n_sq - 1:
            res = each(lambda s, t: _dot3_multi(s, [_split(t), s]), pws, tm)
            tm = [t + r[0] for t, r in zip(tm, res)]
            pws = [_split(r[1]) for r in res]
        else:
            tm = each(lambda t, s: t + _dot3(s, _split(t)), tm, pws)
    tms = each(_split, tm)
    w1 = each(lambda a, v: _split(_dot3(a, v)), a_ak, vs)
    tx = each(lambda t, a, w: _dot3_multi(t, [a, w]), tms, at, w1)
    at2 = [_split(r[0]) for r in tx]
    v2 = [_split(r[1]) for r in tx]
    ax = each(lambda a, x, v: _dot3_multi(a, [x, v]), a_rb, at2, v2)
    r2 = [r + x[0] for r, x in zip(rt_f, ax)]
    y2 = [x[1] + _dot3(c, v) for x, c, v in zip(ax, a_rk, vs)]
    bx = each(lambda b, x, v: _dot3_multi(b, [x, v], _dot_tn), bh, at2, v2)
    mm = [jnp.where(eye_p, jnp.exp(c), 0.0) + x[0] for c, x in zip(lc, bx)]
    nn = [x[1] + _dot3(k, v, _dot_tn) for x, k, v in zip(bx, kh, vs)]
    rp_out[...] = jnp.concatenate([x[:C, :] + x[C:, :] for x in r2], axis=1)
    yp_out[...] = jnp.concatenate([x[:C, :] + x[C:, :] for x in y2], axis=1)
    m_out[0] = jnp.stack(mm, axis=0)
    n_out[0] = jnp.stack(nn, axis=0)


def _dot_exact_rhs_left(m, x):
    hi = x.astype(BF16)
    r1 = x - hi.astype(F32)
    mid = r1.astype(BF16)
    lo = (r1 - mid.astype(F32)).astype(BF16)
    return _dot(m, hi) + (_dot(m, mid) + _dot(m, lo))


def _wkv_chunks(arrs, C, row0, n_chunks, pairs_per_step):
    d = arrs[0].shape[1]
    n_pairs = d // PAIR
    assert row0 % C == 0 and n_pairs % pairs_per_step == 0
    blk0 = row0 // C
    lanes = pairs_per_step * PAIR
    in_spec = pl.BlockSpec((C, lanes), lambda c, q: (blk0 + c, q))
    out_row = pl.BlockSpec((C, lanes), lambda c, q: (c, q))
    out_mat = pl.BlockSpec((1, pairs_per_step, PAIR, PAIR), lambda c, q: (c, q, 0, 0))
    t = n_chunks * C
    return pl.pallas_call(
        functools.partial(_wkv_chunk_kernel, C, pairs_per_step),
        grid=(n_chunks, n_pairs // pairs_per_step),
        in_specs=[in_spec] * 6,
        out_specs=[out_row, out_row, out_mat, out_mat],
        out_shape=[jax.ShapeDtypeStruct((t, d), F32), jax.ShapeDtypeStruct((t, d), F32),
                   jax.ShapeDtypeStruct((n_chunks, n_pairs, PAIR, PAIR), F32),
                   jax.ShapeDtypeStruct((n_chunks, n_pairs, PAIR, PAIR), F32)],
        compiler_params=_params(("parallel", "parallel"), 32),
        name=f"wkv_chunks_c{C}",
    )(*arrs)


def _wkv_seq_kernel(n_pairs, n_steps, rp_ref, yp_ref, m_ref, n_ref, s0_ref, y_out, s_out, s_scr):
    j = pl.program_id(1)

    @pl.when(j == 0)
    def _():
        s_scr[...] = s0_ref[0]

    ys, new_s = [], []
    for p in range(n_pairs):
        sl = slice(PAIR * p, PAIR * (p + 1))
        ss = _split(s_scr[p])
        ys.append(_dot3(_split(rp_ref[:, sl]), ss) + yp_ref[:, sl])
        new_s.append(_dot3(_split(m_ref[0, p]), ss) + n_ref[0, p])
    y_out[...] = jnp.concatenate(ys, axis=1)
    s_scr[...] = jnp.stack(new_s, axis=0)

    @pl.when(j == n_steps - 1)
    def _():
        s_out[0] = s_scr[...]


def _wkv_seq(rp, yp, m, nn, s0, C, n_seq, n_steps):
    t, d = rp.shape
    n_pairs = d // PAIR
    row = pl.BlockSpec((C, d), lambda s, j: (s * n_steps + j, 0))
    mat = pl.BlockSpec((1, n_pairs, PAIR, PAIR), lambda s, j: (s * n_steps + j, 0, 0, 0))
    st = pl.BlockSpec((1, n_pairs, PAIR, PAIR), lambda s, j: (s, 0, 0, 0))
    return pl.pallas_call(
        functools.partial(_wkv_seq_kernel, n_pairs, n_steps),
        grid=(n_seq, n_steps),
        in_specs=[row, row, mat, mat, st],
        out_specs=[row, st],
        out_shape=[jax.ShapeDtypeStruct((t, d), F32),
                   jax.ShapeDtypeStruct((n_seq, n_pairs, PAIR, PAIR), F32)],
        scratch_shapes=[pltpu.VMEM((n_pairs, PAIR, PAIR), F32)],
        compiler_params=_params(("arbitrary", "arbitrary"), 32),
        name=f"wkv_seq_c{C}",
    )(rp, yp, m, nn, s0)


def _state_to_blockdiag(s):
    b, h, n, _ = s.shape
    st = jnp.swapaxes(s, -1, -2).reshape(b, h // 2, 2, n, n)
    z = jnp.zeros_like(st[:, :, 0])
    top = jnp.concatenate([st[:, :, 0], z], axis=-1)
    bot = jnp.concatenate([z, st[:, :, 1]], axis=-1)
    return jnp.concatenate([top, bot], axis=-2)


def _blockdiag_to_state(bd):
    b, hp, _, _ = bd.shape
    n = HEAD_DIM
    s0 = bd[:, :, :n, :n]
    s1 = bd[:, :, n:, n:]
    st = jnp.stack([s0, s1], axis=2).reshape(b, hp * 2, n, n)
    return jnp.swapaxes(st, -1, -2)


def _route(x1, rw_ref, rb_ref):
    logits = _dot3(_split(rw_ref[...]), _split(x1), _dot_nt)
    mx = jnp.max(logits, axis=0, keepdims=True)
    ex = jnp.exp(logits - mx)
    scores = ex / jnp.sum(ex, axis=0, keepdims=True)
    sel = scores + rb_ref[...]
    rows = [sel[e:e + 1, :] for e in range(N_EXPERTS)]
    srow = [scores[e:e + 1, :] for e in range(N_EXPERTS)]

    def top2(vals):
        m1 = jnp.maximum(jnp.maximum(vals[0], vals[1]), jnp.maximum(vals[2], vals[3]))
        i1 = jnp.where(vals[0] == m1, 0, jnp.where(vals[1] == m1, 1, jnp.where(vals[2] == m1, 2, 3)))
        rest = [jnp.where(i1 == j, -jnp.inf, vals[j]) for j in range(4)]
        m2 = jnp.maximum(jnp.maximum(rest[0], rest[1]), jnp.maximum(rest[2], rest[3]))
        i2 = jnp.where(rest[0] == m2, 0, jnp.where(rest[1] == m2, 1, jnp.where(rest[2] == m2, 2, 3)))
        return m1, i1, m2, i2

    gscore = []
    for gidx in range(N_GROUPS):
        m1, _, m2, _ = top2(rows[4 * gidx:4 * gidx + 4])
        gscore.append(m1 + m2)
    gm = jnp.maximum(jnp.maximum(gscore[0], gscore[1]), jnp.maximum(gscore[2], gscore[3]))
    gi = jnp.where(gscore[0] == gm, 0, jnp.where(gscore[1] == gm, 1, jnp.where(gscore[2] == gm, 2, 3)))

    def pick(rws, j):
        return jnp.where(gi == 0, rws[j], jnp.where(gi == 1, rws[4 + j],
                                                    jnp.where(gi == 2, rws[8 + j], rws[12 + j])))

    in_grp = [pick(rows, j) for j in range(4)]
    sc_grp = [pick(srow, j) for j in range(4)]
    _, i1, _, i2 = top2(in_grp)

    def at(vals, idx):
        return jnp.where(idx == 0, vals[0], jnp.where(idx == 1, vals[1],
                                                      jnp.where(idx == 2, vals[2], vals[3])))

    ga = at(sc_grp, i1)
    gb = at(sc_grp, i2)
    tot = ga + gb
    ga = ga / tot
    gb = gb / tot
    lo = jnp.minimum(i1, i2)
    hi = jnp.maximum(i1, i2)
    g_lo = jnp.where(i1 < i2, ga, gb)
    g_hi = jnp.where(i1 < i2, gb, ga)
    pair = jnp.where(lo == 0, hi - 1, jnp.where(lo == 1, hi + 1, 5))
    cls = gi * 6 + pair
    return cls.astype(jnp.int32), g_lo, g_hi


def _mix_post_kernel(is_rwkv, alpha_dn, *refs):
    if is_rwkv:
        (x_ref, y_ref, r_ref, k_ref, v_ref, g_ref, lg_ref, lb_ref, rk_ref, bd_ref,
         wo_ref, n1g_ref, n1b_ref, rw_ref, rb_ref, x1_out, cls_out) = refs
        y = y_ref[...]
        mu = _head_sum(y, bd_ref) * (1.0 / HEAD_DIM)
        yc = y - mu
        var = _head_sum(yc * yc, bd_ref) * (1.0 / HEAD_DIM)
        yn = yc * lax.rsqrt(var + GN_EPS) * lg_ref[...] + lb_ref[...]
        v = v_ref[...]
        bonus = _head_sum(r_ref[...] * k_ref[...] * rk_ref[...], bd_ref)
        z = (yn + bonus * v) * g_ref[...]
    else:
        (x_ref, o_ref, gt_ref, wo_ref, n1g_ref, n1b_ref, rw_ref, rb_ref,
         x1_out, cls_out) = refs
        z = o_ref[...] * gt_ref[...]
    h = _dot(z.astype(BF16), wo_ref[...])
    x1 = _layer_norm(alpha_dn * x_ref[...] + h, n1g_ref[...], n1b_ref[...])
    cls, g_lo, g_hi = _route(x1, rw_ref, rb_ref)
    tm, d = x1.shape
    cls_out[...] = jnp.broadcast_to(cls, (8, tm))
    pieces = [x.astype(F32) for x in _split3(g_lo) + _split3(g_hi)]
    gs = jnp.concatenate(pieces + [jnp.zeros((16 - len(pieces), tm), F32)], axis=0).astype(BF16)
    rr = lax.broadcasted_iota(jnp.int32, (16, GATE_LANES), 0)
    cc = lax.broadcasted_iota(jnp.int32, (16, GATE_LANES), 1)
    place = jnp.where(jnp.logical_and(rr < 6, cc == jnp.where(rr < 3, 0, 1)), 1.0, 0.0).astype(BF16)
    x1_out[:, :d] = x1
    x1_out[:, d:] = _dot_tn(gs, place)


def _mix_post(is_rwkv, alpha_dn, acts, consts, tm):
    n, d = acts[0].shape
    row = pl.BlockSpec((tm, d), lambda i: (i, 0))
    specs = [row] * len(acts) + [_const_spec(c.shape) for c in consts]
    lane = pl.BlockSpec((8, tm), lambda i: (0, i))
    wide = pl.BlockSpec((tm, d + GATE_LANES), lambda i: (i, 0))
    return pl.pallas_call(
        functools.partial(_mix_post_kernel, is_rwkv, alpha_dn),
        grid=(n // tm,),
        in_specs=specs,
        out_specs=[wide, lane],
        out_shape=[jax.ShapeDtypeStruct((n, d + GATE_LANES), F32), jax.ShapeDtypeStruct((8, n), jnp.int32)],
        compiler_params=_params(("parallel",), 48),
        name="mix_post_rwkv" if is_rwkv else "mix_post_fox",
    )(*acts, *consts)


def _moe_kernel(n_tok, d, d_exp, n_blk, e1_ref, e2_ref, off_ref, nv_ref, order_ref, x_hbm,
                wgu1_ref, wd1_ref, wgu2_ref, wd2_ref, out_hbm, xbuf, obuf, gsem, ssem):
    b = pl.program_id(0)
    slot = lax.rem(b, 2)
    nv_b = nv_ref[b]
    unroll = 8

    def gather_start(blk, sl):
        base = off_ref[blk]

        def body(c, carry):
            for u in range(unroll):
                i = c * unroll + u
                tok = order_ref[jnp.minimum(base + i, n_tok - 1)]
                pltpu.make_async_copy(x_hbm.at[pl.ds(tok, 1)], xbuf.at[sl, pl.ds(i, 1)], gsem.at[sl]).start()
            return carry

        lax.fori_loop(0, E_BLOCK // unroll, body, 0)

    def gather_wait(sl):
        pltpu.make_async_copy(x_hbm.at[pl.ds(0, E_BLOCK)], xbuf.at[sl], gsem.at[sl]).wait()

    def scatter_start(blk, sl):
        base = off_ref[blk]

        def body(i, carry):
            tok = order_ref[base + i]
            pltpu.make_async_copy(obuf.at[sl, pl.ds(i, 1)], out_hbm.at[pl.ds(tok, 1)], ssem.at[sl]).start()
            return carry

        lax.fori_loop(0, nv_ref[blk], body, 0)

    def scatter_wait(blk, sl):
        nv = nv_ref[blk]

        @pl.when(nv == E_BLOCK)
        def _():
            pltpu.make_async_copy(obuf.at[sl], out_hbm.at[pl.ds(0, E_BLOCK)], ssem.at[sl]).wait()

        @pl.when(nv < E_BLOCK)
        def _():
            def body(i, carry):
                pltpu.make_async_copy(obuf.at[sl, pl.ds(0, 1)], out_hbm.at[pl.ds(0, 1)], ssem.at[sl]).wait()
                return carry

            lax.fori_loop(0, nv, body, 0)

    @pl.when(jnp.logical_and(b == 0, nv_b > 0))
    def _():
        gather_start(0, 0)

    nxt = jnp.minimum(b + 1, n_blk - 1)

    @pl.when(jnp.logical_and(b + 1 < n_blk, nv_ref[nxt] > 0))
    def _():
        gather_start(nxt, 1 - slot)

    @pl.when(nv_b > 0)
    def _():
        gather_wait(slot)
        rows = xbuf[slot]
        xb = rows[:, :d].astype(BF16)

        def expert(wgu_ref, wd_ref):
            gu = _dot(xb, wgu_ref[0])
            hmid = jax.nn.silu(gu[:, :d_exp]) * gu[:, d_exp:]
            return _dot(hmid.astype(BF16), wd_ref[0])

        y1 = expert(wgu1_ref, wd1_ref)
        y2 = expert(wgu2_ref, wd2_ref)
        obuf[slot] = y1 * rows[:, d:d + 1] + y2 * rows[:, d + 1:d + 2]

    prev = jnp.maximum(b - 1, 0)

    @pl.when(jnp.logical_and(b > 0, nv_ref[prev] > 0))
    def _():
        scatter_wait(prev, 1 - slot)

    @pl.when(nv_b > 0)
    def _():
        scatter_start(b, slot)

    @pl.when(jnp.logical_and(b == n_blk - 1, nv_b > 0))
    def _():
        scatter_wait(b, slot)


def _moe(xg, cls, w_gu, w_down):
    n = xg.shape[0]
    d, d_exp = w_down.shape[2], w_down.shape[1]
    n_blk = (n + N_CLASSES * (E_BLOCK - 1) + E_BLOCK - 1) // E_BLOCK
    counts = jnp.bincount(cls, length=N_CLASSES).astype(jnp.int32)
    padded = (counts + E_BLOCK - 1) // E_BLOCK * E_BLOCK
    pad_end = jnp.cumsum(padded)
    pad_start = pad_end - padded
    seg_start = jnp.cumsum(counts) - counts
    order = jnp.argsort(cls).astype(jnp.int32)
    blk_start = jnp.arange(n_blk, dtype=jnp.int32) * E_BLOCK
    blk_cls = jnp.minimum(jnp.searchsorted(pad_end, blk_start, side='right'), N_CLASSES - 1).astype(jnp.int32)
    rank0 = blk_start - pad_start[blk_cls]
    blk_nv = jnp.clip(counts[blk_cls] - rank0, 0, E_BLOCK).astype(jnp.int32)
    blk_off = (seg_start[blk_cls] + rank0).astype(jnp.int32)
    n_used = jnp.sum((blk_nv > 0).astype(jnp.int32))
    blk_cls = jnp.where(blk_nv > 0, blk_cls, blk_cls[jnp.maximum(n_used - 1, 0)])
    pair_lo = jnp.asarray(_PAIR_LO, jnp.int32)
    pair_hi = jnp.asarray(_PAIR_HI, jnp.int32)
    blk_e1 = (blk_cls // 6) * EXPERTS_PER_GROUP + pair_lo[blk_cls % 6]
    blk_e2 = (blk_cls // 6) * EXPERTS_PER_GROUP + pair_hi[blk_cls % 6]

    gu1 = pl.BlockSpec((1, d, 2 * d_exp), lambda b, e1, e2, of, nv, od: (e1[b], 0, 0))
    dn1 = pl.BlockSpec((1, d_exp, d), lambda b, e1, e2, of, nv, od: (e1[b], 0, 0))
    gu2 = pl.BlockSpec((1, d, 2 * d_exp), lambda b, e1, e2, of, nv, od: (e2[b], 0, 0))
    dn2 = pl.BlockSpec((1, d_exp, d), lambda b, e1, e2, of, nv, od: (e2[b], 0, 0))
    grid_spec = pltpu.PrefetchScalarGridSpec(
        num_scalar_prefetch=5,
        grid=(n_blk,),
        in_specs=[pl.BlockSpec(memory_space=pl.ANY), gu1, dn1, gu2, dn2],
        out_specs=pl.BlockSpec(memory_space=pl.ANY),
        scratch_shapes=[pltpu.VMEM((2, E_BLOCK, d + GATE_LANES), F32), pltpu.VMEM((2, E_BLOCK, d), F32),
                        pltpu.SemaphoreType.DMA((2,)), pltpu.SemaphoreType.DMA((2,))],
    )
    return pl.pallas_call(
        functools.partial(_moe_kernel, n, d, d_exp, n_blk),
        grid_spec=grid_spec,
        out_shape=jax.ShapeDtypeStruct((n, d), F32),
        compiler_params=_params(("arbitrary",), 40),
        name="moe_experts",
    )(blk_e1, blk_e2, blk_off, blk_nv, order, xg, w_gu, w_down, w_gu, w_down)


def _ln2_kernel(alpha_dn, with_kv, *refs):
    if with_kv:
        (x1_ref, m_ref, g_ref, b_ref, wk_ref, wv_ref, wf_ref, bf_ref, tri_ref,
         x2_out, k_out, v_out, kb_out, vb_out, lf_out, cum_out, carry) = refs
    else:
        x1_ref, m_ref, g_ref, b_ref, x2_out = refs
    x2 = _layer_norm(alpha_dn * x1_ref[...] + m_ref[...], g_ref[...], b_ref[...])
    x2_out[...] = x2
    if with_kv:
        xb = x2.astype(BF16)
        k = _dot(xb, wk_ref[...])
        v = _dot(xb, wv_ref[...])
        k_out[...] = k
        v_out[...] = v
        kb_out[...] = k.astype(BF16)
        vb_out[...] = v.astype(BF16)
        z = _dot3(_split(wf_ref[...]), _split(x2), _dot_nt) + bf_ref[...]
        lf = jnp.minimum(z, 0.0) - jnp.log1p(jnp.exp(-jnp.abs(z)))
        lf_out[...] = lf

        @pl.when(pl.program_id(0) == 0)
        def _():
            carry[...] = jnp.zeros_like(carry)

        cum = _dot_exact_rhs(lf, tri_ref[...]) + carry[...]
        cum_out[...] = cum
        carry[...] = cum[:, -1:]


def _ln2(alpha_dn, x1, m, g, b, kv, tm):
    n, d = m.shape
    row = pl.BlockSpec((tm, d), lambda i: (i, 0))
    with_kv = kv is not None
    ins = [x1, m, g, b]
    specs = [row, row, _const_spec(g.shape), _const_spec(b.shape)]
    outs = [jax.ShapeDtypeStruct((n, d), F32)]
    ospecs = [row]
    scratch = []
    if with_kv:
        nh = kv['wf_t'].shape[0]
        tri = (jnp.arange(tm)[:, None] <= jnp.arange(tm)[None, :]).astype(BF16)
        extra = [kv['wk'], kv['wv'], kv['wf_t'], kv['bf'], tri]
        ins += extra
        specs += [_const_spec(e.shape) for e in extra]
        lane = pl.BlockSpec((nh, tm), lambda i: (0, i))
        outs += [jax.ShapeDtypeStruct((n, d), F32), jax.ShapeDtypeStruct((n, d), F32),
                 jax.ShapeDtypeStruct((n, d), BF16), jax.ShapeDtypeStruct((n, d), BF16),
                 jax.ShapeDtypeStruct((nh, n), F32), jax.ShapeDtypeStruct((nh, n), F32)]
        ospecs += [row, row, row, row, lane, lane]
        scratch = [pltpu.VMEM((nh, 1), F32)]
    res = pl.pallas_call(
        functools.partial(_ln2_kernel, alpha_dn, with_kv),
        grid=(n // tm,),
        in_specs=specs,
        out_specs=ospecs,
        out_shape=outs,
        scratch_shapes=scratch,
        compiler_params=_params(("arbitrary",), 48),
        name="ln2_kv" if with_kv else "ln2",
    )(*ins)
    return res


def _fox_qg_kernel(x_ref, wq_ref, wg_ref, q_out, gate_out):
    xb = x_ref[...].astype(BF16)
    q_out[...] = (_dot(xb, wq_ref[...]) * (LOG2E * HEAD_DIM ** -0.5)).astype(BF16)
    gate_out[...] = jax.nn.sigmoid(_dot(xb, wg_ref[...]))


def _fox_qg(x, wq, wg, tm):
    n, d = x.shape
    row = pl.BlockSpec((tm, d), lambda i: (i, 0))
    return pl.pallas_call(
        _fox_qg_kernel,
        grid=(n // tm,),
        in_specs=[row, _const_spec(wq.shape), _const_spec(wg.shape)],
        out_specs=[row, row],
        out_shape=[jax.ShapeDtypeStruct((n, d), BF16), jax.ShapeDtypeStruct((n, d), F32)],
        compiler_params=_params(("parallel",), 40),
        name="fox_qg",
    )(x, wq, wg)


V_ROWS = 80


def _fox_prompt_kernel(tq, hps, nh, base_ref, q_ref, k_ref, vt_ref, o_out, s_scr, p_scr):
    g = pl.program_id(0)
    i = pl.program_id(1)
    rr = lax.broadcasted_iota(jnp.int32, (tq, tq), 0)
    cc = lax.broadcasted_iota(jnp.int32, (tq, tq), 1)
    causal = rr <= cc

    def scores(j, h):
        k0 = pl.multiple_of(j * tq, tq)
        lanes = slice(PAIR * h, PAIR * (h + 1))
        return _dot_nt(k_ref[pl.ds(k0, tq), lanes], q_ref[:, lanes])

    def base_gap(j, h):
        head = g * hps + h
        return base_ref[i * nh + head] - base_ref[j * nh + head]

    def softmax(s, m, gap):
        m_new = jnp.maximum(m, jnp.max(s, axis=0, keepdims=True) + gap)
        return jnp.exp2(s - (m_new - gap)).astype(BF16), jnp.exp2(m - m_new), m_new

    def accum(acc, corr, p, j, h):
        k0 = pl.multiple_of(j * tq, tq)
        return acc * corr + _dot(vt_ref[h, :, pl.ds(k0, tq)], p)

    for h in range(hps):
        s_scr[h] = scores(0, h)
        p_scr[h] = jnp.zeros((tq, tq), BF16)

    def body(j, carry):
        new = []
        for h in range(hps):
            m, acc, corr_prev = carry[h]
            acc = accum(acc, corr_prev, p_scr[h], jnp.maximum(j - 1, 0), h)
            p, corr, m = softmax(s_scr[h], m, base_gap(j, h))
            p_scr[h] = p
            s_scr[h] = scores(j + 1, h)
            new.append((m, acc, corr))
        return tuple(new)

    init = tuple((jnp.full((1, tq), NEG_INF, F32), jnp.zeros((V_ROWS, tq), F32), jnp.ones((1, tq), F32))
                 for _ in range(hps))
    carry = lax.fori_loop(0, i, body, init)
    outs = []
    for h in range(hps):
        m, acc, corr_prev = carry[h]
        acc = accum(acc, corr_prev, p_scr[h], jnp.maximum(i - 1, 0), h)
        p, corr, m = softmax(jnp.where(causal, s_scr[h], NEG_INF), m, 0.0)
        acc = accum(acc, corr, p, i, h)
        outs.append(acc[:HEAD_DIM, :] / acc[HEAD_DIM:HEAD_DIM + 1, :])
    o_out[...] = jnp.concatenate(outs, axis=0)


def _fox_prompt(base, q_aug, k_aug, vt_aug, tq, hps):
    t = q_aug.shape[0]
    nh = vt_aug.shape[0]
    return pl.pallas_call(
        functools.partial(_fox_prompt_kernel, tq, hps, nh),
        grid=(nh // hps, t // tq),
        in_specs=[pl.BlockSpec(memory_space=pltpu.SMEM),
                  pl.BlockSpec((tq, hps * PAIR), lambda g, i: (i, g)),
                  pl.BlockSpec((t, hps * PAIR), lambda g, i: (0, g), pipeline_mode=pl.Buffered(1)),
                  pl.BlockSpec((hps, V_ROWS, t), lambda g, i: (g, 0, 0), pipeline_mode=pl.Buffered(1))],
        out_specs=pl.BlockSpec((hps * HEAD_DIM, tq), lambda g, i: (g, i)),
        out_shape=jax.ShapeDtypeStruct((nh * HEAD_DIM, t), F32),
        scratch_shapes=[pltpu.VMEM((hps, tq, tq), F32), pltpu.VMEM((hps, tq, tq), BF16)],
        compiler_params=_params(("parallel", "parallel"), 48),
        name="fox_prompt_attn",
    )(base, q_aug, k_aug, vt_aug)


def _augment_q(qb, cum2):
    t, d = qb.shape
    nh = d // HEAD_DIM
    hi, mid, lo = _split3(cum2)
    one = jnp.ones((t, nh), BF16)
    extra = jnp.stack([one, one, one, hi, mid, lo], axis=-1)
    pad = jnp.zeros((t, nh, HEAD_DIM - 6), BF16)
    return jnp.concatenate([qb.reshape(t, nh, HEAD_DIM), extra, pad], axis=-1).reshape(t, nh * PAIR)


def _augment_k(kb, cum2):
    t, d = kb.shape
    nh = d // HEAD_DIM
    hi, mid, lo = _split3(-cum2)
    one = jnp.ones((t, nh), BF16)
    extra = jnp.stack([hi, mid, lo, one, one, one], axis=-1)
    pad = jnp.zeros((t, nh, HEAD_DIM - 6), BF16)
    return jnp.concatenate([kb.reshape(t, nh, HEAD_DIM), extra, pad], axis=-1).reshape(t, nh * PAIR)


def _augment_vt(vb):
    t, d = vb.shape
    nh = d // HEAD_DIM
    vt = vb.reshape(t, nh, HEAD_DIM).transpose(1, 2, 0)
    return jnp.concatenate([vt, jnp.ones((nh, 1, t), BF16),
                            jnp.zeros((nh, V_ROWS - HEAD_DIM - 1, t), BF16)], axis=1)


def _fox_sample_kernel(n_pairs, q_ref, kn_ref, vn_ref, lfn_ref, kc_ref, vc_ref, lfc_ref,
                       triu_ref, o_out):
    t = q_ref.shape[0]
    past = kc_ref.shape[1]
    lfc = lfc_ref[0]
    lfn = lfn_ref[0]
    nh = lfc.shape[1]
    lfc_s = _split3(lfc)
    cum_c_r = _dot_tn3(lfc_s, triu_ref[...])
    ones_row = jnp.ones((1, past), BF16)
    tot_r = _dot_m3(ones_row, lfc_s)
    tot_c = cum_c_r[:, past - 1:past]
    rr = lax.broadcasted_iota(jnp.int32, (t, t), 0)
    cc = lax.broadcasted_iota(jnp.int32, (t, t), 1)
    causal = rr >= cc
    tril = jnp.where(causal, 1.0, 0.0).astype(BF16)
    lfn_s = _split3(lfn)
    cum_n_c = (_dot_m3(tril, lfn_s) + tot_r) * LOG2E
    cum_n_r = (_dot_tn3(lfn_s, jnp.where(rr <= cc, 1.0, 0.0).astype(BF16)) + tot_c) * LOG2E
    cum_c_r = cum_c_r * LOG2E
    lane = lax.broadcasted_iota(jnp.int32, (1, PAIR), 1)
    head0 = lane < HEAD_DIM
    hl = lax.broadcasted_iota(jnp.int32, (1, nh), 1)
    for p in range(n_pairs):
        sl = slice(PAIR * p, PAIR * (p + 1))
        q = q_ref[:, sl]
        zero = jnp.zeros_like(q)
        kc = kc_ref[0, :, sl].astype(BF16)
        vc = vc_ref[0, :, sl].astype(BF16)
        kn = kn_ref[:, sl].astype(BF16)
        vn = vn_ref[:, sl].astype(BF16)
        outs = []
        for h in range(2):
            hh = 2 * p + h
            qh = jnp.where(head0, q, zero) if h == 0 else jnp.where(head0, zero, q)
            cq = jnp.sum(jnp.where(hl == hh, cum_n_c, 0.0), axis=1, keepdims=True)
            s_c = _dot_nt(qh, kc) + cq - cum_c_r[hh:hh + 1, :]
            s_n = _dot_nt(qh, kn) + cq - cum_n_r[hh:hh + 1, :]
            s_n = jnp.where(causal, s_n, NEG_INF)
            m = jnp.maximum(jnp.max(s_c, axis=1, keepdims=True), jnp.max(s_n, axis=1, keepdims=True))
            p_c = jnp.exp2(s_c - m)
            p_n = jnp.exp2(s_n - m)
            l = jnp.sum(p_c, axis=1, keepdims=True) + jnp.sum(p_n, axis=1, keepdims=True)
            acc = _dot(p_c.astype(BF16), vc) + _dot(p_n.astype(BF16), vn)
            outs.append(acc / l)
        o_out[:, sl] = jnp.where(head0, outs[0], outs[1])


def _split3(x):
    hi = x.astype(BF16)
    r1 = x - hi.astype(F32)
    mid = r1.astype(BF16)
    lo = (r1 - mid.astype(F32)).astype(BF16)
    return hi, mid, lo


def _dot_tn3(xs, m):
    return _dot_tn(xs[0], m) + (_dot_tn(xs[1], m) + _dot_tn(xs[2], m))


def _dot_m3(m, xs):
    return _dot(m, xs[0]) + (_dot(m, xs[1]) + _dot(m, xs[2]))


def _fox_sample(qb, k, v, lf_new, cache_k, cache_v, cache_lf, row0, n_stream, t):
    d = qb.shape[1]
    n_pairs = d // PAIR
    past = cache_k.shape[1]
    nh = cache_lf.shape[2]
    blk0 = row0 // t
    triu = (jnp.arange(past)[:, None] <= jnp.arange(past)[None, :]).astype(BF16)
    row = pl.BlockSpec((t, d), lambda b: (blk0 + b, 0))
    return pl.pallas_call(
        functools.partial(_fox_sample_kernel, n_pairs),
        grid=(n_stream,),
        in_specs=[row, row, row,
                  pl.BlockSpec((1, t, nh), lambda b: (b, 0, 0)),
                  pl.BlockSpec((1, past, d), lambda b: (b, 0, 0)),
                  pl.BlockSpec((1, past, d), lambda b: (b, 0, 0)),
                  pl.BlockSpec((1, past, nh), lambda b: (b, 0, 0)),
                  _const_spec(triu.shape)],
        out_specs=pl.BlockSpec((t, d), lambda b: (b, 0)),
        out_shape=jax.ShapeDtypeStruct((n_stream * t, d), F32),
        compiler_params=_params(("parallel",), 48),
        name="fox_sample_attn",
    )(qb, k, v, lf_new, cache_k, cache_v, cache_lf, triu)


def kernel(x_prompt, x_sample, state_shift, state_wkv, cache_k, cache_v, cache_logf, ln1_g, ln1_b, ln2_g, ln2_b, a_mix, a_w_rkv, a_w0, a_w1, a_w2, a_a0, a_a1, a_a2, a_v0, a_v1, a_v2, a_g1, a_g2, a_k_k, a_k_a, a_r_k, a_lnx_g, a_lnx_b, a_w_o, kv_w, kv_bf, b_w_qg, b_w_o, router_w, router_b, moe_w_gu, moe_w_down):
    nb, seq, d = x_prompt.shape
    db, dt, _ = x_sample.shape
    assert nb == 1
    n_heads = d // HEAD_DIM
    depth = ln1_g.shape[0]
    n_a = a_mix.shape[0]
    past = cache_k.shape[1]
    t_p = nb * seq
    t_s = db * dt
    n = t_p + t_s
    tm = _row_tile(n)
    c_p = 64
    c_s = dt
    tq_attn = 512 if t_p % 512 == 0 else 256
    assert seq % c_p == 0 and t_p % c_s == 0 and (c_s & (c_s - 1)) == 0
    alpha_dn = (2 * depth) ** 0.25
    row2 = lambda a: a.reshape(1, -1)

    bd = (jnp.arange(256)[:, None] // HEAD_DIM == jnp.arange(256)[None, :] // HEAD_DIM).astype(BF16)
    rw_t = router_w.T
    rb_c = router_b.reshape(-1, 1)

    x = jnp.concatenate([x_prompt.reshape(t_p, d), x_sample.reshape(t_s, d)], axis=0)
    new_shift_p, new_shift_s, new_wkv_p, new_wkv_s = [], [], [], []
    v_first = None
    kv = None
    for l in range(depth):
        if l < n_a:
            xs = x[t_p:].reshape(db, dt, d)
            new_shift_p.append(x[t_p - 1:t_p].reshape(nb, d))
            new_shift_s.append(xs[:, -1])
            xp = jnp.concatenate([
                jnp.zeros((1, d), F32), x[:t_p - 1],
                jnp.concatenate([state_shift[l][:, None, :], xs[:, :-1]], axis=1).reshape(t_s, d)], axis=0)
            w = dict(mix=jnp.concatenate([a_mix[l], jnp.zeros((2, d), F32)], axis=0),
                     wr=a_w_rkv[l, 0].astype(BF16), wk=a_w_rkv[l, 1].astype(BF16), wv=a_w_rkv[l, 2].astype(BF16),
                     w1=a_w1[l].astype(BF16), w2=a_w2[l].astype(BF16), w0=row2(a_w0[l]),
                     a1=a_a1[l].astype(BF16), a2=a_a2[l].astype(BF16), a0=row2(a_a0[l]),
                     g1=a_g1[l].astype(BF16), g2=a_g2[l].astype(BF16),
                     k_k=row2(a_k_k[l]), k_a=row2(a_k_a[l]), bd=bd)
            if l > 0:
                w.update(v1=a_v1[l - 1].astype(BF16), v2=a_v2[l - 1].astype(BF16), v0=row2(a_v0[l - 1]))
            r, lw, k, v, kn, b, g = _rwkv_pre(x, xp, v_first if l > 0 else None, w, tm)
            if l == 0:
                v_first = v
            scan_in = (r, lw, k, v, kn, b)
            rp, yp, mm, nn = _wkv_chunks(scan_in, c_p, 0, t_p // c_p, n_heads // 2)
            s0 = jnp.zeros((1, n_heads // 2, PAIR, PAIR), F32)
            y_p, sf_p = _wkv_seq(rp, yp, mm, nn, s0, c_p, 1, t_p // c_p)
            rp, yp, mm, nn = _wkv_chunks(scan_in, c_s, t_p, db, n_heads // 2)
            y_s, sf_s = _wkv_seq(rp, yp, mm, nn, _state_to_blockdiag(state_wkv[l]), c_s, db, 1)
            new_wkv_p.append(_blockdiag_to_state(sf_p))
            new_wkv_s.append(_blockdiag_to_state(sf_s))
            y = jnp.concatenate([y_p, y_s], axis=0)
            consts = [row2(a_lnx_g[l]), row2(a_lnx_b[l]), row2(a_r_k[l]), bd, a_w_o[l].astype(BF16),
                      row2(ln1_g[l]), row2(ln1_b[l]), rw_t, rb_c]
            x1, cls = _mix_post(True, alpha_dn, [x, y, r, k, v, g], consts, tm)
        else:
            lb = l - n_a
            qb, gate = _fox_qg(x, b_w_qg[lb][:, :d].astype(BF16), b_w_qg[lb][:, d:].astype(BF16), tm)
            o_p = _fox_prompt(kv['base'], _augment_q(qb[:t_p], kv['cum_rem']), kv['k_aug'], kv['vt_aug'],
                              tq_attn, 4).T
            o_s = _fox_sample(qb, kv['k'], kv['v'], kv['lf_s'], cache_k.reshape(db, past, d),
                              cache_v.reshape(db, past, d), cache_logf, t_p, db, dt)
            o = jnp.concatenate([o_p, o_s], axis=0)
            consts = [b_w_o[lb].astype(BF16), row2(ln1_g[l]), row2(ln1_b[l]), rw_t, rb_c]
            x1, cls = _mix_post(False, alpha_dn, [x, o, gate], consts, tm)
        m = _moe(x1, cls[0], moe_w_gu[l].astype(BF16), moe_w_down[l].astype(BF16))
        if l == n_a - 1:
            kvw = dict(wk=kv_w[:, :d].astype(BF16), wv=kv_w[:, d:2 * d].astype(BF16),
                       wf_t=kv_w[:, 2 * d:].T, bf=kv_bf.reshape(-1, 1))
            x, k_all, v_all, kb_all, vb_all, lf_r, cum_r = _ln2(
                alpha_dn, x1, m, row2(ln2_g[l]), row2(ln2_b[l]), kvw, tm)
            cum2 = (cum_r[:, :t_p].T * LOG2E).reshape(t_p // tq_attn, tq_attn, n_heads)
            base = cum2[:, :1, :]
            cum_rem = (cum2 - base).reshape(t_p, n_heads)
            kv = dict(k=k_all, v=v_all, base=base.reshape(-1), cum_rem=cum_rem,
                      k_aug=_augment_k(kb_all[:t_p], cum_rem),
                      vt_aug=_augment_vt(vb_all[:t_p]), lf_p=lf_r[:, :t_p].T,
                      lf_s=lf_r[:, t_p:].T.reshape(db, dt, n_heads))
        else:
            x = _ln2(alpha_dn, x1, m, row2(ln2_g[l]), row2(ln2_b[l]), None, tm)[0]

    y_prompt = x[:t_p].reshape(nb, seq, d)
    y_sample = x[t_p:].reshape(db, dt, d)
    p_k = kv['k'][:t_p].reshape(nb, seq, n_heads, HEAD_DIM)
    p_v = kv['v'][:t_p].reshape(nb, seq, n_heads, HEAD_DIM)
    s_k = kv['k'][t_p:].reshape(db, dt, n_heads, HEAD_DIM)
    s_v = kv['v'][t_p:].reshape(db, dt, n_heads, HEAD_DIM)
    return (y_prompt, y_sample, jnp.stack(new_shift_p), jnp.stack(new_wkv_p), p_k, p_v,
            kv['lf_p'].reshape(nb, seq, n_heads), jnp.stack(new_shift_s), jnp.stack(new_wkv_s),
            s_k, s_v, kv['lf_s'])
```

```python
import functools
import math

import jax
import jax.numpy as jnp
import numpy as np
from jax import lax
from jax.experimental import pallas as pl
from jax.experimental.pallas import tpu as pltpu

F32 = jnp.float32
BF16 = jnp.bfloat16

HEAD_DIM = 64
PAIR = 2 * HEAD_DIM
N_EXPERTS = 16
N_GROUPS = 4
EXPERTS_PER_GROUP = 4
N_CLASSES = 24
E_BLOCK = 128
GATE_LANES = 128
LN_EPS = 1e-5
GN_EPS = 64e-5
NEG_INF = -1e30
LOG2E = 1.4426950408889634
MIB = 2 ** 20

_PAIR_LO = (0, 0, 0, 1, 1, 2)
_PAIR_HI = (1, 2, 3, 2, 3, 3)


def _params(sem, vmem_mib):
    return pltpu.CompilerParams(dimension_semantics=sem, vmem_limit_bytes=vmem_mib * MIB)


def _dot(a, b):
    return jnp.dot(a, b, preferred_element_type=F32)


def _dot_nt(a, b):
    return lax.dot_general(a, b, (((1,), (1,)), ((), ())), preferred_element_type=F32)


def _dot_tn(a, b):
    return lax.dot_general(a, b, (((0,), (0,)), ((), ())), preferred_element_type=F32)


def _split(x):
    hi = x.astype(BF16)
    lo = (x - hi.astype(F32)).astype(BF16)
    return hi, lo


def _dot3(a, b, dot=_dot):
    return _dot3_multi(a, [b], dot)[0]


def _dot3_multi(a, bs, dot=_dot):
    ah, al = a
    ax = 0 if dot is _dot_nt else 1
    ns = [b[0].shape[ax] for b in bs]
    if any(n % 128 for n in ns):
        return [dot(ah, bh) + (dot(ah, bl) + dot(al, bh)) for bh, bl in bs]
    kax = 0 if dot is _dot_tn else 1
    if ah.shape[kax] == 128:
        lhs = jnp.concatenate([ah, al], axis=kax)
        if dot is _dot_nt:
            rhs = jnp.concatenate([jnp.concatenate([x, y], axis=1) for bh, bl in bs
                                   for x, y in ((bh, bh), (bl, jnp.zeros_like(bl)))], axis=0)
        else:
            top = jnp.concatenate([x for b in bs for x in b], axis=1)
            bot = jnp.concatenate([x for bh, bl in bs for x in (bh, jnp.zeros_like(bl))], axis=1)
            rhs = jnp.concatenate([top, bot], axis=0)
        r = dot(lhs, rhs)
        outs, o = [], 0
        for n in ns:
            outs.append(r[:, o:o + n] + r[:, o + n:o + 2 * n])
            o += 2 * n
        return outs
    r1 = dot(ah, jnp.concatenate([x for b in bs for x in b], axis=ax))
    r2 = dot(al, jnp.concatenate([b[0] for b in bs], axis=ax)) if len(bs) > 1 else dot(al, bs[0][0])
    outs, o1, o2 = [], 0, 0
    for n in ns:
        outs.append((r1[:, o1:o1 + n] + r1[:, o1 + n:o1 + 2 * n]) + r2[:, o2:o2 + n])
        o1 += 2 * n
        o2 += n
    return outs


def _dot_exact_rhs(x, m, dot=_dot):
    hi = x.astype(BF16)
    r1 = x - hi.astype(F32)
    mid = r1.astype(BF16)
    lo = (r1 - mid.astype(F32)).astype(BF16)
    return dot(hi, m) + (dot(mid, m) + dot(lo, m))


def _head_sum(x, bd_ref):
    bd = bd_ref[...]
    parts = []
    for j in range(x.shape[1] // 256):
        parts.append(_dot_exact_rhs(x[:, 256 * j:256 * (j + 1)], bd))
    return jnp.concatenate(parts, axis=1)


def _layer_norm(z, g, b):
    mu = jnp.mean(z, axis=-1, keepdims=True)
    zc = z - mu
    var = jnp.mean(zc * zc, axis=-1, keepdims=True)
    return zc * lax.rsqrt(var + LN_EPS) * g + b


def _row_tile(n):
    for t in (256, 128, 64, 32, 16, 8):
        if n % t == 0:
            return t
    raise ValueError(n)


def _const_spec(shape):
    nd = len(shape)
    return pl.BlockSpec(shape, lambda *_: (0,) * nd)


def _rwkv_pre_kernel(has_vres, *refs):
    if has_vres:
        (x_ref, xp_ref, vf_ref, mix_ref, wr_ref, wk_ref, wv_ref, w1_ref, w2_ref, w0_ref,
         a1_ref, a2_ref, a0_ref, g1_ref, g2_ref, v1_ref, v2_ref, v0_ref, kk_ref, ka_ref, bd_ref,
         r_out, lw_out, k_out, v_out, kn_out, b_out, g_out) = refs
    else:
        (x_ref, xp_ref, mix_ref, wr_ref, wk_ref, wv_ref, w1_ref, w2_ref, w0_ref,
         a1_ref, a2_ref, a0_ref, g1_ref, g2_ref, kk_ref, ka_ref, bd_ref,
         r_out, lw_out, k_out, v_out, kn_out, b_out, g_out) = refs
    x = x_ref[...]
    xx = xp_ref[...] - x

    def mixed(i):
        return (x + xx * mix_ref[i:i + 1, :]).astype(BF16)

    xr, xw, xk, xv, xa, xg = (mixed(i) for i in range(6))
    r = _dot(xr, wr_ref[...])
    k = _dot(xk, wk_ref[...])
    v = _dot(xv, wv_ref[...])
    zw = w0_ref[...] + _dot(jnp.tanh(_dot(xw, w1_ref[...])).astype(BF16), w2_ref[...])
    lw = (-math.exp(-0.5)) * jax.nn.sigmoid(zw)
    alpha = jax.nn.sigmoid(a0_ref[...] + _dot(_dot(xa, a1_ref[...]).astype(BF16), a2_ref[...]))
    g = _dot(jax.nn.sigmoid(_dot(xg, g1_ref[...])).astype(BF16), g2_ref[...])
    if has_vres:
        gate_v = jax.nn.sigmoid(v0_ref[...] + _dot(_dot(xv, v1_ref[...]).astype(BF16), v2_ref[...]))
        v = v + (vf_ref[...] - v) * gate_v
    kkr = k * kk_ref[...]
    norm = jnp.sqrt(_head_sum(kkr * kkr, bd_ref))
    kn = kkr / jnp.maximum(norm, 1e-12)
    k2 = k * (1.0 + (alpha - 1.0) * ka_ref[...])
    r_out[...] = r
    lw_out[...] = lw
    k_out[...] = k2
    v_out[...] = v
    kn_out[...] = kn
    b_out[...] = kn * alpha
    g_out[...] = g


def _rwkv_pre(x, xp, vfirst, w, tm):
    n, d = x.shape
    has_vres = vfirst is not None
    row = pl.BlockSpec((tm, d), lambda i: (i, 0))
    ins = [x, xp] + ([vfirst] if has_vres else [])
    specs = [row, row] + ([row] if has_vres else [])
    names = ['mix', 'wr', 'wk', 'wv', 'w1', 'w2', 'w0', 'a1', 'a2', 'a0', 'g1', 'g2']
    if has_vres:
        names += ['v1', 'v2', 'v0']
    names += ['k_k', 'k_a', 'bd']
    for nm in names:
        ins.append(w[nm])
        specs.append(_const_spec(w[nm].shape))
    out = jax.ShapeDtypeStruct((n, d), F32)
    return pl.pallas_call(
        functools.partial(_rwkv_pre_kernel, has_vres),
        grid=(n // tm,),
        in_specs=specs,
        out_specs=[row] * 7,
        out_shape=[out] * 7,
        compiler_params=_params(("parallel",), 56),
        name="rwkv_pre",
    )(*ins)


def _wkv_chunk_kernel(C, n_pairs, r_ref, lw_ref, k_ref, v_ref, kn_ref, b_ref,
                      rp_out, yp_out, m_out, n_out):
    C2 = 2 * C
    row = lax.broadcasted_iota(jnp.int32, (C2, C2), 0)
    col = lax.broadcasted_iota(jnp.int32, (C2, C2), 1)
    same = (row >= C) == (col >= C)
    strict = jnp.logical_and(same, col < row)
    incl = jnp.logical_and(same, col <= row)
    tri_incl = jnp.where(incl, 1.0, 0.0).astype(BF16)
    eye_c = jnp.where(row == col, 1.0, 0.0).astype(F32)
    r128 = lax.broadcasted_iota(jnp.int32, (PAIR, PAIR), 0)
    c128 = lax.broadcasted_iota(jnp.int32, (PAIR, PAIR), 1)
    eye_p = r128 == c128
    head0 = lax.broadcasted_iota(jnp.int32, (1, PAIR), 1) < HEAD_DIM

    def stack(t):
        return jnp.concatenate([jnp.where(head0, t, 0.0), jnp.where(head0, 0.0, t)], axis=0)

    n_sq = int(math.log2(C)) - 1
    sls = [slice(PAIR * p, PAIR * (p + 1)) for p in range(n_pairs)]

    def each(f, *lists):
        return [f(*a) for a in zip(*lists)]

    def load(ref):
        return [stack(ref[:, sl]) for sl in sls]

    lws = load(lw_ref)
    L = each(lambda x: _dot_exact_rhs_left(tri_incl, x), lws)
    lc = each(lambda l: l[C - 1:C, :] + l[C2 - 1:C2, :], L)
    kn, bs, ks = load(kn_ref), load(b_ref), load(k_ref)
    at = each(lambda n, l, w: _split(-n * jnp.exp(l - w)), kn, L, lws)
    rt_f = each(lambda r, l: r * jnp.exp(l), load(r_ref), L)
    rt = each(_split, rt_f)
    e_nl = each(lambda l: jnp.exp(-l), L)
    bb = each(lambda b, e: _split(b * e), bs, e_nl)
    kb = each(lambda k, e: _split(k * e), ks, e_nl)
    e_lc = each(lambda c, l: jnp.exp(c - l), lc, L)
    bh = each(lambda b, e: _split(b * e), bs, e_lc)
    kh = each(lambda k, e: _split(k * e), ks, e_lc)
    vs = each(_split, load(v_ref))

    g_a = each(lambda a, b, k: _dot3_multi(a, [b, k], _dot_nt), at, bb, kb)
    g_r = each(lambda a, b, k: _dot3_multi(a, [b, k], _dot_nt), rt, bb, kb)
    a_ab = [jnp.where(strict, g[0], 0.0) for g in g_a]
    a_ak = [_split(jnp.where(strict, g[1], 0.0)) for g in g_a]
    a_rb = [_split(jnp.where(incl, g[0], 0.0)) for g in g_r]
    a_rk = [_split(jnp.where(incl, g[1], 0.0)) for g in g_r]

    pws = each(lambda a: _split(_dot3(a, a)), each(_split, a_ab))
    tm = each(lambda a: eye_c + a, a_ab)
    for step in range(n_sq):
        if step < n_sq - 1:
            res = each(lambda s, t: _dot3_multi(s, [_split(t), s]), pws, tm)
            tm = [t + r[0] for t, r in zip(tm, res)]
            pws = [_split(r[1]) for r in res]
        else:
            tm = each(lambda t, s: t + _dot3(s, _split(t)), tm, pws)
    tms = each(_split, tm)
    w1 = each(lambda a, v: _split(_dot3(a, v)), a_ak, vs)
    tx = each(lambda t, a, w: _dot3_multi(t, [a, w]), tms, at, w1)
    at2 = [_split(r[0]) for r in tx]
    v2 = [_split(r[1]) for r in tx]
    ax = each(lambda a, x, v: _dot3_multi(a, [x, v]), a_rb, at2, v2)
    r2 = [r + x[0] for r, x in zip(rt_f, ax)]
    y2 = [x[1] + _dot3(c, v) for x, c, v in zip(ax, a_rk, vs)]
    bx = each(lambda b, x, v: _dot3_multi(b, [x, v], _dot_tn), bh, at2, v2)
    mm = [jnp.where(eye_p, jnp.exp(c), 0.0) + x[0] for c, x in zip(lc, bx)]
    nn = [x[1] + _dot3(k, v, _dot_tn) for x, k, v in zip(bx, kh, vs)]
    rp_out[...] = jnp.concatenate([x[:C, :] + x[C:, :] for x in r2], axis=1)
    yp_out[...] = jnp.concatenate([x[:C, :] + x[C:, :] for x in y2], axis=1)
    m_out[0] = jnp.stack(mm, axis=0)
    n_out[0] = jnp.stack(nn, axis=0)


def _dot_exact_rhs_left(m, x):
    hi = x.astype(BF16)
    r1 = x - hi.astype(F32)
    mid = r1.astype(BF16)
    lo = (r1 - mid.astype(F32)).astype(BF16)
    return _dot(m, hi) + (_dot(m, mid) + _dot(m, lo))


def _wkv_chunks(arrs, C, row0, n_chunks, pairs_per_step):
    d = arrs[0].shape[1]
    n_pairs = d // PAIR
    assert row0 % C == 0 and n_pairs % pairs_per_step == 0
    blk0 = row0 // C
    lanes = pairs_per_step * PAIR
    in_spec = pl.BlockSpec((C, lanes), lambda c, q: (blk0 + c, q))
    out_row = pl.BlockSpec((C, lanes), lambda c, q: (c, q))
    out_mat = pl.BlockSpec((1, pairs_per_step, PAIR, PAIR), lambda c, q: (c, q, 0, 0))
    t = n_chunks * C
    return pl.pallas_call(
        functools.partial(_wkv_chunk_kernel, C, pairs_per_step),
        grid=(n_chunks, n_pairs // pairs_per_step),
        in_specs=[in_spec] * 6,
        out_specs=[out_row, out_row, out_mat, out_mat],
        out_shape=[jax.ShapeDtypeStruct((t, d), F32), jax.ShapeDtypeStruct((t, d), F32),
                   jax.ShapeDtypeStruct((n_chunks, n_pairs, PAIR, PAIR), F32),
                   jax.ShapeDtypeStruct((n_chunks, n_pairs, PAIR, PAIR), F32)],
        compiler_params=_params(("parallel", "parallel"), 32),
        name=f"wkv_chunks_c{C}",
    )(*arrs)


def _wkv_seq_kernel(n_pairs, n_steps, rp_ref, yp_ref, m_ref, n_ref, s0_ref, y_out, s_out, s_scr):
    j = pl.program_id(1)

    @pl.when(j == 0)
    def _():
        s_scr[...] = s0_ref[0]

    ys, new_s = [], []
    for p in range(n_pairs):
        sl = slice(PAIR * p, PAIR * (p + 1))
        ss = _split(s_scr[p])
        ys.append(_dot3(_split(rp_ref[:, sl]), ss) + yp_ref[:, sl])
        new_s.append(_dot3(_split(m_ref[0, p]), ss) + n_ref[0, p])
    y_out[...] = jnp.concatenate(ys, axis=1)
    s_scr[...] = jnp.stack(new_s, axis=0)

    @pl.when(j == n_steps - 1)
    def _():
        s_out[0] = s_scr[...]


def _wkv_seq(rp, yp, m, nn, s0, C, n_seq, n_steps):
    t, d = rp.shape
    n_pairs = d // PAIR
    row = pl.BlockSpec((C, d), lambda s, j: (s * n_steps + j, 0))
    mat = pl.BlockSpec((1, n_pairs, PAIR, PAIR), lambda s, j: (s * n_steps + j, 0, 0, 0))
    st = pl.BlockSpec((1, n_pairs, PAIR, PAIR), lambda s, j: (s, 0, 0, 0))
    return pl.pallas_call(
        functools.partial(_wkv_seq_kernel, n_pairs, n_steps),
        grid=(n_seq, n_steps),
        in_specs=[row, row, mat, mat, st],
        out_specs=[row, st],
        out_shape=[jax.ShapeDtypeStruct((t, d), F32),
                   jax.ShapeDtypeStruct((n_seq, n_pairs, PAIR, PAIR), F32)],
        scratch_shapes=[pltpu.VMEM((n_pairs, PAIR, PAIR), F32)],
        compiler_params=_params(("arbitrary", "arbitrary"), 32),
        name=f"wkv_seq_c{C}",
    )(rp, yp, m, nn, s0)


def _state_to_blockdiag(s):
    b, h, n, _ = s.shape
    st = jnp.swapaxes(s, -1, -2).reshape(b, h // 2, 2, n, n)
    z = jnp.zeros_like(st[:, :, 0])
    top = jnp.concatenate([st[:, :, 0], z], axis=-1)
    bot = jnp.concatenate([z, st[:, :, 1]], axis=-1)
    return jnp.concatenate([top, bot], axis=-2)


def _blockdiag_to_state(bd):
    b, hp, _, _ = bd.shape
    n = HEAD_DIM
    s0 = bd[:, :, :n, :n]
    s1 = bd[:, :, n:, n:]
    st = jnp.stack([s0, s1], axis=2).reshape(b, hp * 2, n, n)
    return jnp.swapaxes(st, -1, -2)


def _route(x1, rw_ref, rb_ref):
    logits = _dot3(_split(rw_ref[...]), _split(x1), _dot_nt)
    mx = jnp.max(logits, axis=0, keepdims=True)
    ex = jnp.exp(logits - mx)
    scores = ex / jnp.sum(ex, axis=0, keepdims=True)
    sel = scores + rb_ref[...]
    rows = [sel[e:e + 1, :] for e in range(N_EXPERTS)]
    srow = [scores[e:e + 1, :] for e in range(N_EXPERTS)]

    def top2(vals):
        m1 = jnp.maximum(jnp.maximum(vals[0], vals[1]), jnp.maximum(vals[2], vals[3]))
        i1 = jnp.where(vals[0] == m1, 0, jnp.where(vals[1] == m1, 1, jnp.where(vals[2] == m1, 2, 3)))
        rest = [jnp.where(i1 == j, -jnp.inf, vals[j]) for j in range(4)]
        m2 = jnp.maximum(jnp.maximum(rest[0], rest[1]), jnp.maximum(rest[2], rest[3]))
        i2 = jnp.where(rest[0] == m2, 0, jnp.where(rest[1] == m2, 1, jnp.where(rest[2] == m2, 2, 3)))
        return m1, i1, m2, i2

    gscore = []
    for gidx in range(N_GROUPS):
        m1, _, m2, _ = top2(rows[4 * gidx:4 * gidx + 4])
        gscore.append(m1 + m2)
    gm = jnp.maximum(jnp.maximum(gscore[0], gscore[1]), jnp.maximum(gscore[2], gscore[3]))
    gi = jnp.where(gscore[0] == gm, 0, jnp.where(gscore[1] == gm, 1, jnp.where(gscore[2] == gm, 2, 3)))

    def pick(rws, j):
        return jnp.where(gi == 0, rws[j], jnp.where(gi == 1, rws[4 + j],
                                                    jnp.where(gi == 2, rws[8 + j], rws[12 + j])))

    in_grp = [pick(rows, j) for j in range(4)]
    sc_grp = [pick(srow, j) for j in range(4)]
    _, i1, _, i2 = top2(in_grp)

    def at(vals, idx):
        return jnp.where(idx == 0, vals[0], jnp.where(idx == 1, vals[1],
                                                      jnp.where(idx == 2, vals[2], vals[3])))

    ga = at(sc_grp, i1)
    gb = at(sc_grp, i2)
    tot = ga + gb
    ga = ga / tot
    gb = gb / tot
    lo = jnp.minimum(i1, i2)
    hi = jnp.maximum(i1, i2)
    g_lo = jnp.where(i1 < i2, ga, gb)
    g_hi = jnp.where(i1 < i2, gb, ga)
    pair = jnp.where(lo == 0, hi - 1, jnp.where(lo == 1, hi + 1, 5))
    cls = gi * 6 + pair
    return cls.astype(jnp.int32), g_lo, g_hi


def _mix_post_kernel(is_rwkv, alpha_dn, *refs):
    if is_rwkv:
        (x_ref, y_ref, r_ref, k_ref, v_ref, g_ref, lg_ref, lb_ref, rk_ref, bd_ref,
         wo_ref, n1g_ref, n1b_ref, rw_ref, rb_ref, x1_out, cls_out) = refs
        y = y_ref[...]
        mu = _head_sum(y, bd_ref) * (1.0 / HEAD_DIM)
        yc = y - mu
        var = _head_sum(yc * yc, bd_ref) * (1.0 / HEAD_DIM)
        yn = yc * lax.rsqrt(var + GN_EPS) * lg_ref[...] + lb_ref[...]
        v = v_ref[...]
        bonus = _head_sum(r_ref[...] * k_ref[...] * rk_ref[...], bd_ref)
        z = (yn + bonus * v) * g_ref[...]
    else:
        (x_ref, o_ref, gt_ref, wo_ref, n1g_ref, n1b_ref, rw_ref, rb_ref,
         x1_out, cls_out) = refs
        z = o_ref[...] * gt_ref[...]
    h = _dot(z.astype(BF16), wo_ref[...])
    x1 = _layer_norm(alpha_dn * x_ref[...] + h, n1g_ref[...], n1b_ref[...])
    cls, g_lo, g_hi = _route(x1, rw_ref, rb_ref)
    tm, d = x1.shape
    cls_out[...] = jnp.broadcast_to(cls, (8, tm))
    pieces = [x.astype(F32) for x in _split3(g_lo) + _split3(g_hi)]
    gs = jnp.concatenate(pieces + [jnp.zeros((16 - len(pieces), tm), F32)], axis=0).astype(BF16)
    rr = lax.broadcasted_iota(jnp.int32, (16, GATE_LANES), 0)
    cc = lax.broadcasted_iota(jnp.int32, (16, GATE_LANES), 1)
    place = jnp.where(jnp.logical_and(rr < 6, cc == jnp.where(rr < 3, 0, 1)), 1.0, 0.0).astype(BF16)
    x1_out[:, :d] = x1
    x1_out[:, d:] = _dot_tn(gs, place)


def _mix_post(is_rwkv, alpha_dn, acts, consts, tm):
    n, d = acts[0].shape
    row = pl.BlockSpec((tm, d), lambda i: (i, 0))
    specs = [row] * len(acts) + [_const_spec(c.shape) for c in consts]
    lane = pl.BlockSpec((8, tm), lambda i: (0, i))
    wide = pl.BlockSpec((tm, d + GATE_LANES), lambda i: (i, 0))
    return pl.pallas_call(
        functools.partial(_mix_post_kernel, is_rwkv, alpha_dn),
        grid=(n // tm,),
        in_specs=specs,
        out_specs=[wide, lane],
        out_shape=[jax.ShapeDtypeStruct((n, d + GATE_LANES), F32), jax.ShapeDtypeStruct((8, n), jnp.int32)],
        compiler_params=_params(("parallel",), 48),
        name="mix_post_rwkv" if is_rwkv else "mix_post_fox",
    )(*acts, *consts)


def _moe_kernel(n_tok, d, d_exp, n_blk, e1_ref, e2_ref, off_ref, nv_ref, order_ref, x_hbm,
                wgu1_ref, wd1_ref, wgu2_ref, wd2_ref, out_hbm, xbuf, obuf, gsem, ssem):
    b = pl.program_id(0)
    slot = lax.rem(b, 2)
    nv_b = nv_ref[b]
    unroll = 8

    def gather_start(blk, sl):
        base = off_ref[blk]

        def body(c, carry):
            for u in range(unroll):
                i = c * unroll + u
                tok = order_ref[jnp.minimum(base + i, n_tok - 1)]
                pltpu.make_async_copy(x_hbm.at[pl.ds(tok, 1)], xbuf.at[sl, pl.ds(i, 1)], gsem.at[sl]).start()
            return carry

        lax.fori_loop(0, E_BLOCK // unroll, body, 0)

    def gather_wait(sl):
        pltpu.make_async_copy(x_hbm.at[pl.ds(0, E_BLOCK)], xbuf.at[sl], gsem.at[sl]).wait()

    def scatter_start(blk, sl):
        base = off_ref[blk]

        def body(i, carry):
            tok = order_ref[base + i]
            pltpu.make_async_copy(obuf.at[sl, pl.ds(i, 1)], out_hbm.at[pl.ds(tok, 1)], ssem.at[sl]).start()
            return carry

        lax.fori_loop(0, nv_ref[blk], body, 0)

    def scatter_wait(blk, sl):
        nv = nv_ref[blk]

        @pl.when(nv == E_BLOCK)
        def _():
            pltpu.make_async_copy(obuf.at[sl], out_hbm.at[pl.ds(0, E_BLOCK)], ssem.at[sl]).wait()

        @pl.when(nv < E_BLOCK)
        def _():
            def body(i, carry):
                pltpu.make_async_copy(obuf.at[sl, pl.ds(0, 1)], out_hbm.at[pl.ds(0, 1)], ssem.at[sl]).wait()
                return carry

            lax.fori_loop(0, nv, body, 0)

    @pl.when(jnp.logical_and(b == 0, nv_b > 0))
    def _():
        gather_start(0, 0)

    nxt = jnp.minimum(b + 1, n_blk - 1)

    @pl.when(jnp.logical_and(b + 1 < n_blk, nv_ref[nxt] > 0))
    def _():
        gather_start(nxt, 1 - slot)

    @pl.when(nv_b > 0)
    def _():
        gather_wait(slot)
        rows = xbuf[slot]
        xb = rows[:, :d].astype(BF16)

        def expert(wgu_ref, wd_ref):
            gu = _dot(xb, wgu_ref[0])
            hmid = jax.nn.silu(gu[:, :d_exp]) * gu[:, d_exp:]
            return _dot(hmid.astype(BF16), wd_ref[0])

        y1 = expert(wgu1_ref, wd1_ref)
        y2 = expert(wgu2_ref, wd2_ref)
        obuf[slot] = y1 * rows[:, d:d + 1] + y2 * rows[:, d + 1:d + 2]

    prev = jnp.maximum(b - 1, 0)

    @pl.when(jnp.logical_and(b > 0, nv_ref[prev] > 0))
    def _():
        scatter_wait(prev, 1 - slot)

    @pl.when(nv_b > 0)
    def _():
        scatter_start(b, slot)

    @pl.when(jnp.logical_and(b == n_blk - 1, nv_b > 0))
    def _():
        scatter_wait(b, slot)


def _moe(xg, cls, w_gu, w_down):
    n = xg.shape[0]
    d, d_exp = w_down.shape[2], w_down.shape[1]
    n_blk = (n + N_CLASSES * (E_BLOCK - 1) + E_BLOCK - 1) // E_BLOCK
    counts = jnp.bincount(cls, length=N_CLASSES).astype(jnp.int32)
    padded = (counts + E_BLOCK - 1) // E_BLOCK * E_BLOCK
    pad_end = jnp.cumsum(padded)
    pad_start = pad_end - padded
    seg_start = jnp.cumsum(counts) - counts
    order = jnp.argsort(cls).astype(jnp.int32)
    blk_start = jnp.arange(n_blk, dtype=jnp.int32) * E_BLOCK
    blk_cls = jnp.minimum(jnp.searchsorted(pad_end, blk_start, side='right'), N_CLASSES - 1).astype(jnp.int32)
    rank0 = blk_start - pad_start[blk_cls]
    blk_nv = jnp.clip(counts[blk_cls] - rank0, 0, E_BLOCK).astype(jnp.int32)
    blk_off = (seg_start[blk_cls] + rank0).astype(jnp.int32)
    n_used = jnp.sum((blk_nv > 0).astype(jnp.int32))
    blk_cls = jnp.where(blk_nv > 0, blk_cls, blk_cls[jnp.maximum(n_used - 1, 0)])
    pair_lo = jnp.asarray(_PAIR_LO, jnp.int32)
    pair_hi = jnp.asarray(_PAIR_HI, jnp.int32)
    blk_e1 = (blk_cls // 6) * EXPERTS_PER_GROUP + pair_lo[blk_cls % 6]
    blk_e2 = (blk_cls // 6) * EXPERTS_PER_GROUP + pair_hi[blk_cls % 6]

    gu1 = pl.BlockSpec((1, d, 2 * d_exp), lambda b, e1, e2, of, nv, od: (e1[b], 0, 0))
    dn1 = pl.BlockSpec((1, d_exp, d), lambda b, e1, e2, of, nv, od: (e1[b], 0, 0))
    gu2 = pl.BlockSpec((1, d, 2 * d_exp), lambda b, e1, e2, of, nv, od: (e2[b], 0, 0))
    dn2 = pl.BlockSpec((1, d_exp, d), lambda b, e1, e2, of, nv, od: (e2[b], 0, 0))
    grid_spec = pltpu.PrefetchScalarGridSpec(
        num_scalar_prefetch=5,
        grid=(n_blk,),
        in_specs=[pl.BlockSpec(memory_space=pl.ANY), gu1, dn1, gu2, dn2],
        out_specs=pl.BlockSpec(memory_space=pl.ANY),
        scratch_shapes=[pltpu.VMEM((2, E_BLOCK, d + GATE_LANES), F32), pltpu.VMEM((2, E_BLOCK, d), F32),
                        pltpu.SemaphoreType.DMA((2,)), pltpu.SemaphoreType.DMA((2,))],
    )
    return pl.pallas_call(
        functools.partial(_moe_kernel, n, d, d_exp, n_blk),
        grid_spec=grid_spec,
        out_shape=jax.ShapeDtypeStruct((n, d), F32),
        compiler_params=_params(("arbitrary",), 40),
        name="moe_experts",
    )(blk_e1, blk_e2, blk_off, blk_nv, order, xg, w_gu, w_down, w_gu, w_down)


def _bias_placement(nh, lead_ones):
    p = np.zeros((4 * nh, nh * PAIR), np.float32)
    for h in range(nh):
        lane0 = PAIR * h + HEAD_DIM
        if lead_ones:
            p[h, lane0:lane0 + 3] = 1.0
            for piece in range(3):
                p[(piece + 1) * nh + h, lane0 + 3 + piece] = 1.0
        else:
            for piece in range(3):
                p[piece * nh + h, lane0 + piece] = 1.0
            p[3 * nh + h, lane0 + 3:lane0 + 6] = 1.0
    return jnp.asarray(p, BF16)


def _pad_heads(w, width):
    d, hd = w.shape
    nh = hd // HEAD_DIM
    w3 = w.reshape(d, nh, HEAD_DIM)
    return jnp.concatenate([w3, jnp.zeros((d, nh, width - HEAD_DIM), w.dtype)], axis=-1).reshape(d, nh * width)


def _bias_rows(rem, lead_ones):
    parts = [x.astype(F32) for x in _split3(rem)]
    ones = [jnp.ones_like(rem)]
    return jnp.concatenate(ones + parts if lead_ones else parts + ones, axis=0).astype(BF16)


def _ln2_kernel(alpha_dn, with_kv, tiles_per_blk, *refs):
    if with_kv:
        (x1_ref, m_ref, g_ref, b_ref, wk_ref, wv_ref, wka_ref, wva_ref, wf_ref, bf_ref, tri_ref,
         pk_ref, eye_ref, ones_ref,
         x2_out, k_out, v_out, ka_out, vt_out, lf_out, cum_out, rem_out, carry, base_scr) = refs
    else:
        x1_ref, m_ref, g_ref, b_ref, x2_out = refs
    x2 = _layer_norm(alpha_dn * x1_ref[...] + m_ref[...], g_ref[...], b_ref[...])
    x2_out[...] = x2
    if with_kv:
        i = pl.program_id(0)
        xb = x2.astype(BF16)
        k_out[...] = _dot(xb, wk_ref[...])
        v_out[...] = _dot(xb, wv_ref[...])
        z = _dot3(_split(wf_ref[...]), _split(x2), _dot_nt) + bf_ref[...]
        lf = jnp.minimum(z, 0.0) - jnp.log1p(jnp.exp(-jnp.abs(z)))
        lf_out[...] = lf

        @pl.when(i == 0)
        def _():
            carry[...] = jnp.zeros_like(carry)

        cum = _dot_exact_rhs(lf, tri_ref[...]) + carry[...]
        cum_out[...] = cum
        carry[...] = cum[:, -1:]

        cum2 = cum * LOG2E

        @pl.when(lax.rem(i, tiles_per_blk) == 0)
        def _():
            base_scr[...] = cum2[:, :1]

        rem = cum2 - base_scr[...]
        rem_out[...] = rem
        ka_out[...] = (_dot(xb, wka_ref[...]) + _dot_tn(_bias_rows(-rem, False), pk_ref[...])).astype(BF16)
        vw = _dot(xb, wva_ref[...]).astype(BF16)
        vt_out[...] = (_dot_tn(vw, eye_ref[...]) + ones_ref[...]).astype(BF16)


def _ln2(alpha_dn, x1, m, g, b, kv, tm):
    n, d = m.shape
    row = pl.BlockSpec((tm, d), lambda i: (i, 0))
    with_kv = kv is not None
    ins = [x1, m, g, b]
    specs = [row, row, _const_spec(g.shape), _const_spec(b.shape)]
    outs = [jax.ShapeDtypeStruct((n, d), F32)]
    ospecs = [row]
    scratch = []
    if with_kv:
        nh = kv['wf_t'].shape[0]
        tri = (jnp.arange(tm)[:, None] <= jnp.arange(tm)[None, :]).astype(BF16)
        eye = jnp.eye(tm, dtype=BF16)
        ones_col = (jnp.arange(nh * V_ROWS) % V_ROWS == HEAD_DIM).astype(F32).reshape(-1, 1)
        extra = [kv['wk'], kv['wv'], kv['wk_aug'], kv['wv_aug'], kv['wf_t'], kv['bf'], tri,
                 _bias_placement(nh, False), eye, ones_col]
        ins += extra
        specs += [_const_spec(e.shape) for e in extra]
        lane = pl.BlockSpec((nh, tm), lambda i: (0, i))
        outs += [jax.ShapeDtypeStruct((n, d), F32), jax.ShapeDtypeStruct((n, d), F32),
                 jax.ShapeDtypeStruct((n, nh * PAIR), BF16), jax.ShapeDtypeStruct((nh * V_ROWS, n), BF16),
                 jax.ShapeDtypeStruct((nh, n), F32), jax.ShapeDtypeStruct((nh, n), F32),
                 jax.ShapeDtypeStruct((nh, n), F32)]
        ospecs += [row, row, pl.BlockSpec((tm, nh * PAIR), lambda i: (i, 0)),
                   pl.BlockSpec((nh * V_ROWS, tm), lambda i: (0, i)), lane, lane, lane]
        scratch = [pltpu.VMEM((nh, 1), F32), pltpu.VMEM((nh, 1), F32)]
    res = pl.pallas_call(
        functools.partial(_ln2_kernel, alpha_dn, with_kv, kv['tiles_per_blk'] if with_kv else 1),
        grid=(n // tm,),
        in_specs=specs,
        out_specs=ospecs,
        out_shape=outs,
        scratch_shapes=scratch,
        compiler_params=_params(("arbitrary",), 48),
        name="ln2_kv" if with_kv else "ln2",
    )(*ins)
    return res


def _fox_qg_kernel(x_ref, rem_ref, wq_ref, wqa_ref, wg_ref, pq_ref, q_out, qa_out, gate_out):
    xb = x_ref[...].astype(BF16)
    scale = LOG2E * HEAD_DIM ** -0.5
    q_out[...] = (_dot(xb, wq_ref[...]) * scale).astype(BF16)
    qa_out[...] = (_dot(xb, wqa_ref[...]) * scale
                   + _dot_tn(_bias_rows(rem_ref[...], True), pq_ref[...])).astype(BF16)
    gate_out[...] = jax.nn.sigmoid(_dot(xb, wg_ref[...]))


def _fox_qg(x, rem_r, wq, wq_aug, wg, tm):
    n, d = x.shape
    nh = rem_r.shape[0]
    row = pl.BlockSpec((tm, d), lambda i: (i, 0))
    pq = _bias_placement(nh, True)
    return pl.pallas_call(
        _fox_qg_kernel,
        grid=(n // tm,),
        in_specs=[row, pl.BlockSpec((nh, tm), lambda i: (0, i)), _const_spec(wq.shape),
                  _const_spec(wq_aug.shape), _const_spec(wg.shape), _const_spec(pq.shape)],
        out_specs=[row, pl.BlockSpec((tm, nh * PAIR), lambda i: (i, 0)), row],
        out_shape=[jax.ShapeDtypeStruct((n, d), BF16), jax.ShapeDtypeStruct((n, nh * PAIR), BF16),
                   jax.ShapeDtypeStruct((n, d), F32)],
        compiler_params=_params(("parallel",), 48),
        name="fox_qg",
    )(x, rem_r, wq, wq_aug, wg, pq)


V_ROWS = 80


def _fox_prompt_kernel(tq, hps, nh, base_ref, q_ref, k_ref, vt_ref, o_out, s_scr, p_scr):
    g = pl.program_id(0)
    i = pl.program_id(1)
    rr = lax.broadcasted_iota(jnp.int32, (tq, tq), 0)
    cc = lax.broadcasted_iota(jnp.int32, (tq, tq), 1)
    causal = rr <= cc

    def scores(j, h):
        k0 = pl.multiple_of(j * tq, tq)
        lanes = slice(PAIR * h, PAIR * (h + 1))
        return _dot_nt(k_ref[pl.ds(k0, tq), lanes], q_ref[:, lanes])

    def base_gap(j, h):
        head = g * hps + h
        return base_ref[i * nh + head] - base_ref[j * nh + head]

    def softmax(s, m, gap):
        m_new = jnp.maximum(m, jnp.max(s, axis=0, keepdims=True) + gap)
        return jnp.exp2(s - (m_new - gap)).astype(BF16), jnp.exp2(m - m_new), m_new

    def accum(acc, corr, p, j, h):
        k0 = pl.multiple_of(j * tq, tq)
        return acc * corr + _dot(vt_ref[h, :, pl.ds(k0, tq)], p)

    for h in range(hps):
        s_scr[h] = scores(0, h)
        p_scr[h] = jnp.zeros((tq, tq), BF16)

    def body(j, carry):
        new = []
        for h in range(hps):
            m, acc, corr_prev = carry[h]
            acc = accum(acc, corr_prev, p_scr[h], jnp.maximum(j - 1, 0), h)
            p, corr, m = softmax(s_scr[h], m, base_gap(j, h))
            p_scr[h] = p
            s_scr[h] = scores(j + 1, h)
            new.append((m, acc, corr))
        return tuple(new)

    init = tuple((jnp.full((1, tq), NEG_INF, F32), jnp.zeros((V_ROWS, tq), F32), jnp.ones((1, tq), F32))
                 for _ in range(hps))
    carry = lax.fori_loop(0, i, body, init)
    outs = []
    for h in range(hps):
        m, acc, corr_prev = carry[h]
        acc = accum(acc, corr_prev, p_scr[h], jnp.maximum(i - 1, 0), h)
        p, corr, m = softmax(jnp.where(causal, s_scr[h], NEG_INF), m, 0.0)
        acc = accum(acc, corr, p, i, h)
        outs.append(acc[:HEAD_DIM, :] / acc[HEAD_DIM:HEAD_DIM + 1, :])
    o_out[...] = jnp.concatenate(outs, axis=0)


def _fox_prompt(base, q_aug, k_aug, vt_aug, t, tq, hps):
    nh = vt_aug.shape[0]
    return pl.pallas_call(
        functools.partial(_fox_prompt_kernel, tq, hps, nh),
        grid=(nh // hps, t // tq),
        in_specs=[pl.BlockSpec(memory_space=pltpu.SMEM),
                  pl.BlockSpec((tq, hps * PAIR), lambda g, i: (i, g)),
                  pl.BlockSpec((t, hps * PAIR), lambda g, i: (0, g), pipeline_mode=pl.Buffered(1)),
                  pl.BlockSpec((hps, V_ROWS, t), lambda g, i: (g, 0, 0), pipeline_mode=pl.Buffered(1))],
        out_specs=pl.BlockSpec((hps * HEAD_DIM, tq), lambda g, i: (g, i)),
        out_shape=jax.ShapeDtypeStruct((nh * HEAD_DIM, t), F32),
        scratch_shapes=[pltpu.VMEM((hps, tq, tq), F32), pltpu.VMEM((hps, tq, tq), BF16)],
        compiler_params=_params(("parallel", "parallel"), 48),
        name="fox_prompt_attn",
    )(base, q_aug, k_aug, vt_aug)


def _fox_sample_kernel(n_pairs, q_ref, kn_ref, vn_ref, lfn_ref, kc_ref, vc_ref, lfc_ref,
                       triu_ref, o_out):
    t = q_ref.shape[0]
    past = kc_ref.shape[1]
    lfc = lfc_ref[0]
    lfn = lfn_ref[0]
    nh = lfc.shape[1]
    lfc_s = _split3(lfc)
    cum_c_r = _dot_tn3(lfc_s, triu_ref[...])
    ones_row = jnp.ones((1, past), BF16)
    tot_r = _dot_m3(ones_row, lfc_s)
    tot_c = cum_c_r[:, past - 1:past]
    rr = lax.broadcasted_iota(jnp.int32, (t, t), 0)
    cc = lax.broadcasted_iota(jnp.int32, (t, t), 1)
    causal = rr >= cc
    tril = jnp.where(causal, 1.0, 0.0).astype(BF16)
    lfn_s = _split3(lfn)
    cum_n_c = (_dot_m3(tril, lfn_s) + tot_r) * LOG2E
    cum_n_r = (_dot_tn3(lfn_s, jnp.where(rr <= cc, 1.0, 0.0).astype(BF16)) + tot_c) * LOG2E
    cum_c_r = cum_c_r * LOG2E
    lane = lax.broadcasted_iota(jnp.int32, (1, PAIR), 1)
    head0 = lane < HEAD_DIM
    hl = lax.broadcasted_iota(jnp.int32, (1, nh), 1)
    for p in range(n_pairs):
        sl = slice(PAIR * p, PAIR * (p + 1))
        q = q_ref[:, sl]
        zero = jnp.zeros_like(q)
        kc = kc_ref[0, :, sl].astype(BF16)
        vc = vc_ref[0, :, sl].astype(BF16)
        kn = kn_ref[:, sl].astype(BF16)
        vn = vn_ref[:, sl].astype(BF16)
        outs = []
        for h in range(2):
            hh = 2 * p + h
            qh = jnp.where(head0, q, zero) if h == 0 else jnp.where(head0, zero, q)
            cq = jnp.sum(jnp.where(hl == hh, cum_n_c, 0.0), axis=1, keepdims=True)
            s_c = _dot_nt(qh, kc) + cq - cum_c_r[hh:hh + 1, :]
            s_n = _dot_nt(qh, kn) + cq - cum_n_r[hh:hh + 1, :]
            s_n = jnp.where(causal, s_n, NEG_INF)
            m = jnp.maximum(jnp.max(s_c, axis=1, keepdims=True), jnp.max(s_n, axis=1, keepdims=True))
            p_c = jnp.exp2(s_c - m)
            p_n = jnp.exp2(s_n - m)
            l = jnp.sum(p_c, axis=1, keepdims=True) + jnp.sum(p_n, axis=1, keepdims=True)
            acc = _dot(p_c.astype(BF16), vc) + _dot(p_n.astype(BF16), vn)
            outs.append(acc / l)
        o_out[:, sl] = jnp.where(head0, outs[0], outs[1])


def _split3(x):
    hi = x.astype(BF16)
    r1 = x - hi.astype(F32)
    mid = r1.astype(BF16)
    lo = (r1 - mid.astype(F32)).astype(BF16)
    return hi, mid, lo


def _dot_tn3(xs, m):
    return _dot_tn(xs[0], m) + (_dot_tn(xs[1], m) + _dot_tn(xs[2], m))


def _dot_m3(m, xs):
    return _dot(m, xs[0]) + (_dot(m, xs[1]) + _dot(m, xs[2]))


def _fox_sample(qb, k, v, lf_new, cache_k, cache_v, cache_lf, row0, n_stream, t):
    d = qb.shape[1]
    n_pairs = d // PAIR
    past = cache_k.shape[1]
    nh = cache_lf.shape[2]
    blk0 = row0 // t
    triu = (jnp.arange(past)[:, None] <= jnp.arange(past)[None, :]).astype(BF16)
    row = pl.BlockSpec((t, d), lambda b: (blk0 + b, 0))
    return pl.pallas_call(
        functools.partial(_fox_sample_kernel, n_pairs),
        grid=(n_stream,),
        in_specs=[row, row, row,
                  pl.BlockSpec((1, t, nh), lambda b: (b, 0, 0)),
                  pl.BlockSpec((1, past, d), lambda b: (b, 0, 0)),
                  pl.BlockSpec((1, past, d), lambda b: (b, 0, 0)),
                  pl.BlockSpec((1, past, nh), lambda b: (b, 0, 0)),
                  _const_spec(triu.shape)],
        out_specs=pl.BlockSpec((t, d), lambda b: (b, 0)),
        out_shape=jax.ShapeDtypeStruct((n_stream * t, d), F32),
        compiler_params=_params(("parallel",), 48),
        name="fox_sample_attn",
    )(qb, k, v, lf_new, cache_k, cache_v, cache_lf, triu)


def kernel(x_prompt, x_sample, state_shift, state_wkv, cache_k, cache_v, cache_logf, ln1_g, ln1_b, ln2_g, ln2_b, a_mix, a_w_rkv, a_w0, a_w1, a_w2, a_a0, a_a1, a_a2, a_v0, a_v1, a_v2, a_g1, a_g2, a_k_k, a_k_a, a_r_k, a_lnx_g, a_lnx_b, a_w_o, kv_w, kv_bf, b_w_qg, b_w_o, router_w, router_b, moe_w_gu, moe_w_down):
    nb, seq, d = x_prompt.shape
    db, dt, _ = x_sample.shape
    assert nb == 1
    n_heads = d // HEAD_DIM
    depth = ln1_g.shape[0]
    n_a = a_mix.shape[0]
    past = cache_k.shape[1]
    t_p = nb * seq
    t_s = db * dt
    n = t_p + t_s
    tm = _row_tile(n)
    c_p = 64
    c_s = dt
    tq_attn = 512 if t_p % 512 == 0 else 256
    assert seq % c_p == 0 and t_p % c_s == 0 and (c_s & (c_s - 1)) == 0
    alpha_dn = (2 * depth) ** 0.25
    row2 = lambda a: a.reshape(1, -1)

    bd = (jnp.arange(256)[:, None] // HEAD_DIM == jnp.arange(256)[None, :] // HEAD_DIM).astype(BF16)
    rw_t = router_w.T
    rb_c = router_b.reshape(-1, 1)

    x = jnp.concatenate([x_prompt.reshape(t_p, d), x_sample.reshape(t_s, d)], axis=0)
    new_shift_p, new_shift_s, new_wkv_p, new_wkv_s = [], [], [], []
    v_first = None
    kv = None
    for l in range(depth):
        if l < n_a:
            xs = x[t_p:].reshape(db, dt, d)
            new_shift_p.append(x[t_p - 1:t_p].reshape(nb, d))
            new_shift_s.append(xs[:, -1])
            xp = jnp.concatenate([
                jnp.zeros((1, d), F32), x[:t_p - 1],
                jnp.concatenate([state_shift[l][:, None, :], xs[:, :-1]], axis=1).reshape(t_s, d)], axis=0)
            w = dict(mix=jnp.concatenate([a_mix[l], jnp.zeros((2, d), F32)], axis=0),
                     wr=a_w_rkv[l, 0].astype(BF16), wk=a_w_rkv[l, 1].astype(BF16), wv=a_w_rkv[l, 2].astype(BF16),
                     w1=a_w1[l].astype(BF16), w2=a_w2[l].astype(BF16), w0=row2(a_w0[l]),
                     a1=a_a1[l].astype(BF16), a2=a_a2[l].astype(BF16), a0=row2(a_a0[l]),
                     g1=a_g1[l].astype(BF16), g2=a_g2[l].astype(BF16),
                     k_k=row2(a_k_k[l]), k_a=row2(a_k_a[l]), bd=bd)
            if l > 0:
                w.update(v1=a_v1[l - 1].astype(BF16), v2=a_v2[l - 1].astype(BF16), v0=row2(a_v0[l - 1]))
            r, lw, k, v, kn, b, g = _rwkv_pre(x, xp, v_first if l > 0 else None, w, tm)
            if l == 0:
                v_first = v
            scan_in = (r, lw, k, v, kn, b)
            rp, yp, mm, nn = _wkv_chunks(scan_in, c_p, 0, t_p // c_p, n_heads // 2)
            s0 = jnp.zeros((1, n_heads // 2, PAIR, PAIR), F32)
            y_p, sf_p = _wkv_seq(rp, yp, mm, nn, s0, c_p, 1, t_p // c_p)
            rp, yp, mm, nn = _wkv_chunks(scan_in, c_s, t_p, db, n_heads // 2)
            y_s, sf_s = _wkv_seq(rp, yp, mm, nn, _state_to_blockdiag(state_wkv[l]), c_s, db, 1)
            new_wkv_p.append(_blockdiag_to_state(sf_p))
            new_wkv_s.append(_blockdiag_to_state(sf_s))
            y = jnp.concatenate([y_p, y_s], axis=0)
            consts = [row2(a_lnx_g[l]), row2(a_lnx_b[l]), row2(a_r_k[l]), bd, a_w_o[l].astype(BF16),
                      row2(ln1_g[l]), row2(ln1_b[l]), rw_t, rb_c]
            x1, cls = _mix_post(True, alpha_dn, [x, y, r, k, v, g], consts, tm)
        else:
            lb = l - n_a
            wq = b_w_qg[lb][:, :d].astype(BF16)
            qb, q_aug, gate = _fox_qg(x, kv['rem_r'], wq, _pad_heads(wq, PAIR), b_w_qg[lb][:, d:].astype(BF16), tm)
            o_p = _fox_prompt(kv['base'], q_aug, kv['k_aug'], kv['vt_aug'], t_p, tq_attn, 4).T
            o_s = _fox_sample(qb, kv['k'], kv['v'], kv['lf_s'], cache_k.reshape(db, past, d),
                              cache_v.reshape(db, past, d), cache_logf, t_p, db, dt)
            o = jnp.concatenate([o_p, o_s], axis=0)
            consts = [b_w_o[lb].astype(BF16), row2(ln1_g[l]), row2(ln1_b[l]), rw_t, rb_c]
            x1, cls = _mix_post(False, alpha_dn, [x, o, gate], consts, tm)
        m = _moe(x1, cls[0], moe_w_gu[l].astype(BF16), moe_w_down[l].astype(BF16))
        if l == n_a - 1:
            wk, wv = kv_w[:, :d].astype(BF16), kv_w[:, d:2 * d].astype(BF16)
            kvw = dict(wk=wk, wv=wv, wk_aug=_pad_heads(wk, PAIR), wv_aug=_pad_heads(wv, V_ROWS),
                       wf_t=kv_w[:, 2 * d:].T, bf=kv_bf.reshape(-1, 1), tiles_per_blk=tq_attn // tm)
            x, k_all, v_all, k_aug, vt_aug, lf_r, cum_r, rem_r = _ln2(
                alpha_dn, x1, m, row2(ln2_g[l]), row2(ln2_b[l]), kvw, tm)
            base = (cum_r[:, :t_p:tq_attn] * LOG2E).T.reshape(-1)
            kv = dict(k=k_all, v=v_all, base=base, rem_r=rem_r, k_aug=k_aug,
                      vt_aug=vt_aug.reshape(n_heads, V_ROWS, n), lf_p=lf_r[:, :t_p].T,
                      lf_s=lf_r[:, t_p:].T.reshape(db, dt, n_heads))
        else:
            x = _ln2(alpha_dn, x1, m, row2(ln2_g[l]), row2(ln2_b[l]), None, tm)[0]

    y_prompt = x[:t_p].reshape(nb, seq, d)
    y_sample = x[t_p:].reshape(db, dt, d)
    p_k = kv['k'][:t_p].reshape(nb, seq, n_heads, HEAD_DIM)
    p_v = kv['v'][:t_p].reshape(nb, seq, n_heads, HEAD_DIM)
    s_k = kv['k'][t_p:].reshape(db, dt, n_heads, HEAD_DIM)
    s_v = kv['v'][t_p:].reshape(db, dt, n_heads, HEAD_DIM)
    return (y_prompt, y_sample, jnp.stack(new_shift_p), jnp.stack(new_wkv_p), p_k, p_v,
            kv['lf_p'].reshape(nb, seq, n_heads), jnp.stack(new_shift_s), jnp.stack(new_wkv_s),
            s_k, s_v, kv['lf_s'])
```

```python
import functools
import math

import jax
import jax.numpy as jnp
import numpy as np
from jax import lax
from jax.experimental import pallas as pl
from jax.experimental.pallas import tpu as pltpu

F32 = jnp.float32
BF16 = jnp.bfloat16

HEAD_DIM = 64
PAIR = 2 * HEAD_DIM
N_EXPERTS = 16
N_GROUPS = 4
EXPERTS_PER_GROUP = 4
N_CLASSES = 24
E_BLOCK = 128
GATE_LANES = 128
LN_EPS = 1e-5
GN_EPS = 64e-5
NEG_INF = -1e30
LOG2E = 1.4426950408889634
MIB = 2 ** 20

_PAIR_LO = (0, 0, 0, 1, 1, 2)
_PAIR_HI = (1, 2, 3, 2, 3, 3)


def _params(sem, vmem_mib):
    return pltpu.CompilerParams(dimension_semantics=sem, vmem_limit_bytes=vmem_mib * MIB)


def _dot(a, b):
    return jnp.dot(a, b, preferred_element_type=F32)


def _dot_nt(a, b):
    return lax.dot_general(a, b, (((1,), (1,)), ((), ())), preferred_element_type=F32)


def _dot_tn(a, b):
    return lax.dot_general(a, b, (((0,), (0,)), ((), ())), preferred_element_type=F32)


def _split(x):
    hi = x.astype(BF16)
    lo = (x - hi.astype(F32)).astype(BF16)
    return hi, lo


def _dot3(a, b, dot=_dot):
    return _dot3_multi(a, [b], dot)[0]


def _dot3_multi(a, bs, dot=_dot):
    ah, al = a
    ax = 0 if dot is _dot_nt else 1
    ns = [b[0].shape[ax] for b in bs]
    if any(n % 128 for n in ns):
        return [dot(ah, bh) + (dot(ah, bl) + dot(al, bh)) for bh, bl in bs]
    kax = 0 if dot is _dot_tn else 1
    if ah.shape[kax] == 128:
        lhs = jnp.concatenate([ah, al], axis=kax)
        if dot is _dot_nt:
            rhs = jnp.concatenate([jnp.concatenate([x, y], axis=1) for bh, bl in bs
                                   for x, y in ((bh, bh), (bl, jnp.zeros_like(bl)))], axis=0)
        else:
            top = jnp.concatenate([x for b in bs for x in b], axis=1)
            bot = jnp.concatenate([x for bh, bl in bs for x in (bh, jnp.zeros_like(bl))], axis=1)
            rhs = jnp.concatenate([top, bot], axis=0)
        r = dot(lhs, rhs)
        outs, o = [], 0
        for n in ns:
            outs.append(r[:, o:o + n] + r[:, o + n:o + 2 * n])
            o += 2 * n
        return outs
    r1 = dot(ah, jnp.concatenate([x for b in bs for x in b], axis=ax))
    r2 = dot(al, jnp.concatenate([b[0] for b in bs], axis=ax)) if len(bs) > 1 else dot(al, bs[0][0])
    outs, o1, o2 = [], 0, 0
    for n in ns:
        outs.append((r1[:, o1:o1 + n] + r1[:, o1 + n:o1 + 2 * n]) + r2[:, o2:o2 + n])
        o1 += 2 * n
        o2 += n
    return outs


def _dot_exact_rhs(x, m, dot=_dot):
    hi = x.astype(BF16)
    r1 = x - hi.astype(F32)
    mid = r1.astype(BF16)
    lo = (r1 - mid.astype(F32)).astype(BF16)
    return dot(hi, m) + (dot(mid, m) + dot(lo, m))


def _head_sum(x, bd_ref):
    bd = bd_ref[...]
    parts = []
    for j in range(x.shape[1] // 256):
        parts.append(_dot_exact_rhs(x[:, 256 * j:256 * (j + 1)], bd))
    return jnp.concatenate(parts, axis=1)


def _layer_norm(z, g, b):
    mu = jnp.mean(z, axis=-1, keepdims=True)
    zc = z - mu
    var = jnp.mean(zc * zc, axis=-1, keepdims=True)
    return zc * lax.rsqrt(var + LN_EPS) * g + b


def _row_tile(n):
    for t in (256, 128, 64, 32, 16, 8):
        if n % t == 0:
            return t
    raise ValueError(n)


def _const_spec(shape):
    nd = len(shape)
    return pl.BlockSpec(shape, lambda *_: (0,) * nd)


def _rwkv_pre_kernel(has_vres, *refs):
    if has_vres:
        (x_ref, xp_ref, vf_ref, mix_ref, wr_ref, wk_ref, wv_ref, w1_ref, w2_ref, w0_ref,
         a1_ref, a2_ref, a0_ref, g1_ref, g2_ref, v1_ref, v2_ref, v0_ref, kk_ref, ka_ref, bd_ref,
         r_out, lw_out, k_out, v_out, kn_out, b_out, g_out) = refs
    else:
        (x_ref, xp_ref, mix_ref, wr_ref, wk_ref, wv_ref, w1_ref, w2_ref, w0_ref,
         a1_ref, a2_ref, a0_ref, g1_ref, g2_ref, kk_ref, ka_ref, bd_ref,
         r_out, lw_out, k_out, v_out, kn_out, b_out, g_out) = refs
    x = x_ref[...]
    xx = xp_ref[...] - x

    def mixed(i):
        return (x + xx * mix_ref[i:i + 1, :]).astype(BF16)

    xr, xw, xk, xv, xa, xg = (mixed(i) for i in range(6))
    r = _dot(xr, wr_ref[...])
    k = _dot(xk, wk_ref[...])
    v = _dot(xv, wv_ref[...])
    zw = w0_ref[...] + _dot(jnp.tanh(_dot(xw, w1_ref[...])).astype(BF16), w2_ref[...])
    lw = (-math.exp(-0.5)) * jax.nn.sigmoid(zw)
    alpha = jax.nn.sigmoid(a0_ref[...] + _dot(_dot(xa, a1_ref[...]).astype(BF16), a2_ref[...]))
    g = _dot(jax.nn.sigmoid(_dot(xg, g1_ref[...])).astype(BF16), g2_ref[...])
    if has_vres:
        gate_v = jax.nn.sigmoid(v0_ref[...] + _dot(_dot(xv, v1_ref[...]).astype(BF16), v2_ref[...]))
        v = v + (vf_ref[...] - v) * gate_v
    kkr = k * kk_ref[...]
    norm = jnp.sqrt(_head_sum(kkr * kkr, bd_ref))
    kn = kkr / jnp.maximum(norm, 1e-12)
    k2 = k * (1.0 + (alpha - 1.0) * ka_ref[...])
    r_out[...] = r
    lw_out[...] = lw
    k_out[...] = k2
    v_out[...] = v
    kn_out[...] = kn
    b_out[...] = kn * alpha
    g_out[...] = g


def _rwkv_pre(x, xp, vfirst, w, tm):
    n, d = x.shape
    has_vres = vfirst is not None
    row = pl.BlockSpec((tm, d), lambda i: (i, 0))
    ins = [x, xp] + ([vfirst] if has_vres else [])
    specs = [row, row] + ([row] if has_vres else [])
    names = ['mix', 'wr', 'wk', 'wv', 'w1', 'w2', 'w0', 'a1', 'a2', 'a0', 'g1', 'g2']
    if has_vres:
        names += ['v1', 'v2', 'v0']
    names += ['k_k', 'k_a', 'bd']
    for nm in names:
        ins.append(w[nm])
        specs.append(_const_spec(w[nm].shape))
    out = jax.ShapeDtypeStruct((n, d), F32)
    return pl.pallas_call(
        functools.partial(_rwkv_pre_kernel, has_vres),
        grid=(n // tm,),
        in_specs=specs,
        out_specs=[row] * 7,
        out_shape=[out] * 7,
        compiler_params=_params(("parallel",), 56),
        name="rwkv_pre",
    )(*ins)


def _wkv_chunk_kernel(C, n_pairs, r_ref, lw_ref, k_ref, v_ref, kn_ref, b_ref,
                      rp_out, yp_out, m_out, n_out):
    C2 = 2 * C
    row = lax.broadcasted_iota(jnp.int32, (C2, C2), 0)
    col = lax.broadcasted_iota(jnp.int32, (C2, C2), 1)
    same = (row >= C) == (col >= C)
    strict = jnp.logical_and(same, col < row)
    incl = jnp.logical_and(same, col <= row)
    tri_incl = jnp.where(incl, 1.0, 0.0).astype(BF16)
    eye_c = jnp.where(row == col, 1.0, 0.0).astype(F32)
    r128 = lax.broadcasted_iota(jnp.int32, (PAIR, PAIR), 0)
    c128 = lax.broadcasted_iota(jnp.int32, (PAIR, PAIR), 1)
    eye_p = r128 == c128
    head0 = lax.broadcasted_iota(jnp.int32, (1, PAIR), 1) < HEAD_DIM

    def stack(t):
        return jnp.concatenate([jnp.where(head0, t, 0.0), jnp.where(head0, 0.0, t)], axis=0)

    n_sq = int(math.log2(C)) - 1
    sls = [slice(PAIR * p, PAIR * (p + 1)) for p in range(n_pairs)]

    def each(f, *lists):
        return [f(*a) for a in zip(*lists)]

    def load(ref):
        return [stack(ref[:, sl]) for sl in sls]

    lws = load(lw_ref)
    L = each(lambda x: _dot_exact_rhs_left(tri_incl, x), lws)
    lc = each(lambda l: l[C - 1:C, :] + l[C2 - 1:C2, :], L)
    kn, bs, ks = load(kn_ref), load(b_ref), load(k_ref)
    at = each(lambda n, l, w: _split(-n * jnp.exp(l - w)), kn, L, lws)
    rt_f = each(lambda r, l: r * jnp.exp(l), load(r_ref), L)
    rt = each(_split, rt_f)
    e_nl = each(lambda l: jnp.exp(-l), L)
    bb = each(lambda b, e: _split(b * e), bs, e_nl)
    kb = each(lambda k, e: _split(k * e), ks, e_nl)
    e_lc = each(lambda c, l: jnp.exp(c - l), lc, L)
    bh = each(lambda b, e: _split(b * e), bs, e_lc)
    kh = each(lambda k, e: _split(k * e), ks, e_lc)
    vs = each(_split, load(v_ref))

    g_a = each(lambda a, b, k: _dot3_multi(a, [b, k], _dot_nt), at, bb, kb)
    g_r = each(lambda a, b, k: _dot3_multi(a, [b, k], _dot_nt), rt, bb, kb)
    a_ab = [jnp.where(strict, g[0], 0.0) for g in g_a]
    a_ak = [_split(jnp.where(strict, g[1], 0.0)) for g in g_a]
    a_rb = [_split(jnp.where(incl, g[0], 0.0)) for g in g_r]
    a_rk = [_split(jnp.where(incl, g[1], 0.0)) for g in g_r]

    pws = each(lambda a: _split(_dot3(a, a)), each(_split, a_ab))
    tm = each(lambda a: eye_c + a, a_ab)
    for step in range(n_sq):
        if step < n_sq - 1:
            res = each(lambda s, t: _dot3_multi(s, [_split(t), s]), pws, tm)
            tm = [t + r[0] for t, r in zip(tm, res)]
            pws = [_split(r[1]) for r in res]
        else:
            tm = each(lambda t, s: t + _dot3(s, _split(t)), tm, pws)
    tms = each(_split, tm)
    w1 = each(lambda a, v: _split(_dot3(a, v)), a_ak, vs)
    tx = each(lambda t, a, w: _dot3_multi(t, [a, w]), tms, at, w1)
    at2 = [_split(r[0]) for r in tx]
    v2 = [_split(r[1]) for r in tx]
    ax = each(lambda a, x, v: _dot3_multi(a, [x, v]), a_rb, at2, v2)
    r2 = [r + x[0] for r, x in zip(rt_f, ax)]
    y2 = [x[1] + _dot3(c, v) for x, c, v in zip(ax, a_rk, vs)]
    bx = each(lambda b, x, v: _dot3_multi(b, [x, v], _dot_tn), bh, at2, v2)
    mm = [jnp.where(eye_p, jnp.exp(c), 0.0) + x[0] for c, x in zip(lc, bx)]
    nn = [x[1] + _dot3(k, v, _dot_tn) for x, k, v in zip(bx, kh, vs)]
    rp_out[...] = jnp.concatenate([x[:C, :] + x[C:, :] for x in r2], axis=1)
    yp_out[...] = jnp.concatenate([x[:C, :] + x[C:, :] for x in y2], axis=1)
    m_out[0] = jnp.stack(mm, axis=0)
    n_out[0] = jnp.stack(nn, axis=0)


def _dot_exact_rhs_left(m, x):
    hi = x.astype(BF16)
    r1 = x - hi.astype(F32)
    mid = r1.astype(BF16)
    lo = (r1 - mid.astype(F32)).astype(BF16)
    return _dot(m, hi) + (_dot(m, mid) + _dot(m, lo))


def _wkv_chunks(arrs, C, row0, n_chunks, pairs_per_step):
    d = arrs[0].shape[1]
    n_pairs = d // PAIR
    assert row0 % C == 0 and n_pairs % pairs_per_step == 0
    blk0 = row0 // C
    lanes = pairs_per_step * PAIR
    in_spec = pl.BlockSpec((C, lanes), lambda c, q: (blk0 + c, q))
    out_row = pl.BlockSpec((C, lanes), lambda c, q: (c, q))
    out_mat = pl.BlockSpec((1, pairs_per_step, PAIR, PAIR), lambda c, q: (c, q, 0, 0))
    t = n_chunks * C
    return pl.pallas_call(
        functools.partial(_wkv_chunk_kernel, C, pairs_per_step),
        grid=(n_chunks, n_pairs // pairs_per_step),
        in_specs=[in_spec] * 6,
        out_specs=[out_row, out_row, out_mat, out_mat],
        out_shape=[jax.ShapeDtypeStruct((t, d), F32), jax.ShapeDtypeStruct((t, d), F32),
                   jax.ShapeDtypeStruct((n_chunks, n_pairs, PAIR, PAIR), F32),
                   jax.ShapeDtypeStruct((n_chunks, n_pairs, PAIR, PAIR), F32)],
        compiler_params=_params(("parallel", "parallel"), 32),
        name=f"wkv_chunks_c{C}",
    )(*arrs)


def _wkv_seq_kernel(n_pairs, n_steps, rp_ref, yp_ref, m_ref, n_ref, s0_ref, y_out, s_out, s_scr):
    j = pl.program_id(1)

    @pl.when(j == 0)
    def _():
        s_scr[...] = s0_ref[0]

    ys, new_s = [], []
    for p in range(n_pairs):
        sl = slice(PAIR * p, PAIR * (p + 1))
        ss = _split(s_scr[p])
        ys.append(_dot3(_split(rp_ref[:, sl]), ss) + yp_ref[:, sl])
        new_s.append(_dot3(_split(m_ref[0, p]), ss) + n_ref[0, p])
    y_out[...] = jnp.concatenate(ys, axis=1)
    s_scr[...] = jnp.stack(new_s, axis=0)

    @pl.when(j == n_steps - 1)
    def _():
        s_out[0] = s_scr[...]


def _wkv_seq(rp, yp, m, nn, s0, C, n_seq, n_steps):
    t, d = rp.shape
    n_pairs = d // PAIR
    row = pl.BlockSpec((C, d), lambda s, j: (s * n_steps + j, 0))
    mat = pl.BlockSpec((1, n_pairs, PAIR, PAIR), lambda s, j: (s * n_steps + j, 0, 0, 0))
    st = pl.BlockSpec((1, n_pairs, PAIR, PAIR), lambda s, j: (s, 0, 0, 0))
    return pl.pallas_call(
        functools.partial(_wkv_seq_kernel, n_pairs, n_steps),
        grid=(n_seq, n_steps),
        in_specs=[row, row, mat, mat, st],
        out_specs=[row, st],
        out_shape=[jax.ShapeDtypeStruct((t, d), F32),
                   jax.ShapeDtypeStruct((n_seq, n_pairs, PAIR, PAIR), F32)],
        scratch_shapes=[pltpu.VMEM((n_pairs, PAIR, PAIR), F32)],
        compiler_params=_params(("arbitrary", "arbitrary"), 32),
        name=f"wkv_seq_c{C}",
    )(rp, yp, m, nn, s0)


def _state_to_blockdiag(s):
    b, h, n, _ = s.shape
    st = jnp.swapaxes(s, -1, -2).reshape(b, h // 2, 2, n, n)
    z = jnp.zeros_like(st[:, :, 0])
    top = jnp.concatenate([st[:, :, 0], z], axis=-1)
    bot = jnp.concatenate([z, st[:, :, 1]], axis=-1)
    return jnp.concatenate([top, bot], axis=-2)


def _blockdiag_to_state(bd):
    b, hp, _, _ = bd.shape
    n = HEAD_DIM
    s0 = bd[:, :, :n, :n]
    s1 = bd[:, :, n:, n:]
    st = jnp.stack([s0, s1], axis=2).reshape(b, hp * 2, n, n)
    return jnp.swapaxes(st, -1, -2)


def _route(x1, rw_ref, rb_ref):
    logits = _dot3(_split(rw_ref[...]), _split(x1), _dot_nt)
    mx = jnp.max(logits, axis=0, keepdims=True)
    ex = jnp.exp(logits - mx)
    scores = ex / jnp.sum(ex, axis=0, keepdims=True)
    sel = scores + rb_ref[...]
    rows = [sel[e:e + 1, :] for e in range(N_EXPERTS)]
    srow = [scores[e:e + 1, :] for e in range(N_EXPERTS)]

    def top2(vals):
        m1 = jnp.maximum(jnp.maximum(vals[0], vals[1]), jnp.maximum(vals[2], vals[3]))
        i1 = jnp.where(vals[0] == m1, 0, jnp.where(vals[1] == m1, 1, jnp.where(vals[2] == m1, 2, 3)))
        rest = [jnp.where(i1 == j, -jnp.inf, vals[j]) for j in range(4)]
        m2 = jnp.maximum(jnp.maximum(rest[0], rest[1]), jnp.maximum(rest[2], rest[3]))
        i2 = jnp.where(rest[0] == m2, 0, jnp.where(rest[1] == m2, 1, jnp.where(rest[2] == m2, 2, 3)))
        return m1, i1, m2, i2

    gscore = []
    for gidx in range(N_GROUPS):
        m1, _, m2, _ = top2(rows[4 * gidx:4 * gidx + 4])
        gscore.append(m1 + m2)
    gm = jnp.maximum(jnp.maximum(gscore[0], gscore[1]), jnp.maximum(gscore[2], gscore[3]))
    gi = jnp.where(gscore[0] == gm, 0, jnp.where(gscore[1] == gm, 1, jnp.where(gscore[2] == gm, 2, 3)))

    def pick(rws, j):
        return jnp.where(gi == 0, rws[j], jnp.where(gi == 1, rws[4 + j],
                                                    jnp.where(gi == 2, rws[8 + j], rws[12 + j])))

    in_grp = [pick(rows, j) for j in range(4)]
    sc_grp = [pick(srow, j) for j in range(4)]
    _, i1, _, i2 = top2(in_grp)

    def at(vals, idx):
        return jnp.where(idx == 0, vals[0], jnp.where(idx == 1, vals[1],
                                                      jnp.where(idx == 2, vals[2], vals[3])))

    ga = at(sc_grp, i1)
    gb = at(sc_grp, i2)
    tot = ga + gb
    ga = ga / tot
    gb = gb / tot
    lo = jnp.minimum(i1, i2)
    hi = jnp.maximum(i1, i2)
    g_lo = jnp.where(i1 < i2, ga, gb)
    g_hi = jnp.where(i1 < i2, gb, ga)
    pair = jnp.where(lo == 0, hi - 1, jnp.where(lo == 1, hi + 1, 5))
    cls = gi * 6 + pair
    return cls.astype(jnp.int32), g_lo, g_hi


def _mix_post_kernel(is_rwkv, alpha_dn, *refs):
    if is_rwkv:
        (x_ref, y_ref, r_ref, k_ref, v_ref, g_ref, lg_ref, lb_ref, rk_ref, bd_ref,
         wo_ref, n1g_ref, n1b_ref, rw_ref, rb_ref, x1_out, cls_out) = refs
        y = y_ref[...]
        mu = _head_sum(y, bd_ref) * (1.0 / HEAD_DIM)
        yc = y - mu
        var = _head_sum(yc * yc, bd_ref) * (1.0 / HEAD_DIM)
        yn = yc * lax.rsqrt(var + GN_EPS) * lg_ref[...] + lb_ref[...]
        v = v_ref[...]
        bonus = _head_sum(r_ref[...] * k_ref[...] * rk_ref[...], bd_ref)
        z = (yn + bonus * v) * g_ref[...]
    else:
        (x_ref, o_ref, gt_ref, wo_ref, n1g_ref, n1b_ref, rw_ref, rb_ref,
         x1_out, cls_out) = refs
        z = o_ref[...] * gt_ref[...]
    h = _dot(z.astype(BF16), wo_ref[...])
    x1 = _layer_norm(alpha_dn * x_ref[...] + h, n1g_ref[...], n1b_ref[...])
    cls, g_lo, g_hi = _route(x1, rw_ref, rb_ref)
    tm, d = x1.shape
    cls_out[...] = jnp.broadcast_to(cls, (8, tm))
    pieces = [x.astype(F32) for x in _split3(g_lo) + _split3(g_hi)]
    gs = jnp.concatenate(pieces + [jnp.zeros((16 - len(pieces), tm), F32)], axis=0).astype(BF16)
    rr = lax.broadcasted_iota(jnp.int32, (16, GATE_LANES), 0)
    cc = lax.broadcasted_iota(jnp.int32, (16, GATE_LANES), 1)
    place = jnp.where(jnp.logical_and(rr < 6, cc == jnp.where(rr < 3, 0, 1)), 1.0, 0.0).astype(BF16)
    x1_out[:, :d] = x1
    x1_out[:, d:] = _dot_tn(gs, place)


def _mix_post(is_rwkv, alpha_dn, acts, consts, tm):
    n, d = acts[0].shape
    row = pl.BlockSpec((tm, d), lambda i: (i, 0))
    specs = [row] * len(acts) + [_const_spec(c.shape) for c in consts]
    lane = pl.BlockSpec((8, tm), lambda i: (0, i))
    wide = pl.BlockSpec((tm, d + GATE_LANES), lambda i: (i, 0))
    return pl.pallas_call(
        functools.partial(_mix_post_kernel, is_rwkv, alpha_dn),
        grid=(n // tm,),
        in_specs=specs,
        out_specs=[wide, lane],
        out_shape=[jax.ShapeDtypeStruct((n, d + GATE_LANES), F32), jax.ShapeDtypeStruct((8, n), jnp.int32)],
        compiler_params=_params(("parallel",), 48),
        name="mix_post_rwkv" if is_rwkv else "mix_post_fox",
    )(*acts, *consts)


def _moe_kernel(n_tok, d, d_exp, n_blk, e1_ref, e2_ref, off_ref, nv_ref, order_ref, x_hbm,
                wgu1_ref, wd1_ref, wgu2_ref, wd2_ref, out_hbm, xbuf, obuf, gsem, ssem):
    b = pl.program_id(0)
    slot = lax.rem(b, 2)
    nv_b = nv_ref[b]
    unroll = 8

    def gather_start(blk, sl):
        base = off_ref[blk]

        def body(c, carry):
            for u in range(unroll):
                i = c * unroll + u
                tok = order_ref[jnp.minimum(base + i, n_tok - 1)]
                pltpu.make_async_copy(x_hbm.at[pl.ds(tok, 1)], xbuf.at[sl, pl.ds(i, 1)], gsem.at[sl]).start()
            return carry

        lax.fori_loop(0, E_BLOCK // unroll, body, 0)

    def gather_wait(sl):
        pltpu.make_async_copy(x_hbm.at[pl.ds(0, E_BLOCK)], xbuf.at[sl], gsem.at[sl]).wait()

    def scatter_start(blk, sl):
        base = off_ref[blk]

        def body(i, carry):
            tok = order_ref[base + i]
            pltpu.make_async_copy(obuf.at[sl, pl.ds(i, 1)], out_hbm.at[pl.ds(tok, 1)], ssem.at[sl]).start()
            return carry

        lax.fori_loop(0, nv_ref[blk], body, 0)

    def scatter_wait(blk, sl):
        nv = nv_ref[blk]

        @pl.when(nv == E_BLOCK)
        def _():
            pltpu.make_async_copy(obuf.at[sl], out_hbm.at[pl.ds(0, E_BLOCK)], ssem.at[sl]).wait()

        @pl.when(nv < E_BLOCK)
        def _():
            def body(i, carry):
                pltpu.make_async_copy(obuf.at[sl, pl.ds(0, 1)], out_hbm.at[pl.ds(0, 1)], ssem.at[sl]).wait()
                return carry

            lax.fori_loop(0, nv, body, 0)

    @pl.when(jnp.logical_and(b == 0, nv_b > 0))
    def _():
        gather_start(0, 0)

    nxt = jnp.minimum(b + 1, n_blk - 1)

    @pl.when(jnp.logical_and(b + 1 < n_blk, nv_ref[nxt] > 0))
    def _():
        gather_start(nxt, 1 - slot)

    @pl.when(nv_b > 0)
    def _():
        gather_wait(slot)
        rows = xbuf[slot]
        xb = rows[:, :d].astype(BF16)

        def expert(wgu_ref, wd_ref):
            gu = _dot(xb, wgu_ref[0])
            hmid = jax.nn.silu(gu[:, :d_exp]) * gu[:, d_exp:]
            return _dot(hmid.astype(BF16), wd_ref[0])

        y1 = expert(wgu1_ref, wd1_ref)
        y2 = expert(wgu2_ref, wd2_ref)
        obuf[slot] = y1 * rows[:, d:d + 1] + y2 * rows[:, d + 1:d + 2]

    prev = jnp.maximum(b - 1, 0)

    @pl.when(jnp.logical_and(b > 0, nv_ref[prev] > 0))
    def _():
        scatter_wait(prev, 1 - slot)

    @pl.when(nv_b > 0)
    def _():
        scatter_start(b, slot)

    @pl.when(jnp.logical_and(b == n_blk - 1, nv_b > 0))
    def _():
        scatter_wait(b, slot)


def _moe(xg, cls, w_gu, w_down):
    n = xg.shape[0]
    d, d_exp = w_down.shape[2], w_down.shape[1]
    n_blk = (n + N_CLASSES * (E_BLOCK - 1) + E_BLOCK - 1) // E_BLOCK
    counts = jnp.bincount(cls, length=N_CLASSES).astype(jnp.int32)
    padded = (counts + E_BLOCK - 1) // E_BLOCK * E_BLOCK
    pad_end = jnp.cumsum(padded)
    pad_start = pad_end - padded
    seg_start = jnp.cumsum(counts) - counts
    order = jnp.argsort(cls).astype(jnp.int32)
    blk_start = jnp.arange(n_blk, dtype=jnp.int32) * E_BLOCK
    blk_cls = jnp.minimum(jnp.searchsorted(pad_end, blk_start, side='right'), N_CLASSES - 1).astype(jnp.int32)
    rank0 = blk_start - pad_start[blk_cls]
    blk_nv = jnp.clip(counts[blk_cls] - rank0, 0, E_BLOCK).astype(jnp.int32)
    blk_off = (seg_start[blk_cls] + rank0).astype(jnp.int32)
    n_used = jnp.sum((blk_nv > 0).astype(jnp.int32))
    blk_cls = jnp.where(blk_nv > 0, blk_cls, blk_cls[jnp.maximum(n_used - 1, 0)])
    pair_lo = jnp.asarray(_PAIR_LO, jnp.int32)
    pair_hi = jnp.asarray(_PAIR_HI, jnp.int32)
    blk_e1 = (blk_cls // 6) * EXPERTS_PER_GROUP + pair_lo[blk_cls % 6]
    blk_e2 = (blk_cls // 6) * EXPERTS_PER_GROUP + pair_hi[blk_cls % 6]

    gu1 = pl.BlockSpec((1, d, 2 * d_exp), lambda b, e1, e2, of, nv, od: (e1[b], 0, 0))
    dn1 = pl.BlockSpec((1, d_exp, d), lambda b, e1, e2, of, nv, od: (e1[b], 0, 0))
    gu2 = pl.BlockSpec((1, d, 2 * d_exp), lambda b, e1, e2, of, nv, od: (e2[b], 0, 0))
    dn2 = pl.BlockSpec((1, d_exp, d), lambda b, e1, e2, of, nv, od: (e2[b], 0, 0))
    grid_spec = pltpu.PrefetchScalarGridSpec(
        num_scalar_prefetch=5,
        grid=(n_blk,),
        in_specs=[pl.BlockSpec(memory_space=pl.ANY), gu1, dn1, gu2, dn2],
        out_specs=pl.BlockSpec(memory_space=pl.ANY),
        scratch_shapes=[pltpu.VMEM((2, E_BLOCK, d + GATE_LANES), F32), pltpu.VMEM((2, E_BLOCK, d), F32),
                        pltpu.SemaphoreType.DMA((2,)), pltpu.SemaphoreType.DMA((2,))],
    )
    return pl.pallas_call(
        functools.partial(_moe_kernel, n, d, d_exp, n_blk),
        grid_spec=grid_spec,
        out_shape=jax.ShapeDtypeStruct((n, d), F32),
        compiler_params=_params(("arbitrary",), 40),
        name="moe_experts",
    )(blk_e1, blk_e2, blk_off, blk_nv, order, xg, w_gu, w_down, w_gu, w_down)


def _bias_placement(nh, lead_ones):
    p = np.zeros((4 * nh, nh * PAIR), np.float32)
    for h in range(nh):
        lane0 = PAIR * h + HEAD_DIM
        if lead_ones:
            p[h, lane0:lane0 + 3] = 1.0
            for piece in range(3):
                p[(piece + 1) * nh + h, lane0 + 3 + piece] = 1.0
        else:
            for piece in range(3):
                p[piece * nh + h, lane0 + piece] = 1.0
            p[3 * nh + h, lane0 + 3:lane0 + 6] = 1.0
    return jnp.asarray(p, BF16)


def _pad_heads(w, width):
    d, hd = w.shape
    nh = hd // HEAD_DIM
    w3 = w.reshape(d, nh, HEAD_DIM)
    return jnp.concatenate([w3, jnp.zeros((d, nh, width - HEAD_DIM), w.dtype)], axis=-1).reshape(d, nh * width)


def _bias_rows(rem, lead_ones):
    parts = [x.astype(F32) for x in _split3(rem)]
    ones = [jnp.ones_like(rem)]
    return jnp.concatenate(ones + parts if lead_ones else parts + ones, axis=0).astype(BF16)


def _ln2_kernel(alpha_dn, with_kv, tiles_per_blk, *refs):
    if with_kv:
        (x1_ref, m_ref, g_ref, b_ref, wk_ref, wv_ref, wka_ref, wva_ref, wf_ref, bf_ref, tri_ref,
         pk_ref, eye_ref, ones_ref,
         x2_out, k_out, v_out, ka_out, vt_out, lf_out, cum_out, rem_out, carry, base_scr) = refs
    else:
        x1_ref, m_ref, g_ref, b_ref, x2_out = refs
    x2 = _layer_norm(alpha_dn * x1_ref[...] + m_ref[...], g_ref[...], b_ref[...])
    x2_out[...] = x2
    if with_kv:
        i = pl.program_id(0)
        xb = x2.astype(BF16)
        k_out[...] = _dot(xb, wk_ref[...])
        v_out[...] = _dot(xb, wv_ref[...])
        z = _dot3(_split(wf_ref[...]), _split(x2), _dot_nt) + bf_ref[...]
        lf = jnp.minimum(z, 0.0) - jnp.log1p(jnp.exp(-jnp.abs(z)))
        lf_out[...] = lf

        @pl.when(i == 0)
        def _():
            carry[...] = jnp.zeros_like(carry)

        cum = _dot_exact_rhs(lf, tri_ref[...]) + carry[...]
        cum_out[...] = cum
        carry[...] = cum[:, -1:]

        cum2 = cum * LOG2E

        @pl.when(lax.rem(i, tiles_per_blk) == 0)
        def _():
            base_scr[...] = cum2[:, :1]

        rem = cum2 - base_scr[...]
        rem_out[...] = rem
        ka_out[...] = (_dot(xb, wka_ref[...]) + _dot_tn(_bias_rows(-rem, False), pk_ref[...])).astype(BF16)
        vw = _dot(xb, wva_ref[...]).astype(BF16)
        vt_out[...] = (_dot_tn(vw, eye_ref[...]) + ones_ref[...]).astype(BF16)


def _ln2(alpha_dn, x1, m, g, b, kv, tm):
    n, d = m.shape
    row = pl.BlockSpec((tm, d), lambda i: (i, 0))
    with_kv = kv is not None
    ins = [x1, m, g, b]
    specs = [row, row, _const_spec(g.shape), _const_spec(b.shape)]
    outs = [jax.ShapeDtypeStruct((n, d), F32)]
    ospecs = [row]
    scratch = []
    if with_kv:
        nh = kv['wf_t'].shape[0]
        tri = (jnp.arange(tm)[:, None] <= jnp.arange(tm)[None, :]).astype(BF16)
        eye = jnp.eye(tm, dtype=BF16)
        ones_col = (jnp.arange(nh * V_ROWS) % V_ROWS == HEAD_DIM).astype(F32).reshape(-1, 1)
        extra = [kv['wk'], kv['wv'], kv['wk_aug'], kv['wv_aug'], kv['wf_t'], kv['bf'], tri,
                 _bias_placement(nh, False), eye, ones_col]
        ins += extra
        specs += [_const_spec(e.shape) for e in extra]
        lane = pl.BlockSpec((nh, tm), lambda i: (0, i))
        outs += [jax.ShapeDtypeStruct((n, d), F32), jax.ShapeDtypeStruct((n, d), F32),
                 jax.ShapeDtypeStruct((n, nh * PAIR), BF16), jax.ShapeDtypeStruct((nh * V_ROWS, n), BF16),
                 jax.ShapeDtypeStruct((nh, n), F32), jax.ShapeDtypeStruct((nh, n), F32),
                 jax.ShapeDtypeStruct((nh, n), F32)]
        ospecs += [row, row, pl.BlockSpec((tm, nh * PAIR), lambda i: (i, 0)),
                   pl.BlockSpec((nh * V_ROWS, tm), lambda i: (0, i)), lane, lane, lane]
        scratch = [pltpu.VMEM((nh, 1), F32), pltpu.VMEM((nh, 1), F32)]
    res = pl.pallas_call(
        functools.partial(_ln2_kernel, alpha_dn, with_kv, kv['tiles_per_blk'] if with_kv else 1),
        grid=(n // tm,),
        in_specs=specs,
        out_specs=ospecs,
        out_shape=outs,
        scratch_shapes=scratch,
        compiler_params=_params(("arbitrary",), 48),
        name="ln2_kv" if with_kv else "ln2",
    )(*ins)
    return res


def _fox_qg_kernel(x_ref, rem_ref, wq_ref, wqa_ref, wg_ref, pq_ref, q_out, qa_out, gate_out):
    xb = x_ref[...].astype(BF16)
    scale = LOG2E * HEAD_DIM ** -0.5
    q_out[...] = (_dot(xb, wq_ref[...]) * scale).astype(BF16)
    qa_out[...] = (_dot(xb, wqa_ref[...]) * scale
                   + _dot_tn(_bias_rows(rem_ref[...], True), pq_ref[...])).astype(BF16)
    gate_out[...] = jax.nn.sigmoid(_dot(xb, wg_ref[...]))


def _fox_qg(x, rem_r, wq, wq_aug, wg, tm):
    n, d = x.shape
    nh = rem_r.shape[0]
    row = pl.BlockSpec((tm, d), lambda i: (i, 0))
    pq = _bias_placement(nh, True)
    return pl.pallas_call(
        _fox_qg_kernel,
        grid=(n // tm,),
        in_specs=[row, pl.BlockSpec((nh, tm), lambda i: (0, i)), _const_spec(wq.shape),
                  _const_spec(wq_aug.shape), _const_spec(wg.shape), _const_spec(pq.shape)],
        out_specs=[row, pl.BlockSpec((tm, nh * PAIR), lambda i: (i, 0)), row],
        out_shape=[jax.ShapeDtypeStruct((n, d), BF16), jax.ShapeDtypeStruct((n, nh * PAIR), BF16),
                   jax.ShapeDtypeStruct((n, d), F32)],
        compiler_params=_params(("parallel",), 48),
        name="fox_qg",
    )(x, rem_r, wq, wq_aug, wg, pq)


V_ROWS = 80


def _fox_prompt_kernel(tq, hps, nh, base_ref, jlo_ref, q_ref, k_ref, vt_ref, o_out, s_scr, p_scr):
    g = pl.program_id(0)
    i = pl.program_id(1)
    j_lo = jlo_ref[g * pl.num_programs(1) + i]
    rr = lax.broadcasted_iota(jnp.int32, (tq, tq), 0)
    cc = lax.broadcasted_iota(jnp.int32, (tq, tq), 1)
    causal = rr <= cc

    def scores(j, h):
        k0 = pl.multiple_of(j * tq, tq)
        lanes = slice(PAIR * h, PAIR * (h + 1))
        return _dot_nt(k_ref[pl.ds(k0, tq), lanes], q_ref[:, lanes])

    def base_gap(j, h):
        head = g * hps + h
        return base_ref[i * nh + head] - base_ref[j * nh + head]

    def softmax(s, m, gap):
        m_new = jnp.maximum(m, jnp.max(s, axis=0, keepdims=True) + gap)
        return jnp.exp2(s - (m_new - gap)).astype(BF16), jnp.exp2(m - m_new), m_new

    def accum(acc, corr, p, j, h):
        k0 = pl.multiple_of(j * tq, tq)
        return acc * corr + _dot(vt_ref[h, :, pl.ds(k0, tq)], p)

    init = []
    for h in range(hps):
        s = jnp.where(causal, scores(i, h), NEG_INF)
        m0 = jnp.max(s, axis=0, keepdims=True)
        p_scr[h] = jnp.exp2(s - m0).astype(BF16)
        s_scr[h] = scores(jnp.maximum(i - 1, 0), h)
        init.append((m0, jnp.zeros((V_ROWS, tq), F32), jnp.ones((1, tq), F32)))

    def body(t, carry):
        j = i - 1 - t
        new = []
        for h in range(hps):
            m, acc, corr_prev = carry[h]
            acc = accum(acc, corr_prev, p_scr[h], j + 1, h)
            p, corr, m = softmax(s_scr[h], m, base_gap(j, h))
            p_scr[h] = p
            s_scr[h] = scores(jnp.maximum(j - 1, 0), h)
            new.append((m, acc, corr))
        return tuple(new)

    carry = lax.fori_loop(0, i - j_lo, body, tuple(init))
    outs = []
    for h in range(hps):
        m, acc, corr_prev = carry[h]
        acc = accum(acc, corr_prev, p_scr[h], j_lo, h)
        outs.append(acc[:HEAD_DIM, :] / acc[HEAD_DIM:HEAD_DIM + 1, :])
    o_out[...] = jnp.concatenate(outs, axis=0)


UNDERFLOW_LOG2 = -160.0


def _oldest_needed_block(qb, k, rem_r, base, t, tq, hps):
    d = qb.shape[1]
    nh = d // HEAD_DIM
    nb = t // tq
    slack = 2.0
    q = qb[:t].astype(F32).reshape(nb, tq, nh, HEAD_DIM)
    kk = k[:t].astype(BF16).astype(F32).reshape(nb, tq, nh, HEAD_DIM)
    rem = rem_r[:, :t].T.reshape(nb, tq, nh)
    q_norm = jnp.sqrt(jnp.max(jnp.sum(q * q, axis=-1), axis=1))
    k_norm = jnp.sqrt(jnp.max(jnp.sum(kk * kk, axis=-1), axis=(0, 1)))
    self_min = jnp.min(jnp.sum(q * kk, axis=-1), axis=1)
    s_max = q_norm * k_norm + jnp.max(rem, axis=1) + jnp.max(-rem, axis=(0, 1)) + slack
    m_low = self_min - slack
    b = base.reshape(nb, nh)
    gap = b[:, None, :] - b[None, :, :]
    older = jnp.arange(nb)[None, :, None] < jnp.arange(nb)[:, None, None]
    alive = jnp.logical_and(older, gap + (s_max - m_low)[:, None, :] >= UNDERFLOW_LOG2)
    idx = jnp.arange(nb, dtype=jnp.int32)
    j_lo = jnp.min(jnp.where(alive, idx[None, :, None], idx[:, None, None]), axis=1)
    return jnp.min(j_lo.reshape(nb, nh // hps, hps), axis=-1).T.reshape(-1).astype(jnp.int32)


def _fox_prompt(base, j_lo, q_aug, k_aug, vt_aug, t, tq, hps):
    nh = vt_aug.shape[0]
    return pl.pallas_call(
        functools.partial(_fox_prompt_kernel, tq, hps, nh),
        grid=(nh // hps, t // tq),
        in_specs=[pl.BlockSpec(memory_space=pltpu.SMEM), pl.BlockSpec(memory_space=pltpu.SMEM),
                  pl.BlockSpec((tq, hps * PAIR), lambda g, i: (i, g)),
                  pl.BlockSpec((t, hps * PAIR), lambda g, i: (0, g), pipeline_mode=pl.Buffered(1)),
                  pl.BlockSpec((hps, V_ROWS, t), lambda g, i: (g, 0, 0), pipeline_mode=pl.Buffered(1))],
        out_specs=pl.BlockSpec((hps * HEAD_DIM, tq), lambda g, i: (g, i)),
        out_shape=jax.ShapeDtypeStruct((nh * HEAD_DIM, t), F32),
        scratch_shapes=[pltpu.VMEM((hps, tq, tq), F32), pltpu.VMEM((hps, tq, tq), BF16)],
        compiler_params=_params(("parallel", "parallel"), 48),
        name="fox_prompt_attn",
    )(base, j_lo, q_aug, k_aug, vt_aug)


def _fox_sample_kernel(n_pairs, q_ref, kn_ref, vn_ref, lfn_ref, kc_ref, vc_ref, lfc_ref,
                       triu_ref, o_out):
    t = q_ref.shape[0]
    past = kc_ref.shape[1]
    lfc = lfc_ref[0]
    lfn = lfn_ref[0]
    nh = lfc.shape[1]
    lfc_s = _split3(lfc)
    cum_c_r = _dot_tn3(lfc_s, triu_ref[...])
    ones_row = jnp.ones((1, past), BF16)
    tot_r = _dot_m3(ones_row, lfc_s)
    tot_c = cum_c_r[:, past - 1:past]
    rr = lax.broadcasted_iota(jnp.int32, (t, t), 0)
    cc = lax.broadcasted_iota(jnp.int32, (t, t), 1)
    causal = rr >= cc
    tril = jnp.where(causal, 1.0, 0.0).astype(BF16)
    lfn_s = _split3(lfn)
    cum_n_c = (_dot_m3(tril, lfn_s) + tot_r) * LOG2E
    cum_n_r = (_dot_tn3(lfn_s, jnp.where(rr <= cc, 1.0, 0.0).astype(BF16)) + tot_c) * LOG2E
    cum_c_r = cum_c_r * LOG2E
    lane = lax.broadcasted_iota(jnp.int32, (1, PAIR), 1)
    head0 = lane < HEAD_DIM
    hl = lax.broadcasted_iota(jnp.int32, (1, nh), 1)
    for p in range(n_pairs):
        sl = slice(PAIR * p, PAIR * (p + 1))
        q = q_ref[:, sl]
        zero = jnp.zeros_like(q)
        kc = kc_ref[0, :, sl].astype(BF16)
        vc = vc_ref[0, :, sl].astype(BF16)
        kn = kn_ref[:, sl].astype(BF16)
        vn = vn_ref[:, sl].astype(BF16)
        outs = []
        for h in range(2):
            hh = 2 * p + h
            qh = jnp.where(head0, q, zero) if h == 0 else jnp.where(head0, zero, q)
            cq = jnp.sum(jnp.where(hl == hh, cum_n_c, 0.0), axis=1, keepdims=True)
            s_c = _dot_nt(qh, kc) + cq - cum_c_r[hh:hh + 1, :]
            s_n = _dot_nt(qh, kn) + cq - cum_n_r[hh:hh + 1, :]
            s_n = jnp.where(causal, s_n, NEG_INF)
            m = jnp.maximum(jnp.max(s_c, axis=1, keepdims=True), jnp.max(s_n, axis=1, keepdims=True))
            p_c = jnp.exp2(s_c - m)
            p_n = jnp.exp2(s_n - m)
            l = jnp.sum(p_c, axis=1, keepdims=True) + jnp.sum(p_n, axis=1, keepdims=True)
            acc = _dot(p_c.astype(BF16), vc) + _dot(p_n.astype(BF16), vn)
            outs.append(acc / l)
        o_out[:, sl] = jnp.where(head0, outs[0], outs[1])


def _split3(x):
    hi = x.astype(BF16)
    r1 = x - hi.astype(F32)
    mid = r1.astype(BF16)
    lo = (r1 - mid.astype(F32)).astype(BF16)
    return hi, mid, lo


def _dot_tn3(xs, m):
    return _dot_tn(xs[0], m) + (_dot_tn(xs[1], m) + _dot_tn(xs[2], m))


def _dot_m3(m, xs):
    return _dot(m, xs[0]) + (_dot(m, xs[1]) + _dot(m, xs[2]))


def _fox_sample(qb, k, v, lf_new, cache_k, cache_v, cache_lf, row0, n_stream, t):
    d = qb.shape[1]
    n_pairs = d // PAIR
    past = cache_k.shape[1]
    nh = cache_lf.shape[2]
    blk0 = row0 // t
    triu = (jnp.arange(past)[:, None] <= jnp.arange(past)[None, :]).astype(BF16)
    row = pl.BlockSpec((t, d), lambda b: (blk0 + b, 0))
    return pl.pallas_call(
        functools.partial(_fox_sample_kernel, n_pairs),
        grid=(n_stream,),
        in_specs=[row, row, row,
                  pl.BlockSpec((1, t, nh), lambda b: (b, 0, 0)),
                  pl.BlockSpec((1, past, d), lambda b: (b, 0, 0)),
                  pl.BlockSpec((1, past, d), lambda b: (b, 0, 0)),
                  pl.BlockSpec((1, past, nh), lambda b: (b, 0, 0)),
                  _const_spec(triu.shape)],
        out_specs=pl.BlockSpec((t, d), lambda b: (b, 0)),
        out_shape=jax.ShapeDtypeStruct((n_stream * t, d), F32),
        compiler_params=_params(("parallel",), 48),
        name="fox_sample_attn",
    )(qb, k, v, lf_new, cache_k, cache_v, cache_lf, triu)


def kernel(x_prompt, x_sample, state_shift, state_wkv, cache_k, cache_v, cache_logf, ln1_g, ln1_b, ln2_g, ln2_b, a_mix, a_w_rkv, a_w0, a_w1, a_w2, a_a0, a_a1, a_a2, a_v0, a_v1, a_v2, a_g1, a_g2, a_k_k, a_k_a, a_r_k, a_lnx_g, a_lnx_b, a_w_o, kv_w, kv_bf, b_w_qg, b_w_o, router_w, router_b, moe_w_gu, moe_w_down):
    nb, seq, d = x_prompt.shape
    db, dt, _ = x_sample.shape
    assert nb == 1
    n_heads = d // HEAD_DIM
    depth = ln1_g.shape[0]
    n_a = a_mix.shape[0]
    past = cache_k.shape[1]
    t_p = nb * seq
    t_s = db * dt
    n = t_p + t_s
    tm = _row_tile(n)
    c_p = 64
    c_s = dt
    tq_attn = 512 if t_p % 512 == 0 else 256
    assert seq % c_p == 0 and t_p % c_s == 0 and (c_s & (c_s - 1)) == 0
    alpha_dn = (2 * depth) ** 0.25
    row2 = lambda a: a.reshape(1, -1)

    bd = (jnp.arange(256)[:, None] // HEAD_DIM == jnp.arange(256)[None, :] // HEAD_DIM).astype(BF16)
    rw_t = router_w.T
    rb_c = router_b.reshape(-1, 1)

    x = jnp.concatenate([x_prompt.reshape(t_p, d), x_sample.reshape(t_s, d)], axis=0)
    new_shift_p, new_shift_s, new_wkv_p, new_wkv_s = [], [], [], []
    v_first = None
    kv = None
    for l in range(depth):
        if l < n_a:
            xs = x[t_p:].reshape(db, dt, d)
            new_shift_p.append(x[t_p - 1:t_p].reshape(nb, d))
            new_shift_s.append(xs[:, -1])
            xp = jnp.concatenate([
                jnp.zeros((1, d), F32), x[:t_p - 1],
                jnp.concatenate([state_shift[l][:, None, :], xs[:, :-1]], axis=1).reshape(t_s, d)], axis=0)
            w = dict(mix=jnp.concatenate([a_mix[l], jnp.zeros((2, d), F32)], axis=0),
                     wr=a_w_rkv[l, 0].astype(BF16), wk=a_w_rkv[l, 1].astype(BF16), wv=a_w_rkv[l, 2].astype(BF16),
                     w1=a_w1[l].astype(BF16), w2=a_w2[l].astype(BF16), w0=row2(a_w0[l]),
                     a1=a_a1[l].astype(BF16), a2=a_a2[l].astype(BF16), a0=row2(a_a0[l]),
                     g1=a_g1[l].astype(BF16), g2=a_g2[l].astype(BF16),
                     k_k=row2(a_k_k[l]), k_a=row2(a_k_a[l]), bd=bd)
            if l > 0:
                w.update(v1=a_v1[l - 1].astype(BF16), v2=a_v2[l - 1].astype(BF16), v0=row2(a_v0[l - 1]))
            r, lw, k, v, kn, b, g = _rwkv_pre(x, xp, v_first if l > 0 else None, w, tm)
            if l == 0:
                v_first = v
            scan_in = (r, lw, k, v, kn, b)
            rp, yp, mm, nn = _wkv_chunks(scan_in, c_p, 0, t_p // c_p, n_heads // 2)
            s0 = jnp.zeros((1, n_heads // 2, PAIR, PAIR), F32)
            y_p, sf_p = _wkv_seq(rp, yp, mm, nn, s0, c_p, 1, t_p // c_p)
            rp, yp, mm, nn = _wkv_chunks(scan_in, c_s, t_p, db, n_heads // 2)
            y_s, sf_s = _wkv_seq(rp, yp, mm, nn, _state_to_blockdiag(state_wkv[l]), c_s, db, 1)
            new_wkv_p.append(_blockdiag_to_state(sf_p))
            new_wkv_s.append(_blockdiag_to_state(sf_s))
            y = jnp.concatenate([y_p, y_s], axis=0)
            consts = [row2(a_lnx_g[l]), row2(a_lnx_b[l]), row2(a_r_k[l]), bd, a_w_o[l].astype(BF16),
                      row2(ln1_g[l]), row2(ln1_b[l]), rw_t, rb_c]
            x1, cls = _mix_post(True, alpha_dn, [x, y, r, k, v, g], consts, tm)
        else:
            lb = l - n_a
            wq = b_w_qg[lb][:, :d].astype(BF16)
            qb, q_aug, gate = _fox_qg(x, kv['rem_r'], wq, _pad_heads(wq, PAIR), b_w_qg[lb][:, d:].astype(BF16), tm)
            j_lo = _oldest_needed_block(qb, kv['k'], kv['rem_r'], kv['base'], t_p, tq_attn, 4)
            o_p = _fox_prompt(kv['base'], j_lo, q_aug, kv['k_aug'], kv['vt_aug'], t_p, tq_attn, 4).T
            o_s = _fox_sample(qb, kv['k'], kv['v'], kv['lf_s'], cache_k.reshape(db, past, d),
                              cache_v.reshape(db, past, d), cache_logf, t_p, db, dt)
            o = jnp.concatenate([o_p, o_s], axis=0)
            consts = [b_w_o[lb].astype(BF16), row2(ln1_g[l]), row2(ln1_b[l]), rw_t, rb_c]
            x1, cls = _mix_post(False, alpha_dn, [x, o, gate], consts, tm)
        m = _moe(x1, cls[0], moe_w_gu[l].astype(BF16), moe_w_down[l].astype(BF16))
        if l == n_a - 1:
            wk, wv = kv_w[:, :d].astype(BF16), kv_w[:, d:2 * d].astype(BF16)
            kvw = dict(wk=wk, wv=wv, wk_aug=_pad_heads(wk, PAIR), wv_aug=_pad_heads(wv, V_ROWS),
                       wf_t=kv_w[:, 2 * d:].T, bf=kv_bf.reshape(-1, 1), tiles_per_blk=tq_attn // tm)
            x, k_all, v_all, k_aug, vt_aug, lf_r, cum_r, rem_r = _ln2(
                alpha_dn, x1, m, row2(ln2_g[l]), row2(ln2_b[l]), kvw, tm)
            base = (cum_r[:, :t_p:tq_attn] * LOG2E).T.reshape(-1)
            kv = dict(k=k_all, v=v_all, base=base, rem_r=rem_r, k_aug=k_aug,
                      vt_aug=vt_aug.reshape(n_heads, V_ROWS, n), lf_p=lf_r[:, :t_p].T,
                      lf_s=lf_r[:, t_p:].T.reshape(db, dt, n_heads))
        else:
            x = _ln2(alpha_dn, x1, m, row2(ln2_g[l]), row2(ln2_b[l]), None, tm)[0]

    y_prompt = x[:t_p].reshape(nb, seq, d)
    y_sample = x[t_p:].reshape(db, dt, d)
    p_k = kv['k'][:t_p].reshape(nb, seq, n_heads, HEAD_DIM)
    p_v = kv['v'][:t_p].reshape(nb, seq, n_heads, HEAD_DIM)
    s_k = kv['k'][t_p:].reshape(db, dt, n_heads, HEAD_DIM)
    s_v = kv['v'][t_p:].reshape(db, dt, n_heads, HEAD_DIM)
    return (y_prompt, y_sample, jnp.stack(new_shift_p), jnp.stack(new_wkv_p), p_k, p_v,
            kv['lf_p'].reshape(nb, seq, n_heads), jnp.stack(new_shift_s), jnp.stack(new_wkv_s),
            s_k, s_v, kv['lf_s'])
```

```python
import functools
import math

import jax
import jax.numpy as jnp
import numpy as np
from jax import lax
from jax.experimental import pallas as pl
from jax.experimental.pallas import tpu as pltpu

F32 = jnp.float32
BF16 = jnp.bfloat16

HEAD_DIM = 64
PAIR = 2 * HEAD_DIM
N_EXPERTS = 16
N_GROUPS = 4
EXPERTS_PER_GROUP = 4
N_CLASSES = 24
E_BLOCK = 128
GATE_LANES = 128
LN_EPS = 1e-5
GN_EPS = 64e-5
NEG_INF = -1e30
LOG2E = 1.4426950408889634
MIB = 2 ** 20

_PAIR_LO = (0, 0, 0, 1, 1, 2)
_PAIR_HI = (1, 2, 3, 2, 3, 3)


def _params(sem, vmem_mib):
    return pltpu.CompilerParams(dimension_semantics=sem, vmem_limit_bytes=vmem_mib * MIB)


def _dot(a, b):
    return jnp.dot(a, b, preferred_element_type=F32)


def _dot_nt(a, b):
    return lax.dot_general(a, b, (((1,), (1,)), ((), ())), preferred_element_type=F32)


def _dot_tn(a, b):
    return lax.dot_general(a, b, (((0,), (0,)), ((), ())), preferred_element_type=F32)


def _split(x):
    hi = x.astype(BF16)
    lo = (x - hi.astype(F32)).astype(BF16)
    return hi, lo


def _dot3(a, b, dot=_dot):
    return _dot3_multi(a, [b], dot)[0]


def _dot3_multi(a, bs, dot=_dot):
    ah, al = a
    ax = 0 if dot is _dot_nt else 1
    ns = [b[0].shape[ax] for b in bs]
    if any(n % 128 for n in ns):
        return [dot(ah, bh) + (dot(ah, bl) + dot(al, bh)) for bh, bl in bs]
    kax = 0 if dot is _dot_tn else 1
    if ah.shape[kax] == 128:
        lhs = jnp.concatenate([ah, al], axis=kax)
        if dot is _dot_nt:
            rhs = jnp.concatenate([jnp.concatenate([x, y], axis=1) for bh, bl in bs
                                   for x, y in ((bh, bh), (bl, jnp.zeros_like(bl)))], axis=0)
        else:
            top = jnp.concatenate([x for b in bs for x in b], axis=1)
            bot = jnp.concatenate([x for bh, bl in bs for x in (bh, jnp.zeros_like(bl))], axis=1)
            rhs = jnp.concatenate([top, bot], axis=0)
        r = dot(lhs, rhs)
        outs, o = [], 0
        for n in ns:
            outs.append(r[:, o:o + n] + r[:, o + n:o + 2 * n])
            o += 2 * n
        return outs
    r1 = dot(ah, jnp.concatenate([x for b in bs for x in b], axis=ax))
    r2 = dot(al, jnp.concatenate([b[0] for b in bs], axis=ax)) if len(bs) > 1 else dot(al, bs[0][0])
    outs, o1, o2 = [], 0, 0
    for n in ns:
        outs.append((r1[:, o1:o1 + n] + r1[:, o1 + n:o1 + 2 * n]) + r2[:, o2:o2 + n])
        o1 += 2 * n
        o2 += n
    return outs


def _dot_exact_rhs(x, m, dot=_dot):
    hi = x.astype(BF16)
    r1 = x - hi.astype(F32)
    mid = r1.astype(BF16)
    lo = (r1 - mid.astype(F32)).astype(BF16)
    return dot(hi, m) + (dot(mid, m) + dot(lo, m))


def _head_sum(x, bd_ref):
    bd = bd_ref[...]
    parts = []
    for j in range(x.shape[1] // 256):
        parts.append(_dot_exact_rhs(x[:, 256 * j:256 * (j + 1)], bd))
    return jnp.concatenate(parts, axis=1)


def _layer_norm(z, g, b):
    mu = jnp.mean(z, axis=-1, keepdims=True)
    zc = z - mu
    var = jnp.mean(zc * zc, axis=-1, keepdims=True)
    return zc * lax.rsqrt(var + LN_EPS) * g + b


def _row_tile(n):
    for t in (256, 128, 64, 32, 16, 8):
        if n % t == 0:
            return t
    raise ValueError(n)


def _const_spec(shape):
    nd = len(shape)
    return pl.BlockSpec(shape, lambda *_: (0,) * nd)


def _rwkv_pre_kernel(has_vres, n_prompt_tiles, *refs):
    if has_vres:
        (x_ref, x8_ref, xtail_ref, vf_ref, mix_ref, wr_ref, wk_ref, wv_ref, w1_ref, w2_ref, w0_ref,
         a1_ref, a2_ref, a0_ref, g1_ref, g2_ref, v1_ref, v2_ref, v0_ref, kk_ref, ka_ref, bd_ref,
         r_out, lw_out, k_out, v_out, kn_out, b_out, g_out) = refs
    else:
        (x_ref, x8_ref, xtail_ref, mix_ref, wr_ref, wk_ref, wv_ref, w1_ref, w2_ref, w0_ref,
         a1_ref, a2_ref, a0_ref, g1_ref, g2_ref, kk_ref, ka_ref, bd_ref,
         r_out, lw_out, k_out, v_out, kn_out, b_out, g_out) = refs
    x = x_ref[...]
    i = pl.program_id(0)
    first = lax.broadcasted_iota(jnp.int32, x.shape, 0) == 0
    above = jnp.where(i == 0, 0.0, x8_ref[7:8, :])
    shifted = jnp.where(first, above, pltpu.roll(x, 1, 0))
    xx = jnp.where(i >= n_prompt_tiles, xtail_ref[...], shifted) - x

    def mixed(i):
        return (x + xx * mix_ref[i:i + 1, :]).astype(BF16)

    xr, xw, xk, xv, xa, xg = (mixed(i) for i in range(6))
    r = _dot(xr, wr_ref[...])
    k = _dot(xk, wk_ref[...])
    v = _dot(xv, wv_ref[...])
    zw = w0_ref[...] + _dot(jnp.tanh(_dot(xw, w1_ref[...])).astype(BF16), w2_ref[...])
    lw = (-math.exp(-0.5)) * jax.nn.sigmoid(zw)
    alpha = jax.nn.sigmoid(a0_ref[...] + _dot(_dot(xa, a1_ref[...]).astype(BF16), a2_ref[...]))
    g = _dot(jax.nn.sigmoid(_dot(xg, g1_ref[...])).astype(BF16), g2_ref[...])
    if has_vres:
        gate_v = jax.nn.sigmoid(v0_ref[...] + _dot(_dot(xv, v1_ref[...]).astype(BF16), v2_ref[...]))
        v = v + (vf_ref[...] - v) * gate_v
    kkr = k * kk_ref[...]
    norm = jnp.sqrt(_head_sum(kkr * kkr, bd_ref))
    kn = kkr / jnp.maximum(norm, 1e-12)
    k2 = k * (1.0 + (alpha - 1.0) * ka_ref[...])
    r_out[...] = r
    lw_out[...] = lw
    k_out[...] = k2
    v_out[...] = v
    kn_out[...] = kn
    b_out[...] = kn * alpha
    g_out[...] = g


def _rwkv_pre(x, x_tail_prev, vfirst, w, tm):
    n, d = x.shape
    t_tail = x_tail_prev.shape[0]
    assert t_tail % tm == 0 and (n - t_tail) % tm == 0
    n_prompt_tiles = (n - t_tail) // tm
    has_vres = vfirst is not None
    row = pl.BlockSpec((tm, d), lambda i: (i, 0))
    above = pl.BlockSpec((8, d), lambda i: (jnp.maximum(i * (tm // 8) - 1, 0), 0))
    tail = pl.BlockSpec((tm, d), lambda i: (jnp.maximum(i - n_prompt_tiles, 0), 0))
    ins = [x, x, x_tail_prev] + ([vfirst] if has_vres else [])
    specs = [row, above, tail] + ([row] if has_vres else [])
    names = ['mix', 'wr', 'wk', 'wv', 'w1', 'w2', 'w0', 'a1', 'a2', 'a0', 'g1', 'g2']
    if has_vres:
        names += ['v1', 'v2', 'v0']
    names += ['k_k', 'k_a', 'bd']
    for nm in names:
        ins.append(w[nm])
        specs.append(_const_spec(w[nm].shape))
    out = jax.ShapeDtypeStruct((n, d), F32)
    return pl.pallas_call(
        functools.partial(_rwkv_pre_kernel, has_vres, n_prompt_tiles),
        grid=(n // tm,),
        in_specs=specs,
        out_specs=[row] * 7,
        out_shape=[out] * 7,
        compiler_params=_params(("parallel",), 56),
        name="rwkv_pre",
    )(*ins)


def _wkv_chunk_kernel(C, n_pairs, r_ref, lw_ref, k_ref, v_ref, kn_ref, b_ref,
                      rp_out, yp_out, m_out, n_out):
    C2 = 2 * C
    row = lax.broadcasted_iota(jnp.int32, (C2, C2), 0)
    col = lax.broadcasted_iota(jnp.int32, (C2, C2), 1)
    same = (row >= C) == (col >= C)
    strict = jnp.logical_and(same, col < row)
    incl = jnp.logical_and(same, col <= row)
    tri_incl = jnp.where(incl, 1.0, 0.0).astype(BF16)
    eye_c = jnp.where(row == col, 1.0, 0.0).astype(F32)
    r128 = lax.broadcasted_iota(jnp.int32, (PAIR, PAIR), 0)
    c128 = lax.broadcasted_iota(jnp.int32, (PAIR, PAIR), 1)
    eye_p = r128 == c128
    head0 = lax.broadcasted_iota(jnp.int32, (1, PAIR), 1) < HEAD_DIM

    def stack(t):
        return jnp.concatenate([jnp.where(head0, t, 0.0), jnp.where(head0, 0.0, t)], axis=0)

    n_sq = int(math.log2(C)) - 1
    sls = [slice(PAIR * p, PAIR * (p + 1)) for p in range(n_pairs)]

    def each(f, *lists):
        return [f(*a) for a in zip(*lists)]

    def load(ref):
        return [stack(ref[:, sl]) for sl in sls]

    lws = load(lw_ref)
    L = each(lambda x: _dot_exact_rhs_left(tri_incl, x), lws)
    lc = each(lambda l: l[C - 1:C, :] + l[C2 - 1:C2, :], L)
    kn, bs, ks = load(kn_ref), load(b_ref), load(k_ref)
    at = each(lambda n, l, w: _split(-n * jnp.exp(l - w)), kn, L, lws)
    rt_f = each(lambda r, l: r * jnp.exp(l), load(r_ref), L)
    rt = each(_split, rt_f)
    e_nl = each(lambda l: jnp.exp(-l), L)
    bb = each(lambda b, e: _split(b * e), bs, e_nl)
    kb = each(lambda k, e: _split(k * e), ks, e_nl)
    e_lc = each(lambda c, l: jnp.exp(c - l), lc, L)
    bh = each(lambda b, e: _split(b * e), bs, e_lc)
    kh = each(lambda k, e: _split(k * e), ks, e_lc)
    vs = each(_split, load(v_ref))

    g_a = each(lambda a, b, k: _dot3_multi(a, [b, k], _dot_nt), at, bb, kb)
    g_r = each(lambda a, b, k: _dot3_multi(a, [b, k], _dot_nt), rt, bb, kb)
    a_ab = [jnp.where(strict, g[0], 0.0) for g in g_a]
    a_ak = [_split(jnp.where(strict, g[1], 0.0)) for g in g_a]
    a_rb = [_split(jnp.where(incl, g[0], 0.0)) for g in g_r]
    a_rk = [_split(jnp.where(incl, g[1], 0.0)) for g in g_r]

    pws = each(lambda a: _split(_dot3(a, a)), each(_split, a_ab))
    tm = each(lambda a: eye_c + a, a_ab)
    for step in range(n_sq):
        if step < n_sq - 1:
            res = each(lambda s, t: _dot3_multi(s, [_split(t), s]), pws, tm)
            tm = [t + r[0] for t, r in zip(tm, res)]
            pws = [_split(r[1]) for r in res]
        else:
            tm = each(lambda t, s: t + _dot3(s, _split(t)), tm, pws)
    tms = each(_split, tm)
    w1 = each(lambda a, v: _split(_dot3(a, v)), a_ak, vs)
    tx = each(lambda t, a, w: _dot3_multi(t, [a, w]), tms, at, w1)
    at2 = [_split(r[0]) for r in tx]
    v2 = [_split(r[1]) for r in tx]
    ax = each(lambda a, x, v: _dot3_multi(a, [x, v]), a_rb, at2, v2)
    r2 = [r + x[0] for r, x in zip(rt_f, ax)]
    y2 = [x[1] + _dot3(c, v) for x, c, v in zip(ax, a_rk, vs)]
    bx = each(lambda b, x, v: _dot3_multi(b, [x, v], _dot_tn), bh, at2, v2)
    mm = [jnp.where(eye_p, jnp.exp(c), 0.0) + x[0] for c, x in zip(lc, bx)]
    nn = [x[1] + _dot3(k, v, _dot_tn) for x, k, v in zip(bx, kh, vs)]
    rp_out[...] = jnp.concatenate([x[:C, :] + x[C:, :] for x in r2], axis=1)
    yp_out[...] = jnp.concatenate([x[:C, :] + x[C:, :] for x in y2], axis=1)
    m_out[0] = jnp.stack(mm, axis=0)
    n_out[0] = jnp.stack(nn, axis=0)


def _dot_exact_rhs_left(m, x):
    hi = x.astype(BF16)
    r1 = x - hi.astype(F32)
    mid = r1.astype(BF16)
    lo = (r1 - mid.astype(F32)).astype(BF16)
    return _dot(m, hi) + (_dot(m, mid) + _dot(m, lo))


def _wkv_chunks(arrs, C, row0, n_chunks, pairs_per_step):
    d = arrs[0].shape[1]
    n_pairs = d // PAIR
    assert row0 % C == 0 and n_pairs % pairs_per_step == 0
    blk0 = row0 // C
    lanes = pairs_per_step * PAIR
    in_spec = pl.BlockSpec((C, lanes), lambda c, q: (blk0 + c, q))
    out_row = pl.BlockSpec((C, lanes), lambda c, q: (c, q))
    out_mat = pl.BlockSpec((1, pairs_per_step, PAIR, PAIR), lambda c, q: (c, q, 0, 0))
    t = n_chunks * C
    return pl.pallas_call(
        functools.partial(_wkv_chunk_kernel, C, pairs_per_step),
        grid=(n_chunks, n_pairs // pairs_per_step),
        in_specs=[in_spec] * 6,
        out_specs=[out_row, out_row, out_mat, out_mat],
        out_shape=[jax.ShapeDtypeStruct((t, d), F32), jax.ShapeDtypeStruct((t, d), F32),
                   jax.ShapeDtypeStruct((n_chunks, n_pairs, PAIR, PAIR), F32),
                   jax.ShapeDtypeStruct((n_chunks, n_pairs, PAIR, PAIR), F32)],
        compiler_params=_params(("parallel", "parallel"), 32),
        name=f"wkv_chunks_c{C}",
    )(*arrs)


def _wkv_seq_kernel(n_pairs, n_steps, rp_ref, yp_ref, m_ref, n_ref, s0_ref, y_out, s_out, s_scr):
    j = pl.program_id(1)

    @pl.when(j == 0)
    def _():
        s_scr[...] = s0_ref[0]

    ys, new_s = [], []
    for p in range(n_pairs):
        sl = slice(PAIR * p, PAIR * (p + 1))
        ss = _split(s_scr[p])
        ys.append(_dot3(_split(rp_ref[:, sl]), ss) + yp_ref[:, sl])
        new_s.append(_dot3(_split(m_ref[0, p]), ss) + n_ref[0, p])
    y_out[...] = jnp.concatenate(ys, axis=1)
    s_scr[...] = jnp.stack(new_s, axis=0)

    @pl.when(j == n_steps - 1)
    def _():
        s_out[0] = s_scr[...]


def _wkv_seq(rp, yp, m, nn, s0, C, n_seq, n_steps):
    t, d = rp.shape
    n_pairs = d // PAIR
    row = pl.BlockSpec((C, d), lambda s, j: (s * n_steps + j, 0))
    mat = pl.BlockSpec((1, n_pairs, PAIR, PAIR), lambda s, j: (s * n_steps + j, 0, 0, 0))
    st = pl.BlockSpec((1, n_pairs, PAIR, PAIR), lambda s, j: (s, 0, 0, 0))
    return pl.pallas_call(
        functools.partial(_wkv_seq_kernel, n_pairs, n_steps),
        grid=(n_seq, n_steps),
        in_specs=[row, row, mat, mat, st],
        out_specs=[row, st],
        out_shape=[jax.ShapeDtypeStruct((t, d), F32),
                   jax.ShapeDtypeStruct((n_seq, n_pairs, PAIR, PAIR), F32)],
        scratch_shapes=[pltpu.VMEM((n_pairs, PAIR, PAIR), F32)],
        compiler_params=_params(("arbitrary", "arbitrary"), 32),
        name=f"wkv_seq_c{C}",
    )(rp, yp, m, nn, s0)


def _state_to_blockdiag(s):
    b, h, n, _ = s.shape
    st = jnp.swapaxes(s, -1, -2).reshape(b, h // 2, 2, n, n)
    z = jnp.zeros_like(st[:, :, 0])
    top = jnp.concatenate([st[:, :, 0], z], axis=-1)
    bot = jnp.concatenate([z, st[:, :, 1]], axis=-1)
    return jnp.concatenate([top, bot], axis=-2)


def _blockdiag_to_state(bd):
    b, hp, _, _ = bd.shape
    n = HEAD_DIM
    s0 = bd[:, :, :n, :n]
    s1 = bd[:, :, n:, n:]
    st = jnp.stack([s0, s1], axis=2).reshape(b, hp * 2, n, n)
    return jnp.swapaxes(st, -1, -2)


def _route(x1, rw_ref, rb_ref):
    logits = _dot3(_split(rw_ref[...]), _split(x1), _dot_nt)
    mx = jnp.max(logits, axis=0, keepdims=True)
    ex = jnp.exp(logits - mx)
    scores = ex / jnp.sum(ex, axis=0, keepdims=True)
    sel = scores + rb_ref[...]
    rows = [sel[e:e + 1, :] for e in range(N_EXPERTS)]
    srow = [scores[e:e + 1, :] for e in range(N_EXPERTS)]

    def top2(vals):
        m1 = jnp.maximum(jnp.maximum(vals[0], vals[1]), jnp.maximum(vals[2], vals[3]))
        i1 = jnp.where(vals[0] == m1, 0, jnp.where(vals[1] == m1, 1, jnp.where(vals[2] == m1, 2, 3)))
        rest = [jnp.where(i1 == j, -jnp.inf, vals[j]) for j in range(4)]
        m2 = jnp.maximum(jnp.maximum(rest[0], rest[1]), jnp.maximum(rest[2], rest[3]))
        i2 = jnp.where(rest[0] == m2, 0, jnp.where(rest[1] == m2, 1, jnp.where(rest[2] == m2, 2, 3)))
        return m1, i1, m2, i2

    gscore = []
    for gidx in range(N_GROUPS):
        m1, _, m2, _ = top2(rows[4 * gidx:4 * gidx + 4])
        gscore.append(m1 + m2)
    gm = jnp.maximum(jnp.maximum(gscore[0], gscore[1]), jnp.maximum(gscore[2], gscore[3]))
    gi = jnp.where(gscore[0] == gm, 0, jnp.where(gscore[1] == gm, 1, jnp.where(gscore[2] == gm, 2, 3)))

    def pick(rws, j):
        return jnp.where(gi == 0, rws[j], jnp.where(gi == 1, rws[4 + j],
                                                    jnp.where(gi == 2, rws[8 + j], rws[12 + j])))

    in_grp = [pick(rows, j) for j in range(4)]
    sc_grp = [pick(srow, j) for j in range(4)]
    _, i1, _, i2 = top2(in_grp)

    def at(vals, idx):
        return jnp.where(idx == 0, vals[0], jnp.where(idx == 1, vals[1],
                                                      jnp.where(idx == 2, vals[2], vals[3])))

    ga = at(sc_grp, i1)
    gb = at(sc_grp, i2)
    tot = ga + gb
    ga = ga / tot
    gb = gb / tot
    lo = jnp.minimum(i1, i2)
    hi = jnp.maximum(i1, i2)
    g_lo = jnp.where(i1 < i2, ga, gb)
    g_hi = jnp.where(i1 < i2, gb, ga)
    pair = jnp.where(lo == 0, hi - 1, jnp.where(lo == 1, hi + 1, 5))
    cls = gi * 6 + pair
    return cls.astype(jnp.int32), g_lo, g_hi


def _mix_post_kernel(is_rwkv, alpha_dn, *refs):
    if is_rwkv:
        (x_ref, y_ref, r_ref, k_ref, v_ref, g_ref, lg_ref, lb_ref, rk_ref, bd_ref,
         wo_ref, n1g_ref, n1b_ref, rw_ref, rb_ref, x1_out, cls_out) = refs
        y = y_ref[...]
        mu = _head_sum(y, bd_ref) * (1.0 / HEAD_DIM)
        yc = y - mu
        var = _head_sum(yc * yc, bd_ref) * (1.0 / HEAD_DIM)
        yn = yc * lax.rsqrt(var + GN_EPS) * lg_ref[...] + lb_ref[...]
        v = v_ref[...]
        bonus = _head_sum(r_ref[...] * k_ref[...] * rk_ref[...], bd_ref)
        z = (yn + bonus * v) * g_ref[...]
    else:
        (x_ref, o_ref, gt_ref, wo_ref, n1g_ref, n1b_ref, rw_ref, rb_ref,
         x1_out, cls_out) = refs
        z = o_ref[...] * gt_ref[...]
    h = _dot(z.astype(BF16), wo_ref[...])
    x1 = _layer_norm(alpha_dn * x_ref[...] + h, n1g_ref[...], n1b_ref[...])
    cls, g_lo, g_hi = _route(x1, rw_ref, rb_ref)
    tm, d = x1.shape
    cls_out[...] = jnp.broadcast_to(cls, (8, tm))
    pieces = [x.astype(F32) for x in _split3(g_lo) + _split3(g_hi)]
    gs = jnp.concatenate(pieces + [jnp.zeros((16 - len(pieces), tm), F32)], axis=0).astype(BF16)
    rr = lax.broadcasted_iota(jnp.int32, (16, GATE_LANES), 0)
    cc = lax.broadcasted_iota(jnp.int32, (16, GATE_LANES), 1)
    place = jnp.where(jnp.logical_and(rr < 6, cc == jnp.where(rr < 3, 0, 1)), 1.0, 0.0).astype(BF16)
    x1_out[:, :d] = x1
    x1_out[:, d:] = _dot_tn(gs, place)


def _mix_post(is_rwkv, alpha_dn, acts, consts, tm):
    n, d = acts[0].shape
    row = pl.BlockSpec((tm, d), lambda i: (i, 0))
    specs = [row] * len(acts) + [_const_spec(c.shape) for c in consts]
    lane = pl.BlockSpec((8, tm), lambda i: (0, i))
    wide = pl.BlockSpec((tm, d + GATE_LANES), lambda i: (i, 0))
    return pl.pallas_call(
        functools.partial(_mix_post_kernel, is_rwkv, alpha_dn),
        grid=(n // tm,),
        in_specs=specs,
        out_specs=[wide, lane],
        out_shape=[jax.ShapeDtypeStruct((n, d + GATE_LANES), F32), jax.ShapeDtypeStruct((8, n), jnp.int32)],
        compiler_params=_params(("parallel",), 48),
        name="mix_post_rwkv" if is_rwkv else "mix_post_fox",
    )(*acts, *consts)


def _moe_kernel(n_tok, d, d_exp, n_blk, e1_ref, e2_ref, off_ref, nv_ref, order_ref, x_hbm,
                wgu1_ref, wd1_ref, wgu2_ref, wd2_ref, out_hbm, xbuf, obuf, gsem, ssem):
    b = pl.program_id(0)
    slot = lax.rem(b, 2)
    nv_b = nv_ref[b]
    unroll = 8

    def gather_start(blk, sl):
        base = off_ref[blk]

        def body(c, carry):
            for u in range(unroll):
                i = c * unroll + u
                tok = order_ref[jnp.minimum(base + i, n_tok - 1)]
                pltpu.make_async_copy(x_hbm.at[pl.ds(tok, 1)], xbuf.at[sl, pl.ds(i, 1)], gsem.at[sl]).start()
            return carry

        lax.fori_loop(0, E_BLOCK // unroll, body, 0)

    def gather_wait(sl):
        pltpu.make_async_copy(x_hbm.at[pl.ds(0, E_BLOCK)], xbuf.at[sl], gsem.at[sl]).wait()

    def scatter_start(blk, sl):
        base = off_ref[blk]

        def body(i, carry):
            tok = order_ref[base + i]
            pltpu.make_async_copy(obuf.at[sl, pl.ds(i, 1)], out_hbm.at[pl.ds(tok, 1)], ssem.at[sl]).start()
            return carry

        lax.fori_loop(0, nv_ref[blk], body, 0)

    def scatter_wait(blk, sl):
        nv = nv_ref[blk]

        @pl.when(nv == E_BLOCK)
        def _():
            pltpu.make_async_copy(obuf.at[sl], out_hbm.at[pl.ds(0, E_BLOCK)], ssem.at[sl]).wait()

        @pl.when(nv < E_BLOCK)
        def _():
            def body(i, carry):
                pltpu.make_async_copy(obuf.at[sl, pl.ds(0, 1)], out_hbm.at[pl.ds(0, 1)], ssem.at[sl]).wait()
                return carry

            lax.fori_loop(0, nv, body, 0)

    @pl.when(jnp.logical_and(b == 0, nv_b > 0))
    def _():
        gather_start(0, 0)

    nxt = jnp.minimum(b + 1, n_blk - 1)

    @pl.when(jnp.logical_and(b + 1 < n_blk, nv_ref[nxt] > 0))
    def _():
        gather_start(nxt, 1 - slot)

    @pl.when(nv_b > 0)
    def _():
        gather_wait(slot)
        rows = xbuf[slot]
        xb = rows[:, :d].astype(BF16)

        def expert(wgu_ref, wd_ref):
            gu = _dot(xb, wgu_ref[0])
            hmid = jax.nn.silu(gu[:, :d_exp]) * gu[:, d_exp:]
            return _dot(hmid.astype(BF16), wd_ref[0])

        y1 = expert(wgu1_ref, wd1_ref)
        y2 = expert(wgu2_ref, wd2_ref)
        obuf[slot] = y1 * rows[:, d:d + 1] + y2 * rows[:, d + 1:d + 2]

    prev = jnp.maximum(b - 1, 0)

    @pl.when(jnp.logical_and(b > 0, nv_ref[prev] > 0))
    def _():
        scatter_wait(prev, 1 - slot)

    @pl.when(nv_b > 0)
    def _():
        scatter_start(b, slot)

    @pl.when(jnp.logical_and(b == n_blk - 1, nv_b > 0))
    def _():
        scatter_wait(b, slot)


def _moe(xg, cls, w_gu, w_down):
    n = xg.shape[0]
    d, d_exp = w_down.shape[2], w_down.shape[1]
    n_blk = (n + N_CLASSES * (E_BLOCK - 1) + E_BLOCK - 1) // E_BLOCK
    counts = jnp.bincount(cls, length=N_CLASSES).astype(jnp.int32)
    padded = (counts + E_BLOCK - 1) // E_BLOCK * E_BLOCK
    pad_end = jnp.cumsum(padded)
    pad_start = pad_end - padded
    seg_start = jnp.cumsum(counts) - counts
    order = jnp.argsort(cls).astype(jnp.int32)
    blk_start = jnp.arange(n_blk, dtype=jnp.int32) * E_BLOCK
    blk_cls = jnp.minimum(jnp.sum((pad_end[None, :] <= blk_start[:, None]).astype(jnp.int32), axis=1),
                          N_CLASSES - 1)
    classes = jnp.arange(N_CLASSES, dtype=jnp.int32)

    def lookup(table, idx):
        return jnp.sum(jnp.where(idx[:, None] == classes[None, :], table[None, :], 0), axis=1)

    rank0 = blk_start - lookup(pad_start, blk_cls)
    blk_nv = jnp.clip(lookup(counts, blk_cls) - rank0, 0, E_BLOCK).astype(jnp.int32)
    blk_off = (lookup(seg_start, blk_cls) + rank0).astype(jnp.int32)
    last_cls = jnp.max(jnp.where(blk_nv > 0, blk_cls, 0))
    blk_cls = jnp.where(blk_nv > 0, blk_cls, last_cls)
    e_lo = jnp.asarray([(c // 6) * EXPERTS_PER_GROUP + _PAIR_LO[c % 6] for c in range(N_CLASSES)], jnp.int32)
    e_hi = jnp.asarray([(c // 6) * EXPERTS_PER_GROUP + _PAIR_HI[c % 6] for c in range(N_CLASSES)], jnp.int32)
    blk_e1 = lookup(e_lo, blk_cls)
    blk_e2 = lookup(e_hi, blk_cls)

    gu1 = pl.BlockSpec((1, d, 2 * d_exp), lambda b, e1, e2, of, nv, od: (e1[b], 0, 0))
    dn1 = pl.BlockSpec((1, d_exp, d), lambda b, e1, e2, of, nv, od: (e1[b], 0, 0))
    gu2 = pl.BlockSpec((1, d, 2 * d_exp), lambda b, e1, e2, of, nv, od: (e2[b], 0, 0))
    dn2 = pl.BlockSpec((1, d_exp, d), lambda b, e1, e2, of, nv, od: (e2[b], 0, 0))
    grid_spec = pltpu.PrefetchScalarGridSpec(
        num_scalar_prefetch=5,
        grid=(n_blk,),
        in_specs=[pl.BlockSpec(memory_space=pl.ANY), gu1, dn1, gu2, dn2],
        out_specs=pl.BlockSpec(memory_space=pl.ANY),
        scratch_shapes=[pltpu.VMEM((2, E_BLOCK, d + GATE_LANES), F32), pltpu.VMEM((2, E_BLOCK, d), F32),
                        pltpu.SemaphoreType.DMA((2,)), pltpu.SemaphoreType.DMA((2,))],
    )
    return pl.pallas_call(
        functools.partial(_moe_kernel, n, d, d_exp, n_blk),
        grid_spec=grid_spec,
        out_shape=jax.ShapeDtypeStruct((n, d), F32),
        compiler_params=_params(("arbitrary",), 40),
        name="moe_experts",
    )(blk_e1, blk_e2, blk_off, blk_nv, order, xg, w_gu, w_down, w_gu, w_down)


def _bias_placement(nh, lead_ones):
    p = np.zeros((4 * nh, nh * PAIR), np.float32)
    for h in range(nh):
        lane0 = PAIR * h + HEAD_DIM
        if lead_ones:
            p[h, lane0:lane0 + 3] = 1.0
            for piece in range(3):
                p[(piece + 1) * nh + h, lane0 + 3 + piece] = 1.0
        else:
            for piece in range(3):
                p[piece * nh + h, lane0 + piece] = 1.0
            p[3 * nh + h, lane0 + 3:lane0 + 6] = 1.0
    return jnp.asarray(p, BF16)


def _pad_heads(w, width):
    d, hd = w.shape
    nh = hd // HEAD_DIM
    w3 = w.reshape(d, nh, HEAD_DIM)
    return jnp.concatenate([w3, jnp.zeros((d, nh, width - HEAD_DIM), w.dtype)], axis=-1).reshape(d, nh * width)


def _bias_rows(rem, lead_ones):
    parts = [x.astype(F32) for x in _split3(rem)]
    ones = [jnp.ones_like(rem)]
    return jnp.concatenate(ones + parts if lead_ones else parts + ones, axis=0).astype(BF16)


def _ln2_kernel(alpha_dn, with_kv, tiles_per_blk, *refs):
    if with_kv:
        (x1_ref, m_ref, g_ref, b_ref, wk_ref, wv_ref, wka_ref, wva_ref, wf_ref, bf_ref, tri_ref,
         pk_ref, eye_ref, ones_ref,
         x2_out, k_out, v_out, ka_out, vt_out, lf_out, cum_out, rem_out, carry, base_scr) = refs
    else:
        x1_ref, m_ref, g_ref, b_ref, x2_out = refs
    x2 = _layer_norm(alpha_dn * x1_ref[...] + m_ref[...], g_ref[...], b_ref[...])
    x2_out[...] = x2
    if with_kv:
        i = pl.program_id(0)
        xb = x2.astype(BF16)
        k_out[...] = _dot(xb, wk_ref[...])
        v_out[...] = _dot(xb, wv_ref[...])
        z = _dot3(_split(wf_ref[...]), _split(x2), _dot_nt) + bf_ref[...]
        lf = jnp.minimum(z, 0.0) - jnp.log1p(jnp.exp(-jnp.abs(z)))
        lf_out[...] = lf

        @pl.when(i == 0)
        def _():
            carry[...] = jnp.zeros_like(carry)

        cum = _dot_exact_rhs(lf, tri_ref[...]) + carry[...]
        cum_out[...] = cum
        carry[...] = cum[:, -1:]

        cum2 = cum * LOG2E

        @pl.when(lax.rem(i, tiles_per_blk) == 0)
        def _():
            base_scr[...] = cum2[:, :1]

        rem = cum2 - base_scr[...]
        rem_out[...] = rem
        ka_out[...] = (_dot(xb, wka_ref[...]) + _dot_tn(_bias_rows(-rem, False), pk_ref[...])).astype(BF16)
        vw = _dot(xb, wva_ref[...]).astype(BF16)
        vt_out[...] = (_dot_tn(vw, eye_ref[...]) + ones_ref[...]).astype(BF16)


def _ln2(alpha_dn, x1, m, g, b, kv, tm):
    n, d = m.shape
    row = pl.BlockSpec((tm, d), lambda i: (i, 0))
    with_kv = kv is not None
    ins = [x1, m, g, b]
    specs = [row, row, _const_spec(g.shape), _const_spec(b.shape)]
    outs = [jax.ShapeDtypeStruct((n, d), F32)]
    ospecs = [row]
    scratch = []
    if with_kv:
        nh = kv['wf_t'].shape[0]
        tri = (jnp.arange(tm)[:, None] <= jnp.arange(tm)[None, :]).astype(BF16)
        eye = jnp.eye(tm, dtype=BF16)
        ones_col = (jnp.arange(nh * V_ROWS) % V_ROWS == HEAD_DIM).astype(F32).reshape(-1, 1)
        extra = [kv['wk'], kv['wv'], kv['wk_aug'], kv['wv_aug'], kv['wf_t'], kv['bf'], tri,
                 _bias_placement(nh, False), eye, ones_col]
        ins += extra
        specs += [_const_spec(e.shape) for e in extra]
        lane = pl.BlockSpec((nh, tm), lambda i: (0, i))
        outs += [jax.ShapeDtypeStruct((n, d), F32), jax.ShapeDtypeStruct((n, d), F32),
                 jax.ShapeDtypeStruct((n, nh * PAIR), BF16), jax.ShapeDtypeStruct((nh * V_ROWS, n), BF16),
                 jax.ShapeDtypeStruct((nh, n), F32), jax.ShapeDtypeStruct((nh, n), F32),
                 jax.ShapeDtypeStruct((nh, n), F32)]
        ospecs += [row, row, pl.BlockSpec((tm, nh * PAIR), lambda i: (i, 0)),
                   pl.BlockSpec((nh * V_ROWS, tm), lambda i: (0, i)), lane, lane, lane]
        scratch = [pltpu.VMEM((nh, 1), F32), pltpu.VMEM((nh, 1), F32)]
    res = pl.pallas_call(
        functools.partial(_ln2_kernel, alpha_dn, with_kv, kv['tiles_per_blk'] if with_kv else 1),
        grid=(n // tm,),
        in_specs=specs,
        out_specs=ospecs,
        out_shape=outs,
        scratch_shapes=scratch,
        compiler_params=_params(("arbitrary",), 48),
        name="ln2_kv" if with_kv else "ln2",
    )(*ins)
    return res


def _fox_qg_kernel(x_ref, rem_ref, wq_ref, wqa_ref, wg_ref, pq_ref, q_out, qa_out, gate_out):
    xb = x_ref[...].astype(BF16)
    scale = LOG2E * HEAD_DIM ** -0.5
    q_out[...] = (_dot(xb, wq_ref[...]) * scale).astype(BF16)
    qa_out[...] = (_dot(xb, wqa_ref[...]) * scale
                   + _dot_tn(_bias_rows(rem_ref[...], True), pq_ref[...])).astype(BF16)
    gate_out[...] = jax.nn.sigmoid(_dot(xb, wg_ref[...]))


def _fox_qg(x, rem_r, wq, wq_aug, wg, tm):
    n, d = x.shape
    nh = rem_r.shape[0]
    row = pl.BlockSpec((tm, d), lambda i: (i, 0))
    pq = _bias_placement(nh, True)
    return pl.pallas_call(
        _fox_qg_kernel,
        grid=(n // tm,),
        in_specs=[row, pl.BlockSpec((nh, tm), lambda i: (0, i)), _const_spec(wq.shape),
                  _const_spec(wq_aug.shape), _const_spec(wg.shape), _const_spec(pq.shape)],
        out_specs=[row, pl.BlockSpec((tm, nh * PAIR), lambda i: (i, 0)), row],
        out_shape=[jax.ShapeDtypeStruct((n, d), BF16), jax.ShapeDtypeStruct((n, nh * PAIR), BF16),
                   jax.ShapeDtypeStruct((n, d), F32)],
        compiler_params=_params(("parallel",), 48),
        name="fox_qg",
    )(x, rem_r, wq, wq_aug, wg, pq)


V_ROWS = 80


def _fox_prompt_kernel(tq, hps, nh, base_ref, jlo_ref, q_ref, k_ref, vt_ref, o_out, s_scr, p_scr):
    g = pl.program_id(0)
    i = pl.program_id(1)
    j_lo = jlo_ref[g * pl.num_programs(1) + i]
    rr = lax.broadcasted_iota(jnp.int32, (tq, tq), 0)
    cc = lax.broadcasted_iota(jnp.int32, (tq, tq), 1)
    causal = rr <= cc

    def scores(j, h):
        k0 = pl.multiple_of(j * tq, tq)
        lanes = slice(PAIR * h, PAIR * (h + 1))
        return _dot_nt(k_ref[pl.ds(k0, tq), lanes], q_ref[:, lanes])

    def base_gap(j, h):
        head = g * hps + h
        return base_ref[i * nh + head] - base_ref[j * nh + head]

    def softmax(s, m, gap):
        m_new = jnp.maximum(m, jnp.max(s, axis=0, keepdims=True) + gap)
        return jnp.exp2(s - (m_new - gap)).astype(BF16), jnp.exp2(m - m_new), m_new

    def accum(acc, corr, p, j, h):
        k0 = pl.multiple_of(j * tq, tq)
        return acc * corr + _dot(vt_ref[h, :, pl.ds(k0, tq)], p)

    init = []
    for h in range(hps):
        s = jnp.where(causal, scores(i, h), NEG_INF)
        m0 = jnp.max(s, axis=0, keepdims=True)
        p_scr[h] = jnp.exp2(s - m0).astype(BF16)
        s_scr[h] = scores(jnp.maximum(i - 1, 0), h)
        init.append((m0, jnp.zeros((V_ROWS, tq), F32), jnp.ones((1, tq), F32)))

    def body(t, carry):
        j = i - 1 - t
        new = []
        for h in range(hps):
            m, acc, corr_prev = carry[h]
            acc = accum(acc, corr_prev, p_scr[h], j + 1, h)
            p, corr, m = softmax(s_scr[h], m, base_gap(j, h))
            p_scr[h] = p
            s_scr[h] = scores(jnp.maximum(j - 1, 0), h)
            new.append((m, acc, corr))
        return tuple(new)

    carry = lax.fori_loop(0, i - j_lo, body, tuple(init))
    outs = []
    for h in range(hps):
        m, acc, corr_prev = carry[h]
        acc = accum(acc, corr_prev, p_scr[h], j_lo, h)
        outs.append(acc[:HEAD_DIM, :] / acc[HEAD_DIM:HEAD_DIM + 1, :])
    o_out[...] = jnp.concatenate(outs, axis=0)


UNDERFLOW_LOG2 = -160.0


def _oldest_needed_block(qb, k, rem_r, base, t, tq, hps):
    d = qb.shape[1]
    nh = d // HEAD_DIM
    nb = t // tq
    slack = 2.0
    q = qb[:t].astype(F32).reshape(nb, tq, nh, HEAD_DIM)
    kk = k[:t].astype(BF16).astype(F32).reshape(nb, tq, nh, HEAD_DIM)
    rem = rem_r[:, :t].T.reshape(nb, tq, nh)
    q_norm = jnp.sqrt(jnp.max(jnp.sum(q * q, axis=-1), axis=1))
    k_norm = jnp.sqrt(jnp.max(jnp.sum(kk * kk, axis=-1), axis=(0, 1)))
    self_min = jnp.min(jnp.sum(q * kk, axis=-1), axis=1)
    s_max = q_norm * k_norm + jnp.max(rem, axis=1) + jnp.max(-rem, axis=(0, 1)) + slack
    m_low = self_min - slack
    b = base.reshape(nb, nh)
    gap = b[:, None, :] - b[None, :, :]
    older = jnp.arange(nb)[None, :, None] < jnp.arange(nb)[:, None, None]
    alive = jnp.logical_and(older, gap + (s_max - m_low)[:, None, :] >= UNDERFLOW_LOG2)
    idx = jnp.arange(nb, dtype=jnp.int32)
    j_lo = jnp.min(jnp.where(alive, idx[None, :, None], idx[:, None, None]), axis=1)
    return jnp.min(j_lo.reshape(nb, nh // hps, hps), axis=-1).T.reshape(-1).astype(jnp.int32)


def _fox_prompt(base, j_lo, q_aug, k_aug, vt_aug, t, tq, hps):
    nh = vt_aug.shape[0]
    return pl.pallas_call(
        functools.partial(_fox_prompt_kernel, tq, hps, nh),
        grid=(nh // hps, t // tq),
        in_specs=[pl.BlockSpec(memory_space=pltpu.SMEM), pl.BlockSpec(memory_space=pltpu.SMEM),
                  pl.BlockSpec((tq, hps * PAIR), lambda g, i: (i, g)),
                  pl.BlockSpec((t, hps * PAIR), lambda g, i: (0, g), pipeline_mode=pl.Buffered(1)),
                  pl.BlockSpec((hps, V_ROWS, t), lambda g, i: (g, 0, 0), pipeline_mode=pl.Buffered(1))],
        out_specs=pl.BlockSpec((hps * HEAD_DIM, tq), lambda g, i: (g, i)),
        out_shape=jax.ShapeDtypeStruct((nh * HEAD_DIM, t), F32),
        scratch_shapes=[pltpu.VMEM((hps, tq, tq), F32), pltpu.VMEM((hps, tq, tq), BF16)],
        compiler_params=_params(("parallel", "parallel"), 48),
        name="fox_prompt_attn",
    )(base, j_lo, q_aug, k_aug, vt_aug)


def _fox_sample_kernel(n_pairs, q_ref, kn_ref, vn_ref, lfn_ref, kc_ref, vc_ref, lfc_ref,
                       triu_ref, o_out):
    t = q_ref.shape[0]
    past = kc_ref.shape[1]
    lfc = lfc_ref[0]
    lfn = lfn_ref[0]
    nh = lfc.shape[1]
    lfc_s = _split3(lfc)
    cum_c_r = _dot_tn3(lfc_s, triu_ref[...])
    ones_row = jnp.ones((1, past), BF16)
    tot_r = _dot_m3(ones_row, lfc_s)
    tot_c = cum_c_r[:, past - 1:past]
    rr = lax.broadcasted_iota(jnp.int32, (t, t), 0)
    cc = lax.broadcasted_iota(jnp.int32, (t, t), 1)
    causal = rr >= cc
    tril = jnp.where(causal, 1.0, 0.0).astype(BF16)
    lfn_s = _split3(lfn)
    cum_n_c = (_dot_m3(tril, lfn_s) + tot_r) * LOG2E
    cum_n_r = (_dot_tn3(lfn_s, jnp.where(rr <= cc, 1.0, 0.0).astype(BF16)) + tot_c) * LOG2E
    cum_c_r = cum_c_r * LOG2E
    lane = lax.broadcasted_iota(jnp.int32, (1, PAIR), 1)
    head0 = lane < HEAD_DIM
    hl = lax.broadcasted_iota(jnp.int32, (1, nh), 1)
    for p in range(n_pairs):
        sl = slice(PAIR * p, PAIR * (p + 1))
        q = q_ref[:, sl]
        zero = jnp.zeros_like(q)
        kc = kc_ref[0, :, sl].astype(BF16)
        vc = vc_ref[0, :, sl].astype(BF16)
        kn = kn_ref[:, sl].astype(BF16)
        vn = vn_ref[:, sl].astype(BF16)
        outs = []
        for h in range(2):
            hh = 2 * p + h
            qh = jnp.where(head0, q, zero) if h == 0 else jnp.where(head0, zero, q)
            cq = jnp.sum(jnp.where(hl == hh, cum_n_c, 0.0), axis=1, keepdims=True)
            s_c = _dot_nt(qh, kc) + cq - cum_c_r[hh:hh + 1, :]
            s_n = _dot_nt(qh, kn) + cq - cum_n_r[hh:hh + 1, :]
            s_n = jnp.where(causal, s_n, NEG_INF)
            m = jnp.maximum(jnp.max(s_c, axis=1, keepdims=True), jnp.max(s_n, axis=1, keepdims=True))
            p_c = jnp.exp2(s_c - m)
            p_n = jnp.exp2(s_n - m)
            l = jnp.sum(p_c, axis=1, keepdims=True) + jnp.sum(p_n, axis=1, keepdims=True)
            acc = _dot(p_c.astype(BF16), vc) + _dot(p_n.astype(BF16), vn)
            outs.append(acc / l)
        o_out[:, sl] = jnp.where(head0, outs[0], outs[1])


def _split3(x):
    hi = x.astype(BF16)
    r1 = x - hi.astype(F32)
    mid = r1.astype(BF16)
    lo = (r1 - mid.astype(F32)).astype(BF16)
    return hi, mid, lo


def _dot_tn3(xs, m):
    return _dot_tn(xs[0], m) + (_dot_tn(xs[1], m) + _dot_tn(xs[2], m))


def _dot_m3(m, xs):
    return _dot(m, xs[0]) + (_dot(m, xs[1]) + _dot(m, xs[2]))


def _fox_sample(qb, k, v, lf_new, cache_k, cache_v, cache_lf, row0, n_stream, t):
    d = qb.shape[1]
    n_pairs = d // PAIR
    past = cache_k.shape[1]
    nh = cache_lf.shape[2]
    blk0 = row0 // t
    triu = (jnp.arange(past)[:, None] <= jnp.arange(past)[None, :]).astype(BF16)
    row = pl.BlockSpec((t, d), lambda b: (blk0 + b, 0))
    return pl.pallas_call(
        functools.partial(_fox_sample_kernel, n_pairs),
        grid=(n_stream,),
        in_specs=[row, row, row,
                  pl.BlockSpec((1, t, nh), lambda b: (b, 0, 0)),
                  pl.BlockSpec((1, past, d), lambda b: (b, 0, 0)),
                  pl.BlockSpec((1, past, d), lambda b: (b, 0, 0)),
                  pl.BlockSpec((1, past, nh), lambda b: (b, 0, 0)),
                  _const_spec(triu.shape)],
        out_specs=pl.BlockSpec((t, d), lambda b: (b, 0)),
        out_shape=jax.ShapeDtypeStruct((n_stream * t, d), F32),
        compiler_params=_params(("parallel",), 48),
        name="fox_sample_attn",
    )(qb, k, v, lf_new, cache_k, cache_v, cache_lf, triu)


def kernel(x_prompt, x_sample, state_shift, state_wkv, cache_k, cache_v, cache_logf, ln1_g, ln1_b, ln2_g, ln2_b, a_mix, a_w_rkv, a_w0, a_w1, a_w2, a_a0, a_a1, a_a2, a_v0, a_v1, a_v2, a_g1, a_g2, a_k_k, a_k_a, a_r_k, a_lnx_g, a_lnx_b, a_w_o, kv_w, kv_bf, b_w_qg, b_w_o, router_w, router_b, moe_w_gu, moe_w_down):
    nb, seq, d = x_prompt.shape
    db, dt, _ = x_sample.shape
    assert nb == 1
    n_heads = d // HEAD_DIM
    depth = ln1_g.shape[0]
    n_a = a_mix.shape[0]
    past = cache_k.shape[1]
    t_p = nb * seq
    t_s = db * dt
    n = t_p + t_s
    tm = _row_tile(n)
    c_p = 64
    c_s = dt
    tq_attn = 512 if t_p % 512 == 0 else 256
    assert seq % c_p == 0 and t_p % c_s == 0 and (c_s & (c_s - 1)) == 0
    alpha_dn = (2 * depth) ** 0.25
    row2 = lambda a: a.reshape(1, -1)

    bd = (jnp.arange(256)[:, None] // HEAD_DIM == jnp.arange(256)[None, :] // HEAD_DIM).astype(BF16)
    rw_t = router_w.T
    rb_c = router_b.reshape(-1, 1)

    x = jnp.concatenate([x_prompt.reshape(t_p, d), x_sample.reshape(t_s, d)], axis=0)
    new_shift_p, new_shift_s, new_wkv_p, new_wkv_s = [], [], [], []
    v_first = None
    kv = None
    for l in range(depth):
        if l < n_a:
            xs = x[t_p:].reshape(db, dt, d)
            new_shift_p.append(x[t_p - 1:t_p].reshape(nb, d))
            new_shift_s.append(xs[:, -1])
            xs_prev = jnp.concatenate([state_shift[l][:, None, :], xs[:, :-1]], axis=1).reshape(t_s, d)
            w = dict(mix=jnp.concatenate([a_mix[l], jnp.zeros((2, d), F32)], axis=0),
                     wr=a_w_rkv[l, 0].astype(BF16), wk=a_w_rkv[l, 1].astype(BF16), wv=a_w_rkv[l, 2].astype(BF16),
                     w1=a_w1[l].astype(BF16), w2=a_w2[l].astype(BF16), w0=row2(a_w0[l]),
                     a1=a_a1[l].astype(BF16), a2=a_a2[l].astype(BF16), a0=row2(a_a0[l]),
                     g1=a_g1[l].astype(BF16), g2=a_g2[l].astype(BF16),
                     k_k=row2(a_k_k[l]), k_a=row2(a_k_a[l]), bd=bd)
            if l > 0:
                w.update(v1=a_v1[l - 1].astype(BF16), v2=a_v2[l - 1].astype(BF16), v0=row2(a_v0[l - 1]))
            r, lw, k, v, kn, b, g = _rwkv_pre(x, xs_prev, v_first if l > 0 else None, w, tm)
            if l == 0:
                v_first = v
            scan_in = (r, lw, k, v, kn, b)
            rp, yp, mm, nn = _wkv_chunks(scan_in, c_p, 0, t_p // c_p, n_heads // 2)
            s0 = jnp.zeros((1, n_heads // 2, PAIR, PAIR), F32)
            y_p, sf_p = _wkv_seq(rp, yp, mm, nn, s0, c_p, 1, t_p // c_p)
            rp, yp, mm, nn = _wkv_chunks(scan_in, c_s, t_p, db, n_heads // 2)
            y_s, sf_s = _wkv_seq(rp, yp, mm, nn, _state_to_blockdiag(state_wkv[l]), c_s, db, 1)
            new_wkv_p.append(_blockdiag_to_state(sf_p))
            new_wkv_s.append(_blockdiag_to_state(sf_s))
            y = jnp.concatenate([y_p, y_s], axis=0)
            consts = [row2(a_lnx_g[l]), row2(a_lnx_b[l]), row2(a_r_k[l]), bd, a_w_o[l].astype(BF16),
                      row2(ln1_g[l]), row2(ln1_b[l]), rw_t, rb_c]
            x1, cls = _mix_post(True, alpha_dn, [x, y, r, k, v, g], consts, tm)
        else:
            lb = l - n_a
            wq = b_w_qg[lb][:, :d].astype(BF16)
            qb, q_aug, gate = _fox_qg(x, kv['rem_r'], wq, _pad_heads(wq, PAIR), b_w_qg[lb][:, d:].astype(BF16), tm)
            j_lo = _oldest_needed_block(qb, kv['k'], kv['rem_r'], kv['base'], t_p, tq_attn, 4)
            o_p = _fox_prompt(kv['base'], j_lo, q_aug, kv['k_aug'], kv['vt_aug'], t_p, tq_attn, 4).T
            o_s = _fox_sample(qb, kv['k'], kv['v'], kv['lf_s'], cache_k.reshape(db, past, d),
                              cache_v.reshape(db, past, d), cache_logf, t_p, db, dt)
            o = jnp.concatenate([o_p, o_s], axis=0)
            consts = [b_w_o[lb].astype(BF16), row2(ln1_g[l]), row2(ln1_b[l]), rw_t, rb_c]
            x1, cls = _mix_post(False, alpha_dn, [x, o, gate], consts, tm)
        m = _moe(x1, cls[0], moe_w_gu[l].astype(BF16), moe_w_down[l].astype(BF16))
        if l == n_a - 1:
            wk, wv = kv_w[:, :d].astype(BF16), kv_w[:, d:2 * d].astype(BF16)
            kvw = dict(wk=wk, wv=wv, wk_aug=_pad_heads(wk, PAIR), wv_aug=_pad_heads(wv, V_ROWS),
                       wf_t=kv_w[:, 2 * d:].T, bf=kv_bf.reshape(-1, 1), tiles_per_blk=tq_attn // tm)
            x, k_all, v_all, k_aug, vt_aug, lf_r, cum_r, rem_r = _ln2(
                alpha_dn, x1, m, row2(ln2_g[l]), row2(ln2_b[l]), kvw, tm)
            base = (cum_r[:, :t_p:tq_attn] * LOG2E).T.reshape(-1)
            kv = dict(k=k_all, v=v_all, base=base, rem_r=rem_r, k_aug=k_aug,
                      vt_aug=vt_aug.reshape(n_heads, V_ROWS, n), lf_p=lf_r[:, :t_p].T,
                      lf_s=lf_r[:, t_p:].T.reshape(db, dt, n_heads))
        else:
            x = _ln2(alpha_dn, x1, m, row2(ln2_g[l]), row2(ln2_b[l]), None, tm)[0]

    y_prompt = x[:t_p].reshape(nb, seq, d)
    y_sample = x[t_p:].reshape(db, dt, d)
    p_k = kv['k'][:t_p].reshape(nb, seq, n_heads, HEAD_DIM)
    p_v = kv['v'][:t_p].reshape(nb, seq, n_heads, HEAD_DIM)
    s_k = kv['k'][t_p:].reshape(db, dt, n_heads, HEAD_DIM)
    s_v = kv['v'][t_p:].reshape(db, dt, n_heads, HEAD_DIM)
    return (y_prompt, y_sample, jnp.stack(new_shift_p), jnp.stack(new_wkv_p), p_k, p_v,
            kv['lf_p'].reshape(nb, seq, n_heads), jnp.stack(new_shift_s), jnp.stack(new_wkv_s),
            s_k, s_v, kv['lf_s'])
```

```python
import functools
import math

import jax
import jax.numpy as jnp
import numpy as np
from jax import lax
from jax.experimental import pallas as pl
from jax.experimental.pallas import tpu as pltpu

F32 = jnp.float32
BF16 = jnp.bfloat16

HEAD_DIM = 64
PAIR = 2 * HEAD_DIM
N_EXPERTS = 16
N_GROUPS = 4
EXPERTS_PER_GROUP = 4
N_CLASSES = 24
E_BLOCK = 128
GATE_LANES = 128
LN_EPS = 1e-5
GN_EPS = 64e-5
NEG_INF = -1e30
LOG2E = 1.4426950408889634
MIB = 2 ** 20

_PAIR_LO = (0, 0, 0, 1, 1, 2)
_PAIR_HI = (1, 2, 3, 2, 3, 3)


def _params(sem, vmem_mib):
    return pltpu.CompilerParams(dimension_semantics=sem, vmem_limit_bytes=vmem_mib * MIB)


def _dot(a, b):
    return jnp.dot(a, b, preferred_element_type=F32)


def _dot_nt(a, b):
    return lax.dot_general(a, b, (((1,), (1,)), ((), ())), preferred_element_type=F32)


def _dot_tn(a, b):
    return lax.dot_general(a, b, (((0,), (0,)), ((), ())), preferred_element_type=F32)


def _split(x):
    hi = x.astype(BF16)
    lo = (x - hi.astype(F32)).astype(BF16)
    return hi, lo


def _dot3(a, b, dot=_dot):
    return _dot3_multi(a, [b], dot)[0]


def _dot3_multi(a, bs, dot=_dot):
    ah, al = a
    ax = 0 if dot is _dot_nt else 1
    ns = [b[0].shape[ax] for b in bs]
    if any(n % 128 for n in ns):
        return [dot(ah, bh) + (dot(ah, bl) + dot(al, bh)) for bh, bl in bs]
    kax = 0 if dot is _dot_tn else 1
    if ah.shape[kax] == 128:
        lhs = jnp.concatenate([ah, al], axis=kax)
        if dot is _dot_nt:
            rhs = jnp.concatenate([jnp.concatenate([x, y], axis=1) for bh, bl in bs
                                   for x, y in ((bh, bh), (bl, jnp.zeros_like(bl)))], axis=0)
        else:
            top = jnp.concatenate([x for b in bs for x in b], axis=1)
            bot = jnp.concatenate([x for bh, bl in bs for x in (bh, jnp.zeros_like(bl))], axis=1)
            rhs = jnp.concatenate([top, bot], axis=0)
        r = dot(lhs, rhs)
        outs, o = [], 0
        for n in ns:
            outs.append(r[:, o:o + n] + r[:, o + n:o + 2 * n])
            o += 2 * n
        return outs
    r1 = dot(ah, jnp.concatenate([x for b in bs for x in b], axis=ax))
    r2 = dot(al, jnp.concatenate([b[0] for b in bs], axis=ax)) if len(bs) > 1 else dot(al, bs[0][0])
    outs, o1, o2 = [], 0, 0
    for n in ns:
        outs.append((r1[:, o1:o1 + n] + r1[:, o1 + n:o1 + 2 * n]) + r2[:, o2:o2 + n])
        o1 += 2 * n
        o2 += n
    return outs


def _dot_exact_rhs(x, m, dot=_dot):
    hi = x.astype(BF16)
    r1 = x - hi.astype(F32)
    mid = r1.astype(BF16)
    lo = (r1 - mid.astype(F32)).astype(BF16)
    return dot(hi, m) + (dot(mid, m) + dot(lo, m))


def _head_sum(x, bd_ref):
    bd = bd_ref[...]
    parts = []
    for j in range(x.shape[1] // 256):
        parts.append(_dot_exact_rhs(x[:, 256 * j:256 * (j + 1)], bd))
    return jnp.concatenate(parts, axis=1)


def _layer_norm(z, g, b):
    mu = jnp.mean(z, axis=-1, keepdims=True)
    zc = z - mu
    var = jnp.mean(zc * zc, axis=-1, keepdims=True)
    return zc * lax.rsqrt(var + LN_EPS) * g + b


def _row_tile(n):
    for t in (256, 128, 64, 32, 16, 8):
        if n % t == 0:
            return t
    raise ValueError(n)


def _const_spec(shape):
    nd = len(shape)
    return pl.BlockSpec(shape, lambda *_: (0,) * nd)


def _rwkv_pre_kernel(has_vres, n_prompt_tiles, *refs):
    if has_vres:
        (x_ref, x8_ref, xtail_ref, vf_ref, mix_ref, wr_ref, wk_ref, wv_ref, w1_ref, w2_ref, w0_ref,
         a1_ref, a2_ref, a0_ref, g1_ref, g2_ref, v1_ref, v2_ref, v0_ref, kk_ref, ka_ref, bd_ref,
         r_out, lw_out, k_out, v_out, kn_out, b_out, g_out) = refs
    else:
        (x_ref, x8_ref, xtail_ref, mix_ref, wr_ref, wk_ref, wv_ref, w1_ref, w2_ref, w0_ref,
         a1_ref, a2_ref, a0_ref, g1_ref, g2_ref, kk_ref, ka_ref, bd_ref,
         r_out, lw_out, k_out, v_out, kn_out, b_out, g_out) = refs
    x = x_ref[...]
    i = pl.program_id(0)
    first = lax.broadcasted_iota(jnp.int32, x.shape, 0) == 0
    above = jnp.where(i == 0, 0.0, x8_ref[7:8, :])
    shifted = jnp.where(first, above, pltpu.roll(x, 1, 0))
    xx = jnp.where(i >= n_prompt_tiles, xtail_ref[...], shifted) - x

    def mixed(i):
        return (x + xx * mix_ref[i:i + 1, :]).astype(BF16)

    xr, xw, xk, xv, xa, xg = (mixed(i) for i in range(6))
    r = _dot(xr, wr_ref[...])
    k = _dot(xk, wk_ref[...])
    v = _dot(xv, wv_ref[...])
    zw = w0_ref[...] + _dot(jnp.tanh(_dot(xw, w1_ref[...])).astype(BF16), w2_ref[...])
    lw = (-math.exp(-0.5)) * jax.nn.sigmoid(zw)
    alpha = jax.nn.sigmoid(a0_ref[...] + _dot(_dot(xa, a1_ref[...]).astype(BF16), a2_ref[...]))
    g = _dot(jax.nn.sigmoid(_dot(xg, g1_ref[...])).astype(BF16), g2_ref[...])
    if has_vres:
        gate_v = jax.nn.sigmoid(v0_ref[...] + _dot(_dot(xv, v1_ref[...]).astype(BF16), v2_ref[...]))
        v = v + (vf_ref[...] - v) * gate_v
    kkr = k * kk_ref[...]
    norm = jnp.sqrt(_head_sum(kkr * kkr, bd_ref))
    kn = kkr / jnp.maximum(norm, 1e-12)
    k2 = k * (1.0 + (alpha - 1.0) * ka_ref[...])
    r_out[...] = r
    lw_out[...] = lw
    k_out[...] = k2
    v_out[...] = v
    kn_out[...] = kn
    b_out[...] = kn * alpha
    g_out[...] = g


def _rwkv_pre(x, x_tail_prev, vfirst, w, tm):
    n, d = x.shape
    t_tail = x_tail_prev.shape[0]
    assert t_tail % tm == 0 and (n - t_tail) % tm == 0
    n_prompt_tiles = (n - t_tail) // tm
    has_vres = vfirst is not None
    row = pl.BlockSpec((tm, d), lambda i: (i, 0))
    above = pl.BlockSpec((8, d), lambda i: (jnp.maximum(i * (tm // 8) - 1, 0), 0))
    tail = pl.BlockSpec((tm, d), lambda i: (jnp.maximum(i - n_prompt_tiles, 0), 0))
    ins = [x, x, x_tail_prev] + ([vfirst] if has_vres else [])
    specs = [row, above, tail] + ([row] if has_vres else [])
    names = ['mix', 'wr', 'wk', 'wv', 'w1', 'w2', 'w0', 'a1', 'a2', 'a0', 'g1', 'g2']
    if has_vres:
        names += ['v1', 'v2', 'v0']
    names += ['k_k', 'k_a', 'bd']
    for nm in names:
        ins.append(w[nm])
        specs.append(_const_spec(w[nm].shape))
    out = jax.ShapeDtypeStruct((n, d), F32)
    return pl.pallas_call(
        functools.partial(_rwkv_pre_kernel, has_vres, n_prompt_tiles),
        grid=(n // tm,),
        in_specs=specs,
        out_specs=[row] * 7,
        out_shape=[out] * 7,
        compiler_params=_params(("parallel",), 56),
        name="rwkv_pre",
    )(*ins)


def _wkv_chunk_kernel(C, n_pairs, r_ref, lw_ref, k_ref, v_ref, kn_ref, b_ref,
                      rp_out, yp_out, m_out, n_out):
    C2 = 2 * C
    row = lax.broadcasted_iota(jnp.int32, (C2, C2), 0)
    col = lax.broadcasted_iota(jnp.int32, (C2, C2), 1)
    same = (row >= C) == (col >= C)
    strict = jnp.logical_and(same, col < row)
    incl = jnp.logical_and(same, col <= row)
    tri_incl = jnp.where(incl, 1.0, 0.0).astype(BF16)
    eye_c = jnp.where(row == col, 1.0, 0.0).astype(F32)
    r128 = lax.broadcasted_iota(jnp.int32, (PAIR, PAIR), 0)
    c128 = lax.broadcasted_iota(jnp.int32, (PAIR, PAIR), 1)
    eye_p = r128 == c128
    head0 = lax.broadcasted_iota(jnp.int32, (1, PAIR), 1) < HEAD_DIM

    def stack(t):
        return jnp.concatenate([jnp.where(head0, t, 0.0), jnp.where(head0, 0.0, t)], axis=0)

    n_sq = int(math.log2(C)) - 1
    sls = [slice(PAIR * p, PAIR * (p + 1)) for p in range(n_pairs)]

    def each(f, *lists):
        return [f(*a) for a in zip(*lists)]

    def load(ref):
        return [stack(ref[:, sl]) for sl in sls]

    lws = load(lw_ref)
    L = each(lambda x: _dot_exact_rhs_left(tri_incl, x), lws)
    lc = each(lambda l: l[C - 1:C, :] + l[C2 - 1:C2, :], L)
    kn, bs, ks = load(kn_ref), load(b_ref), load(k_ref)
    at = each(lambda n, l, w: _split(-n * jnp.exp(l - w)), kn, L, lws)
    rt_f = each(lambda r, l: r * jnp.exp(l), load(r_ref), L)
    rt = each(_split, rt_f)
    e_nl = each(lambda l: jnp.exp(-l), L)
    bb = each(lambda b, e: _split(b * e), bs, e_nl)
    kb = each(lambda k, e: _split(k * e), ks, e_nl)
    e_lc = each(lambda c, l: jnp.exp(c - l), lc, L)
    bh = each(lambda b, e: _split(b * e), bs, e_lc)
    kh = each(lambda k, e: _split(k * e), ks, e_lc)
    vs = each(_split, load(v_ref))

    g_a = each(lambda a, b, k: _dot3_multi(a, [b, k], _dot_nt), at, bb, kb)
    g_r = each(lambda a, b, k: _dot3_multi(a, [b, k], _dot_nt), rt, bb, kb)
    a_ab = [jnp.where(strict, g[0], 0.0) for g in g_a]
    a_ak = [_split(jnp.where(strict, g[1], 0.0)) for g in g_a]
    a_rb = [_split(jnp.where(incl, g[0], 0.0)) for g in g_r]
    a_rk = [_split(jnp.where(incl, g[1], 0.0)) for g in g_r]

    pws = each(lambda a: _split(_dot3(a, a)), each(_split, a_ab))
    tm = each(lambda a: eye_c + a, a_ab)
    for step in range(n_sq):
        if step < n_sq - 1:
            res = each(lambda s, t: _dot3_multi(s, [_split(t), s]), pws, tm)
            tm = [t + r[0] for t, r in zip(tm, res)]
            pws = [_split(r[1]) for r in res]
        else:
            tm = each(lambda t, s: t + _dot3(s, _split(t)), tm, pws)
    tms = each(_split, tm)
    w1 = each(lambda a, v: _split(_dot3(a, v)), a_ak, vs)
    tx = each(lambda t, a, w: _dot3_multi(t, [a, w]), tms, at, w1)
    at2 = [_split(r[0]) for r in tx]
    v2 = [_split(r[1]) for r in tx]
    ax = each(lambda a, x, v: _dot3_multi(a, [x, v]), a_rb, at2, v2)
    r2 = [r + x[0] for r, x in zip(rt_f, ax)]
    y2 = [x[1] + _dot3(c, v) for x, c, v in zip(ax, a_rk, vs)]
    bx = each(lambda b, x, v: _dot3_multi(b, [x, v], _dot_tn), bh, at2, v2)
    mm = [jnp.where(eye_p, jnp.exp(c), 0.0) + x[0] for c, x in zip(lc, bx)]
    nn = [x[1] + _dot3(k, v, _dot_tn) for x, k, v in zip(bx, kh, vs)]
    rp_out[...] = jnp.concatenate([x[:C, :] + x[C:, :] for x in r2], axis=1)
    yp_out[...] = jnp.concatenate([x[:C, :] + x[C:, :] for x in y2], axis=1)
    m_out[0] = jnp.stack(mm, axis=0)
    n_out[0] = jnp.stack(nn, axis=0)


def _dot_exact_rhs_left(m, x):
    hi = x.astype(BF16)
    r1 = x - hi.astype(F32)
    mid = r1.astype(BF16)
    lo = (r1 - mid.astype(F32)).astype(BF16)
    return _dot(m, hi) + (_dot(m, mid) + _dot(m, lo))


def _wkv_chunks(arrs, C, row0, n_chunks, pairs_per_step):
    d = arrs[0].shape[1]
    n_pairs = d // PAIR
    assert row0 % C == 0 and n_pairs % pairs_per_step == 0
    blk0 = row0 // C
    lanes = pairs_per_step * PAIR
    in_spec = pl.BlockSpec((C, lanes), lambda c, q: (blk0 + c, q))
    out_row = pl.BlockSpec((C, lanes), lambda c, q: (c, q))
    out_mat = pl.BlockSpec((1, pairs_per_step, PAIR, PAIR), lambda c, q: (c, q, 0, 0))
    t = n_chunks * C
    return pl.pallas_call(
        functools.partial(_wkv_chunk_kernel, C, pairs_per_step),
        grid=(n_chunks, n_pairs // pairs_per_step),
        in_specs=[in_spec] * 6,
        out_specs=[out_row, out_row, out_mat, out_mat],
        out_shape=[jax.ShapeDtypeStruct((t, d), F32), jax.ShapeDtypeStruct((t, d), F32),
                   jax.ShapeDtypeStruct((n_chunks, n_pairs, PAIR, PAIR), F32),
                   jax.ShapeDtypeStruct((n_chunks, n_pairs, PAIR, PAIR), F32)],
        compiler_params=_params(("parallel", "parallel"), 32),
        name=f"wkv_chunks_c{C}",
    )(*arrs)


def _wkv_seq_kernel(n_pairs, n_steps, rp_ref, yp_ref, m_ref, n_ref, s0_ref, y_out, s_out, s_scr):
    j = pl.program_id(1)

    @pl.when(j == 0)
    def _():
        s_scr[...] = s0_ref[0]

    ys, new_s = [], []
    for p in range(n_pairs):
        sl = slice(PAIR * p, PAIR * (p + 1))
        ss = _split(s_scr[p])
        ys.append(_dot3(_split(rp_ref[:, sl]), ss) + yp_ref[:, sl])
        new_s.append(_dot3(_split(m_ref[0, p]), ss) + n_ref[0, p])
    y_out[...] = jnp.concatenate(ys, axis=1)
    s_scr[...] = jnp.stack(new_s, axis=0)

    @pl.when(j == n_steps - 1)
    def _():
        s_out[0] = s_scr[...]


def _wkv_seq(rp, yp, m, nn, s0, C, n_seq, n_steps):
    t, d = rp.shape
    n_pairs = d // PAIR
    row = pl.BlockSpec((C, d), lambda s, j: (s * n_steps + j, 0))
    mat = pl.BlockSpec((1, n_pairs, PAIR, PAIR), lambda s, j: (s * n_steps + j, 0, 0, 0))
    st = pl.BlockSpec((1, n_pairs, PAIR, PAIR), lambda s, j: (s, 0, 0, 0))
    return pl.pallas_call(
        functools.partial(_wkv_seq_kernel, n_pairs, n_steps),
        grid=(n_seq, n_steps),
        in_specs=[row, row, mat, mat, st],
        out_specs=[row, st],
        out_shape=[jax.ShapeDtypeStruct((t, d), F32),
                   jax.ShapeDtypeStruct((n_seq, n_pairs, PAIR, PAIR), F32)],
        scratch_shapes=[pltpu.VMEM((n_pairs, PAIR, PAIR), F32)],
        compiler_params=_params(("arbitrary", "arbitrary"), 32),
        name=f"wkv_seq_c{C}",
    )(rp, yp, m, nn, s0)


def _state_to_blockdiag(s):
    b, h, n, _ = s.shape
    st = jnp.swapaxes(s, -1, -2).reshape(b, h // 2, 2, n, n)
    z = jnp.zeros_like(st[:, :, 0])
    top = jnp.concatenate([st[:, :, 0], z], axis=-1)
    bot = jnp.concatenate([z, st[:, :, 1]], axis=-1)
    return jnp.concatenate([top, bot], axis=-2)


def _blockdiag_to_state(bd):
    b, hp, _, _ = bd.shape
    n = HEAD_DIM
    s0 = bd[:, :, :n, :n]
    s1 = bd[:, :, n:, n:]
    st = jnp.stack([s0, s1], axis=2).reshape(b, hp * 2, n, n)
    return jnp.swapaxes(st, -1, -2)


def _route(x1, rw_ref, rb_ref):
    logits = _dot3(_split(rw_ref[...]), _split(x1), _dot_nt)
    mx = jnp.max(logits, axis=0, keepdims=True)
    ex = jnp.exp(logits - mx)
    scores = ex / jnp.sum(ex, axis=0, keepdims=True)
    sel = scores + rb_ref[...]
    rows = [sel[e:e + 1, :] for e in range(N_EXPERTS)]
    srow = [scores[e:e + 1, :] for e in range(N_EXPERTS)]

    def top2(vals):
        m1 = jnp.maximum(jnp.maximum(vals[0], vals[1]), jnp.maximum(vals[2], vals[3]))
        i1 = jnp.where(vals[0] == m1, 0, jnp.where(vals[1] == m1, 1, jnp.where(vals[2] == m1, 2, 3)))
        rest = [jnp.where(i1 == j, -jnp.inf, vals[j]) for j in range(4)]
        m2 = jnp.maximum(jnp.maximum(rest[0], rest[1]), jnp.maximum(rest[2], rest[3]))
        i2 = jnp.where(rest[0] == m2, 0, jnp.where(rest[1] == m2, 1, jnp.where(rest[2] == m2, 2, 3)))
        return m1, i1, m2, i2

    gscore = []
    for gidx in range(N_GROUPS):
        m1, _, m2, _ = top2(rows[4 * gidx:4 * gidx + 4])
        gscore.append(m1 + m2)
    gm = jnp.maximum(jnp.maximum(gscore[0], gscore[1]), jnp.maximum(gscore[2], gscore[3]))
    gi = jnp.where(gscore[0] == gm, 0, jnp.where(gscore[1] == gm, 1, jnp.where(gscore[2] == gm, 2, 3)))

    def pick(rws, j):
        return jnp.where(gi == 0, rws[j], jnp.where(gi == 1, rws[4 + j],
                                                    jnp.where(gi == 2, rws[8 + j], rws[12 + j])))

    in_grp = [pick(rows, j) for j in range(4)]
    sc_grp = [pick(srow, j) for j in range(4)]
    _, i1, _, i2 = top2(in_grp)

    def at(vals, idx):
        return jnp.where(idx == 0, vals[0], jnp.where(idx == 1, vals[1],
                                                      jnp.where(idx == 2, vals[2], vals[3])))

    ga = at(sc_grp, i1)
    gb = at(sc_grp, i2)
    tot = ga + gb
    ga = ga / tot
    gb = gb / tot
    lo = jnp.minimum(i1, i2)
    hi = jnp.maximum(i1, i2)
    g_lo = jnp.where(i1 < i2, ga, gb)
    g_hi = jnp.where(i1 < i2, gb, ga)
    pair = jnp.where(lo == 0, hi - 1, jnp.where(lo == 1, hi + 1, 5))
    cls = gi * 6 + pair
    return cls.astype(jnp.int32), g_lo, g_hi


def _mix_post_kernel(is_rwkv, alpha_dn, *refs):
    if is_rwkv:
        (x_ref, y_ref, r_ref, k_ref, v_ref, g_ref, lg_ref, lb_ref, rk_ref, bd_ref,
         wo_ref, n1g_ref, n1b_ref, rw_ref, rb_ref, x1_out, cls_out) = refs
        y = y_ref[...]
        mu = _head_sum(y, bd_ref) * (1.0 / HEAD_DIM)
        yc = y - mu
        var = _head_sum(yc * yc, bd_ref) * (1.0 / HEAD_DIM)
        yn = yc * lax.rsqrt(var + GN_EPS) * lg_ref[...] + lb_ref[...]
        v = v_ref[...]
        bonus = _head_sum(r_ref[...] * k_ref[...] * rk_ref[...], bd_ref)
        z = (yn + bonus * v) * g_ref[...]
    else:
        (x_ref, o_ref, gt_ref, wo_ref, n1g_ref, n1b_ref, rw_ref, rb_ref,
         x1_out, cls_out) = refs
        z = o_ref[...] * gt_ref[...]
    h = _dot(z.astype(BF16), wo_ref[...])
    x1 = _layer_norm(alpha_dn * x_ref[...] + h, n1g_ref[...], n1b_ref[...])
    cls, g_lo, g_hi = _route(x1, rw_ref, rb_ref)
    tm, d = x1.shape
    cls_out[...] = jnp.broadcast_to(cls, (8, tm))
    pieces = [x.astype(F32) for x in _split3(g_lo) + _split3(g_hi)]
    gs = jnp.concatenate(pieces + [jnp.zeros((16 - len(pieces), tm), F32)], axis=0).astype(BF16)
    rr = lax.broadcasted_iota(jnp.int32, (16, GATE_LANES), 0)
    cc = lax.broadcasted_iota(jnp.int32, (16, GATE_LANES), 1)
    place = jnp.where(jnp.logical_and(rr < 6, cc == jnp.where(rr < 3, 0, 1)), 1.0, 0.0).astype(BF16)
    x1_out[:, :d] = x1
    x1_out[:, d:] = _dot_tn(gs, place)


def _mix_post(is_rwkv, alpha_dn, acts, consts, tm):
    n, d = acts[0].shape
    row = pl.BlockSpec((tm, d), lambda i: (i, 0))
    specs = [row] * len(acts) + [_const_spec(c.shape) for c in consts]
    lane = pl.BlockSpec((8, tm), lambda i: (0, i))
    wide = pl.BlockSpec((tm, d + GATE_LANES), lambda i: (i, 0))
    return pl.pallas_call(
        functools.partial(_mix_post_kernel, is_rwkv, alpha_dn),
        grid=(n // tm,),
        in_specs=specs,
        out_specs=[wide, lane],
        out_shape=[jax.ShapeDtypeStruct((n, d + GATE_LANES), F32), jax.ShapeDtypeStruct((8, n), jnp.int32)],
        compiler_params=_params(("parallel",), 48),
        name="mix_post_rwkv" if is_rwkv else "mix_post_fox",
    )(*acts, *consts)


def _moe_kernel(n_tok, d, d_exp, n_blk, e1_ref, e2_ref, off_ref, nv_ref, order_ref, x_hbm,
                wgu1_ref, wd1_ref, wgu2_ref, wd2_ref, out_hbm, xbuf, obuf, gsem, ssem):
    b = pl.program_id(0)
    slot = lax.rem(b, 2)
    nv_b = nv_ref[b]
    unroll = 8

    def gather_start(blk, sl):
        base = off_ref[blk]

        def body(c, carry):
            for u in range(unroll):
                i = c * unroll + u
                tok = order_ref[jnp.minimum(base + i, n_tok - 1)]
                pltpu.make_async_copy(x_hbm.at[pl.ds(tok, 1)], xbuf.at[sl, pl.ds(i, 1)], gsem.at[sl]).start()
            return carry

        lax.fori_loop(0, E_BLOCK // unroll, body, 0)

    def gather_wait(sl):
        pltpu.make_async_copy(x_hbm.at[pl.ds(0, E_BLOCK)], xbuf.at[sl], gsem.at[sl]).wait()

    def scatter_start(blk, sl):
        base = off_ref[blk]

        def body(i, carry):
            tok = order_ref[base + i]
            pltpu.make_async_copy(obuf.at[sl, pl.ds(i, 1)], out_hbm.at[pl.ds(tok, 1)], ssem.at[sl]).start()
            return carry

        lax.fori_loop(0, nv_ref[blk], body, 0)

    def scatter_wait(blk, sl):
        nv = nv_ref[blk]

        @pl.when(nv == E_BLOCK)
        def _():
            pltpu.make_async_copy(obuf.at[sl], out_hbm.at[pl.ds(0, E_BLOCK)], ssem.at[sl]).wait()

        @pl.when(nv < E_BLOCK)
        def _():
            def body(i, carry):
                pltpu.make_async_copy(obuf.at[sl, pl.ds(0, 1)], out_hbm.at[pl.ds(0, 1)], ssem.at[sl]).wait()
                return carry

            lax.fori_loop(0, nv, body, 0)

    @pl.when(jnp.logical_and(b == 0, nv_b > 0))
    def _():
        gather_start(0, 0)

    nxt = jnp.minimum(b + 1, n_blk - 1)

    @pl.when(jnp.logical_and(b + 1 < n_blk, nv_ref[nxt] > 0))
    def _():
        gather_start(nxt, 1 - slot)

    @pl.when(nv_b > 0)
    def _():
        gather_wait(slot)
        rows = xbuf[slot]
        xb = rows[:, :d].astype(BF16)

        def expert(wgu_ref, wd_ref):
            gu = _dot(xb, wgu_ref[0])
            hmid = jax.nn.silu(gu[:, :d_exp]) * gu[:, d_exp:]
            return _dot(hmid.astype(BF16), wd_ref[0])

        y1 = expert(wgu1_ref, wd1_ref)
        y2 = expert(wgu2_ref, wd2_ref)
        obuf[slot] = y1 * rows[:, d:d + 1] + y2 * rows[:, d + 1:d + 2]

    prev = jnp.maximum(b - 1, 0)

    @pl.when(jnp.logical_and(b > 0, nv_ref[prev] > 0))
    def _():
        scatter_wait(prev, 1 - slot)

    @pl.when(nv_b > 0)
    def _():
        scatter_start(b, slot)

    @pl.when(jnp.logical_and(b == n_blk - 1, nv_b > 0))
    def _():
        scatter_wait(b, slot)


def _moe(xg, cls, w_gu, w_down):
    n = xg.shape[0]
    d, d_exp = w_down.shape[2], w_down.shape[1]
    n_blk = (n + N_CLASSES * (E_BLOCK - 1) + E_BLOCK - 1) // E_BLOCK
    counts = jnp.sum((cls[:, None] == jnp.arange(N_CLASSES, dtype=jnp.int32)[None, :]).astype(jnp.int32), axis=0)
    padded = (counts + E_BLOCK - 1) // E_BLOCK * E_BLOCK
    pad_end = jnp.cumsum(padded)
    pad_start = pad_end - padded
    seg_start = jnp.cumsum(counts) - counts
    order = jnp.argsort(cls).astype(jnp.int32)
    blk_start = jnp.arange(n_blk, dtype=jnp.int32) * E_BLOCK
    blk_cls = jnp.minimum(jnp.sum((pad_end[None, :] <= blk_start[:, None]).astype(jnp.int32), axis=1),
                          N_CLASSES - 1)
    classes = jnp.arange(N_CLASSES, dtype=jnp.int32)

    def lookup(table, idx):
        return jnp.sum(jnp.where(idx[:, None] == classes[None, :], table[None, :], 0), axis=1)

    rank0 = blk_start - lookup(pad_start, blk_cls)
    blk_nv = jnp.clip(lookup(counts, blk_cls) - rank0, 0, E_BLOCK).astype(jnp.int32)
    blk_off = (lookup(seg_start, blk_cls) + rank0).astype(jnp.int32)
    last_cls = jnp.max(jnp.where(blk_nv > 0, blk_cls, 0))
    blk_cls = jnp.where(blk_nv > 0, blk_cls, last_cls)
    e_lo = jnp.asarray([(c // 6) * EXPERTS_PER_GROUP + _PAIR_LO[c % 6] for c in range(N_CLASSES)], jnp.int32)
    e_hi = jnp.asarray([(c // 6) * EXPERTS_PER_GROUP + _PAIR_HI[c % 6] for c in range(N_CLASSES)], jnp.int32)
    blk_e1 = lookup(e_lo, blk_cls)
    blk_e2 = lookup(e_hi, blk_cls)

    gu1 = pl.BlockSpec((1, d, 2 * d_exp), lambda b, e1, e2, of, nv, od: (e1[b], 0, 0))
    dn1 = pl.BlockSpec((1, d_exp, d), lambda b, e1, e2, of, nv, od: (e1[b], 0, 0))
    gu2 = pl.BlockSpec((1, d, 2 * d_exp), lambda b, e1, e2, of, nv, od: (e2[b], 0, 0))
    dn2 = pl.BlockSpec((1, d_exp, d), lambda b, e1, e2, of, nv, od: (e2[b], 0, 0))
    grid_spec = pltpu.PrefetchScalarGridSpec(
        num_scalar_prefetch=5,
        grid=(n_blk,),
        in_specs=[pl.BlockSpec(memory_space=pl.ANY), gu1, dn1, gu2, dn2],
        out_specs=pl.BlockSpec(memory_space=pl.ANY),
        scratch_shapes=[pltpu.VMEM((2, E_BLOCK, d + GATE_LANES), F32), pltpu.VMEM((2, E_BLOCK, d), F32),
                        pltpu.SemaphoreType.DMA((2,)), pltpu.SemaphoreType.DMA((2,))],
    )
    return pl.pallas_call(
        functools.partial(_moe_kernel, n, d, d_exp, n_blk),
        grid_spec=grid_spec,
        out_shape=jax.ShapeDtypeStruct((n, d), F32),
        compiler_params=_params(("arbitrary",), 40),
        name="moe_experts",
    )(blk_e1, blk_e2, blk_off, blk_nv, order, xg, w_gu, w_down, w_gu, w_down)


def _bias_placement(nh, lead_ones):
    p = np.zeros((4 * nh, nh * PAIR), np.float32)
    for h in range(nh):
        lane0 = PAIR * h + HEAD_DIM
        if lead_ones:
            p[h, lane0:lane0 + 3] = 1.0
            for piece in range(3):
                p[(piece + 1) * nh + h, lane0 + 3 + piece] = 1.0
        else:
            for piece in range(3):
                p[piece * nh + h, lane0 + piece] = 1.0
            p[3 * nh + h, lane0 + 3:lane0 + 6] = 1.0
    return jnp.asarray(p, BF16)


def _pad_heads(w, width):
    d, hd = w.shape
    nh = hd // HEAD_DIM
    w3 = w.reshape(d, nh, HEAD_DIM)
    return jnp.concatenate([w3, jnp.zeros((d, nh, width - HEAD_DIM), w.dtype)], axis=-1).reshape(d, nh * width)


def _bias_rows(rem, lead_ones):
    parts = [x.astype(F32) for x in _split3(rem)]
    ones = [jnp.ones_like(rem)]
    return jnp.concatenate(ones + parts if lead_ones else parts + ones, axis=0).astype(BF16)


def _ln2_kernel(alpha_dn, with_kv, tiles_per_blk, *refs):
    if with_kv:
        (x1_ref, m_ref, g_ref, b_ref, wk_ref, wv_ref, wka_ref, wva_ref, wf_ref, bf_ref, tri_ref,
         pk_ref, eye_ref, ones_ref,
         x2_out, k_out, v_out, ka_out, vt_out, lf_out, cum_out, rem_out, carry, base_scr) = refs
    else:
        x1_ref, m_ref, g_ref, b_ref, x2_out = refs
    x2 = _layer_norm(alpha_dn * x1_ref[...] + m_ref[...], g_ref[...], b_ref[...])
    x2_out[...] = x2
    if with_kv:
        i = pl.program_id(0)
        xb = x2.astype(BF16)
        k_out[...] = _dot(xb, wk_ref[...])
        v_out[...] = _dot(xb, wv_ref[...])
        z = _dot3(_split(wf_ref[...]), _split(x2), _dot_nt) + bf_ref[...]
        lf = jnp.minimum(z, 0.0) - jnp.log1p(jnp.exp(-jnp.abs(z)))
        lf_out[...] = lf

        @pl.when(i == 0)
        def _():
            carry[...] = jnp.zeros_like(carry)

        cum = _dot_exact_rhs(lf, tri_ref[...]) + carry[...]
        cum_out[...] = cum
        carry[...] = cum[:, -1:]

        cum2 = cum * LOG2E

        @pl.when(lax.rem(i, tiles_per_blk) == 0)
        def _():
            base_scr[...] = cum2[:, :1]

        rem = cum2 - base_scr[...]
        rem_out[...] = rem
        ka_out[...] = (_dot(xb, wka_ref[...]) + _dot_tn(_bias_rows(-rem, False), pk_ref[...])).astype(BF16)
        vw = _dot(xb, wva_ref[...]).astype(BF16)
        vt_out[...] = (_dot_tn(vw, eye_ref[...]) + ones_ref[...]).astype(BF16)


def _ln2(alpha_dn, x1, m, g, b, kv, tm):
    n, d = m.shape
    row = pl.BlockSpec((tm, d), lambda i: (i, 0))
    with_kv = kv is not None
    ins = [x1, m, g, b]
    specs = [row, row, _const_spec(g.shape), _const_spec(b.shape)]
    outs = [jax.ShapeDtypeStruct((n, d), F32)]
    ospecs = [row]
    scratch = []
    if with_kv:
        nh = kv['wf_t'].shape[0]
        tri = (jnp.arange(tm)[:, None] <= jnp.arange(tm)[None, :]).astype(BF16)
        eye = jnp.eye(tm, dtype=BF16)
        ones_col = (jnp.arange(nh * V_ROWS) % V_ROWS == HEAD_DIM).astype(F32).reshape(-1, 1)
        extra = [kv['wk'], kv['wv'], kv['wk_aug'], kv['wv_aug'], kv['wf_t'], kv['bf'], tri,
                 _bias_placement(nh, False), eye, ones_col]
        ins += extra
        specs += [_const_spec(e.shape) for e in extra]
        lane = pl.BlockSpec((nh, tm), lambda i: (0, i))
        outs += [jax.ShapeDtypeStruct((n, d), F32), jax.ShapeDtypeStruct((n, d), F32),
                 jax.ShapeDtypeStruct((n, nh * PAIR), BF16), jax.ShapeDtypeStruct((nh * V_ROWS, n), BF16),
                 jax.ShapeDtypeStruct((nh, n), F32), jax.ShapeDtypeStruct((nh, n), F32),
                 jax.ShapeDtypeStruct((nh, n), F32)]
        ospecs += [row, row, pl.BlockSpec((tm, nh * PAIR), lambda i: (i, 0)),
                   pl.BlockSpec((nh * V_ROWS, tm), lambda i: (0, i)), lane, lane, lane]
        scratch = [pltpu.VMEM((nh, 1), F32), pltpu.VMEM((nh, 1), F32)]
    res = pl.pallas_call(
        functools.partial(_ln2_kernel, alpha_dn, with_kv, kv['tiles_per_blk'] if with_kv else 1),
        grid=(n // tm,),
        in_specs=specs,
        out_specs=ospecs,
        out_shape=outs,
        scratch_shapes=scratch,
        compiler_params=_params(("arbitrary",), 48),
        name="ln2_kv" if with_kv else "ln2",
    )(*ins)
    return res


def _fox_qg_kernel(x_ref, rem_ref, wq_ref, wqa_ref, wg_ref, pq_ref, q_out, qa_out, gate_out):
    xb = x_ref[...].astype(BF16)
    scale = LOG2E * HEAD_DIM ** -0.5
    q_out[...] = (_dot(xb, wq_ref[...]) * scale).astype(BF16)
    qa_out[...] = (_dot(xb, wqa_ref[...]) * scale
                   + _dot_tn(_bias_rows(rem_ref[...], True), pq_ref[...])).astype(BF16)
    gate_out[...] = jax.nn.sigmoid(_dot(xb, wg_ref[...]))


def _fox_qg(x, rem_r, wq, wq_aug, wg, tm):
    n, d = x.shape
    nh = rem_r.shape[0]
    row = pl.BlockSpec((tm, d), lambda i: (i, 0))
    pq = _bias_placement(nh, True)
    return pl.pallas_call(
        _fox_qg_kernel,
        grid=(n // tm,),
        in_specs=[row, pl.BlockSpec((nh, tm), lambda i: (0, i)), _const_spec(wq.shape),
                  _const_spec(wq_aug.shape), _const_spec(wg.shape), _const_spec(pq.shape)],
        out_specs=[row, pl.BlockSpec((tm, nh * PAIR), lambda i: (i, 0)), row],
        out_shape=[jax.ShapeDtypeStruct((n, d), BF16), jax.ShapeDtypeStruct((n, nh * PAIR), BF16),
                   jax.ShapeDtypeStruct((n, d), F32)],
        compiler_params=_params(("parallel",), 48),
        name="fox_qg",
    )(x, rem_r, wq, wq_aug, wg, pq)


V_ROWS = 80


def _fox_prompt_kernel(tq, hps, nh, base_ref, jlo_ref, q_ref, k_ref, vt_ref, o_out, s_scr, p_scr):
    g = pl.program_id(0)
    i = pl.program_id(1)
    j_lo = jlo_ref[g * pl.num_programs(1) + i]
    rr = lax.broadcasted_iota(jnp.int32, (tq, tq), 0)
    cc = lax.broadcasted_iota(jnp.int32, (tq, tq), 1)
    causal = rr <= cc

    def scores(j, h):
        k0 = pl.multiple_of(j * tq, tq)
        lanes = slice(PAIR * h, PAIR * (h + 1))
        return _dot_nt(k_ref[pl.ds(k0, tq), lanes], q_ref[:, lanes])

    def base_gap(j, h):
        head = g * hps + h
        return base_ref[i * nh + head] - base_ref[j * nh + head]

    def softmax(s, m, gap):
        m_new = jnp.maximum(m, jnp.max(s, axis=0, keepdims=True) + gap)
        return jnp.exp2(s - (m_new - gap)).astype(BF16), jnp.exp2(m - m_new), m_new

    def accum(acc, corr, p, j, h):
        k0 = pl.multiple_of(j * tq, tq)
        return acc * corr + _dot(vt_ref[h, :, pl.ds(k0, tq)], p)

    init = []
    for h in range(hps):
        s = jnp.where(causal, scores(i, h), NEG_INF)
        m0 = jnp.max(s, axis=0, keepdims=True)
        p_scr[h] = jnp.exp2(s - m0).astype(BF16)
        s_scr[h] = scores(jnp.maximum(i - 1, 0), h)
        init.append((m0, jnp.zeros((V_ROWS, tq), F32), jnp.ones((1, tq), F32)))

    def body(t, carry):
        j = i - 1 - t
        new = []
        for h in range(hps):
            m, acc, corr_prev = carry[h]
            acc = accum(acc, corr_prev, p_scr[h], j + 1, h)
            p, corr, m = softmax(s_scr[h], m, base_gap(j, h))
            p_scr[h] = p
            s_scr[h] = scores(jnp.maximum(j - 1, 0), h)
            new.append((m, acc, corr))
        return tuple(new)

    carry = lax.fori_loop(0, i - j_lo, body, tuple(init))
    outs = []
    for h in range(hps):
        m, acc, corr_prev = carry[h]
        acc = accum(acc, corr_prev, p_scr[h], j_lo, h)
        outs.append(acc[:HEAD_DIM, :] / acc[HEAD_DIM:HEAD_DIM + 1, :])
    o_out[...] = jnp.concatenate(outs, axis=0)


UNDERFLOW_LOG2 = -160.0


def _oldest_needed_block(qb, k, rem_r, base, t, tq, hps):
    d = qb.shape[1]
    nh = d // HEAD_DIM
    nb = t // tq
    slack = 2.0
    q = qb[:t].astype(F32).reshape(nb, tq, nh, HEAD_DIM)
    kk = k[:t].astype(BF16).astype(F32).reshape(nb, tq, nh, HEAD_DIM)
    rem = rem_r[:, :t].T.reshape(nb, tq, nh)
    q_norm = jnp.sqrt(jnp.max(jnp.sum(q * q, axis=-1), axis=1))
    k_norm = jnp.sqrt(jnp.max(jnp.sum(kk * kk, axis=-1), axis=(0, 1)))
    self_min = jnp.min(jnp.sum(q * kk, axis=-1), axis=1)
    s_max = q_norm * k_norm + jnp.max(rem, axis=1) + jnp.max(-rem, axis=(0, 1)) + slack
    m_low = self_min - slack
    b = base.reshape(nb, nh)
    gap = b[:, None, :] - b[None, :, :]
    older = jnp.arange(nb)[None, :, None] < jnp.arange(nb)[:, None, None]
    alive = jnp.logical_and(older, gap + (s_max - m_low)[:, None, :] >= UNDERFLOW_LOG2)
    idx = jnp.arange(nb, dtype=jnp.int32)
    j_lo = jnp.min(jnp.where(alive, idx[None, :, None], idx[:, None, None]), axis=1)
    return jnp.min(j_lo.reshape(nb, nh // hps, hps), axis=-1).T.reshape(-1).astype(jnp.int32)


def _fox_prompt(base, j_lo, q_aug, k_aug, vt_aug, t, tq, hps):
    nh = vt_aug.shape[0]
    return pl.pallas_call(
        functools.partial(_fox_prompt_kernel, tq, hps, nh),
        grid=(nh // hps, t // tq),
        in_specs=[pl.BlockSpec(memory_space=pltpu.SMEM), pl.BlockSpec(memory_space=pltpu.SMEM),
                  pl.BlockSpec((tq, hps * PAIR), lambda g, i: (i, g)),
                  pl.BlockSpec((t, hps * PAIR), lambda g, i: (0, g), pipeline_mode=pl.Buffered(1)),
                  pl.BlockSpec((hps, V_ROWS, t), lambda g, i: (g, 0, 0), pipeline_mode=pl.Buffered(1))],
        out_specs=pl.BlockSpec((hps * HEAD_DIM, tq), lambda g, i: (g, i)),
        out_shape=jax.ShapeDtypeStruct((nh * HEAD_DIM, t), F32),
        scratch_shapes=[pltpu.VMEM((hps, tq, tq), F32), pltpu.VMEM((hps, tq, tq), BF16)],
        compiler_params=_params(("parallel", "parallel"), 48),
        name="fox_prompt_attn",
    )(base, j_lo, q_aug, k_aug, vt_aug)


def _fox_sample_kernel(n_pairs, q_ref, kn_ref, vn_ref, lfn_ref, kc_ref, vc_ref, lfc_ref,
                       triu_ref, o_out):
    t = q_ref.shape[0]
    past = kc_ref.shape[1]
    lfc = lfc_ref[0]
    lfn = lfn_ref[0]
    nh = lfc.shape[1]
    lfc_s = _split3(lfc)
    cum_c_r = _dot_tn3(lfc_s, triu_ref[...])
    ones_row = jnp.ones((1, past), BF16)
    tot_r = _dot_m3(ones_row, lfc_s)
    tot_c = cum_c_r[:, past - 1:past]
    rr = lax.broadcasted_iota(jnp.int32, (t, t), 0)
    cc = lax.broadcasted_iota(jnp.int32, (t, t), 1)
    causal = rr >= cc
    tril = jnp.where(causal, 1.0, 0.0).astype(BF16)
    lfn_s = _split3(lfn)
    cum_n_c = (_dot_m3(tril, lfn_s) + tot_r) * LOG2E
    cum_n_r = (_dot_tn3(lfn_s, jnp.where(rr <= cc, 1.0, 0.0).astype(BF16)) + tot_c) * LOG2E
    cum_c_r = cum_c_r * LOG2E
    lane = lax.broadcasted_iota(jnp.int32, (1, PAIR), 1)
    head0 = lane < HEAD_DIM
    hl = lax.broadcasted_iota(jnp.int32, (1, nh), 1)
    for p in range(n_pairs):
        sl = slice(PAIR * p, PAIR * (p + 1))
        q = q_ref[:, sl]
        zero = jnp.zeros_like(q)
        kc = kc_ref[0, :, sl].astype(BF16)
        vc = vc_ref[0, :, sl].astype(BF16)
        kn = kn_ref[:, sl].astype(BF16)
        vn = vn_ref[:, sl].astype(BF16)
        outs = []
        for h in range(2):
            hh = 2 * p + h
            qh = jnp.where(head0, q, zero) if h == 0 else jnp.where(head0, zero, q)
            cq = jnp.sum(jnp.where(hl == hh, cum_n_c, 0.0), axis=1, keepdims=True)
            s_c = _dot_nt(qh, kc) + cq - cum_c_r[hh:hh + 1, :]
            s_n = _dot_nt(qh, kn) + cq - cum_n_r[hh:hh + 1, :]
            s_n = jnp.where(causal, s_n, NEG_INF)
            m = jnp.maximum(jnp.max(s_c, axis=1, keepdims=True), jnp.max(s_n, axis=1, keepdims=True))
            p_c = jnp.exp2(s_c - m)
            p_n = jnp.exp2(s_n - m)
            l = jnp.sum(p_c, axis=1, keepdims=True) + jnp.sum(p_n, axis=1, keepdims=True)
            acc = _dot(p_c.astype(BF16), vc) + _dot(p_n.astype(BF16), vn)
            outs.append(acc / l)
        o_out[:, sl] = jnp.where(head0, outs[0], outs[1])


def _split3(x):
    hi = x.astype(BF16)
    r1 = x - hi.astype(F32)
    mid = r1.astype(BF16)
    lo = (r1 - mid.astype(F32)).astype(BF16)
    return hi, mid, lo


def _dot_tn3(xs, m):
    return _dot_tn(xs[0], m) + (_dot_tn(xs[1], m) + _dot_tn(xs[2], m))


def _dot_m3(m, xs):
    return _dot(m, xs[0]) + (_dot(m, xs[1]) + _dot(m, xs[2]))


def _fox_sample(qb, k, v, lf_new, cache_k, cache_v, cache_lf, row0, n_stream, t):
    d = qb.shape[1]
    n_pairs = d // PAIR
    past = cache_k.shape[1]
    nh = cache_lf.shape[2]
    blk0 = row0 // t
    triu = (jnp.arange(past)[:, None] <= jnp.arange(past)[None, :]).astype(BF16)
    row = pl.BlockSpec((t, d), lambda b: (blk0 + b, 0))
    return pl.pallas_call(
        functools.partial(_fox_sample_kernel, n_pairs),
        grid=(n_stream,),
        in_specs=[row, row, row,
                  pl.BlockSpec((1, t, nh), lambda b: (b, 0, 0)),
                  pl.BlockSpec((1, past, d), lambda b: (b, 0, 0)),
                  pl.BlockSpec((1, past, d), lambda b: (b, 0, 0)),
                  pl.BlockSpec((1, past, nh), lambda b: (b, 0, 0)),
                  _const_spec(triu.shape)],
        out_specs=pl.BlockSpec((t, d), lambda b: (b, 0)),
        out_shape=jax.ShapeDtypeStruct((n_stream * t, d), F32),
        compiler_params=_params(("parallel",), 48),
        name="fox_sample_attn",
    )(qb, k, v, lf_new, cache_k, cache_v, cache_lf, triu)


def kernel(x_prompt, x_sample, state_shift, state_wkv, cache_k, cache_v, cache_logf, ln1_g, ln1_b, ln2_g, ln2_b, a_mix, a_w_rkv, a_w0, a_w1, a_w2, a_a0, a_a1, a_a2, a_v0, a_v1, a_v2, a_g1, a_g2, a_k_k, a_k_a, a_r_k, a_lnx_g, a_lnx_b, a_w_o, kv_w, kv_bf, b_w_qg, b_w_o, router_w, router_b, moe_w_gu, moe_w_down):
    nb, seq, d = x_prompt.shape
    db, dt, _ = x_sample.shape
    assert nb == 1
    n_heads = d // HEAD_DIM
    depth = ln1_g.shape[0]
    n_a = a_mix.shape[0]
    past = cache_k.shape[1]
    t_p = nb * seq
    t_s = db * dt
    n = t_p + t_s
    tm = _row_tile(n)
    c_p = 64
    c_s = dt
    tq_attn = 512 if t_p % 512 == 0 else 256
    assert seq % c_p == 0 and t_p % c_s == 0 and (c_s & (c_s - 1)) == 0
    alpha_dn = (2 * depth) ** 0.25
    row2 = lambda a: a.reshape(1, -1)

    bd = (jnp.arange(256)[:, None] // HEAD_DIM == jnp.arange(256)[None, :] // HEAD_DIM).astype(BF16)
    rw_t = router_w.T
    rb_c = router_b.reshape(-1, 1)

    x = jnp.concatenate([x_prompt.reshape(t_p, d), x_sample.reshape(t_s, d)], axis=0)
    new_shift_p, new_shift_s, new_wkv_p, new_wkv_s = [], [], [], []
    v_first = None
    kv = None
    for l in range(depth):
        if l < n_a:
            xs = x[t_p:].reshape(db, dt, d)
            new_shift_p.append(x[t_p - 1:t_p].reshape(nb, d))
            new_shift_s.append(xs[:, -1])
            xs_prev = jnp.concatenate([state_shift[l][:, None, :], xs[:, :-1]], axis=1).reshape(t_s, d)
            w = dict(mix=jnp.concatenate([a_mix[l], jnp.zeros((2, d), F32)], axis=0),
                     wr=a_w_rkv[l, 0].astype(BF16), wk=a_w_rkv[l, 1].astype(BF16), wv=a_w_rkv[l, 2].astype(BF16),
                     w1=a_w1[l].astype(BF16), w2=a_w2[l].astype(BF16), w0=row2(a_w0[l]),
                     a1=a_a1[l].astype(BF16), a2=a_a2[l].astype(BF16), a0=row2(a_a0[l]),
                     g1=a_g1[l].astype(BF16), g2=a_g2[l].astype(BF16),
                     k_k=row2(a_k_k[l]), k_a=row2(a_k_a[l]), bd=bd)
            if l > 0:
                w.update(v1=a_v1[l - 1].astype(BF16), v2=a_v2[l - 1].astype(BF16), v0=row2(a_v0[l - 1]))
            r, lw, k, v, kn, b, g = _rwkv_pre(x, xs_prev, v_first if l > 0 else None, w, tm)
            if l == 0:
                v_first = v
            scan_in = (r, lw, k, v, kn, b)
            rp, yp, mm, nn = _wkv_chunks(scan_in, c_p, 0, t_p // c_p, n_heads // 2)
            s0 = jnp.zeros((1, n_heads // 2, PAIR, PAIR), F32)
            y_p, sf_p = _wkv_seq(rp, yp, mm, nn, s0, c_p, 1, t_p // c_p)
            rp, yp, mm, nn = _wkv_chunks(scan_in, c_s, t_p, db, n_heads // 2)
            y_s, sf_s = _wkv_seq(rp, yp, mm, nn, _state_to_blockdiag(state_wkv[l]), c_s, db, 1)
            new_wkv_p.append(_blockdiag_to_state(sf_p))
            new_wkv_s.append(_blockdiag_to_state(sf_s))
            y = jnp.concatenate([y_p, y_s], axis=0)
            consts = [row2(a_lnx_g[l]), row2(a_lnx_b[l]), row2(a_r_k[l]), bd, a_w_o[l].astype(BF16),
                      row2(ln1_g[l]), row2(ln1_b[l]), rw_t, rb_c]
            x1, cls = _mix_post(True, alpha_dn, [x, y, r, k, v, g], consts, tm)
        else:
            lb = l - n_a
            wq = b_w_qg[lb][:, :d].astype(BF16)
            qb, q_aug, gate = _fox_qg(x, kv['rem_r'], wq, _pad_heads(wq, PAIR), b_w_qg[lb][:, d:].astype(BF16), tm)
            j_lo = _oldest_needed_block(qb, kv['k'], kv['rem_r'], kv['base'], t_p, tq_attn, 4)
            o_p = _fox_prompt(kv['base'], j_lo, q_aug, kv['k_aug'], kv['vt_aug'], t_p, tq_attn, 4).T
            o_s = _fox_sample(qb, kv['k'], kv['v'], kv['lf_s'], cache_k.reshape(db, past, d),
                              cache_v.reshape(db, past, d), cache_logf, t_p, db, dt)
            o = jnp.concatenate([o_p, o_s], axis=0)
            consts = [b_w_o[lb].astype(BF16), row2(ln1_g[l]), row2(ln1_b[l]), rw_t, rb_c]
            x1, cls = _mix_post(False, alpha_dn, [x, o, gate], consts, tm)
        m = _moe(x1, cls[0], moe_w_gu[l].astype(BF16), moe_w_down[l].astype(BF16))
        if l == n_a - 1:
            wk, wv = kv_w[:, :d].astype(BF16), kv_w[:, d:2 * d].astype(BF16)
            kvw = dict(wk=wk, wv=wv, wk_aug=_pad_heads(wk, PAIR), wv_aug=_pad_heads(wv, V_ROWS),
                       wf_t=kv_w[:, 2 * d:].T, bf=kv_bf.reshape(-1, 1), tiles_per_blk=tq_attn // tm)
            x, k_all, v_all, k_aug, vt_aug, lf_r, cum_r, rem_r = _ln2(
                alpha_dn, x1, m, row2(ln2_g[l]), row2(ln2_b[l]), kvw, tm)
            base = (cum_r[:, :t_p:tq_attn] * LOG2E).T.reshape(-1)
            kv = dict(k=k_all, v=v_all, base=base, rem_r=rem_r, k_aug=k_aug,
                      vt_aug=vt_aug.reshape(n_heads, V_ROWS, n), lf_p=lf_r[:, :t_p].T,
                      lf_s=lf_r[:, t_p:].T.reshape(db, dt, n_heads))
        else:
            x = _ln2(alpha_dn, x1, m, row2(ln2_g[l]), row2(ln2_b[l]), None, tm)[0]

    y_prompt = x[:t_p].reshape(nb, seq, d)
    y_sample = x[t_p:].reshape(db, dt, d)
    p_k = kv['k'][:t_p].reshape(nb, seq, n_heads, HEAD_DIM)
    p_v = kv['v'][:t_p].reshape(nb, seq, n_heads, HEAD_DIM)
    s_k = kv['k'][t_p:].reshape(db, dt, n_heads, HEAD_DIM)
    s_v = kv['v'][t_p:].reshape(db, dt, n_heads, HEAD_DIM)
    return (y_prompt, y_sample, jnp.stack(new_shift_p), jnp.stack(new_wkv_p), p_k, p_v,
            kv['lf_p'].reshape(nb, seq, n_heads), jnp.stack(new_shift_s), jnp.stack(new_wkv_s),
            s_k, s_v, kv['lf_s'])
```

```python
import functools
import math

import jax
import jax.numpy as jnp
import numpy as np
from jax import lax
from jax.experimental import pallas as pl
from jax.experimental.pallas import tpu as pltpu

F32 = jnp.float32
BF16 = jnp.bfloat16

HEAD_DIM = 64
PAIR = 2 * HEAD_DIM
N_EXPERTS = 16
N_GROUPS = 4
EXPERTS_PER_GROUP = 4
N_CLASSES = 24
E_BLOCK = 128
GATE_LANES = 128
LN_EPS = 1e-5
GN_EPS = 64e-5
NEG_INF = -1e30
LOG2E = 1.4426950408889634
MIB = 2 ** 20

_PAIR_LO = (0, 0, 0, 1, 1, 2)
_PAIR_HI = (1, 2, 3, 2, 3, 3)


def _params(sem, vmem_mib):
    return pltpu.CompilerParams(dimension_semantics=sem, vmem_limit_bytes=vmem_mib * MIB)


def _dot(a, b):
    return jnp.dot(a, b, preferred_element_type=F32)


def _dot_nt(a, b):
    return lax.dot_general(a, b, (((1,), (1,)), ((), ())), preferred_element_type=F32)


def _dot_tn(a, b):
    return lax.dot_general(a, b, (((0,), (0,)), ((), ())), preferred_element_type=F32)


def _split(x):
    hi = x.astype(BF16)
    lo = (x - hi.astype(F32)).astype(BF16)
    return hi, lo


def _dot3(a, b, dot=_dot):
    return _dot3_multi(a, [b], dot)[0]


def _dot3_multi(a, bs, dot=_dot):
    ah, al = a
    ax = 0 if dot is _dot_nt else 1
    ns = [b[0].shape[ax] for b in bs]
    if any(n % 128 for n in ns):
        return [dot(ah, bh) + (dot(ah, bl) + dot(al, bh)) for bh, bl in bs]
    kax = 0 if dot is _dot_tn else 1
    if ah.shape[kax] == 128:
        lhs = jnp.concatenate([ah, al], axis=kax)
        if dot is _dot_nt:
            rhs = jnp.concatenate([jnp.concatenate([x, y], axis=1) for bh, bl in bs
                                   for x, y in ((bh, bh), (bl, jnp.zeros_like(bl)))], axis=0)
        else:
            top = jnp.concatenate([x for b in bs for x in b], axis=1)
            bot = jnp.concatenate([x for bh, bl in bs for x in (bh, jnp.zeros_like(bl))], axis=1)
            rhs = jnp.concatenate([top, bot], axis=0)
        r = dot(lhs, rhs)
        outs, o = [], 0
        for n in ns:
            outs.append(r[:, o:o + n] + r[:, o + n:o + 2 * n])
            o += 2 * n
        return outs
    r1 = dot(ah, jnp.concatenate([x for b in bs for x in b], axis=ax))
    r2 = dot(al, jnp.concatenate([b[0] for b in bs], axis=ax)) if len(bs) > 1 else dot(al, bs[0][0])
    outs, o1, o2 = [], 0, 0
    for n in ns:
        outs.append((r1[:, o1:o1 + n] + r1[:, o1 + n:o1 + 2 * n]) + r2[:, o2:o2 + n])
        o1 += 2 * n
        o2 += n
    return outs


def _dot_exact_rhs(x, m, dot=_dot):
    hi = x.astype(BF16)
    r1 = x - hi.astype(F32)
    mid = r1.astype(BF16)
    lo = (r1 - mid.astype(F32)).astype(BF16)
    return dot(hi, m) + (dot(mid, m) + dot(lo, m))


def _head_sum(x, bd_ref):
    bd = bd_ref[...]
    parts = []
    for j in range(x.shape[1] // 256):
        parts.append(_dot_exact_rhs(x[:, 256 * j:256 * (j + 1)], bd))
    return jnp.concatenate(parts, axis=1)


def _layer_norm(z, g, b):
    mu = jnp.mean(z, axis=-1, keepdims=True)
    zc = z - mu
    var = jnp.mean(zc * zc, axis=-1, keepdims=True)
    return zc * lax.rsqrt(var + LN_EPS) * g + b


def _row_tile(n):
    for t in (256, 128, 64, 32, 16, 8):
        if n % t == 0:
            return t
    raise ValueError(n)


def _const_spec(shape):
    nd = len(shape)
    return pl.BlockSpec(shape, lambda *_: (0,) * nd)


def _rwkv_pre_kernel(has_vres, n_prompt_tiles, *refs):
    if has_vres:
        (x_ref, x8_ref, xtail_ref, vf_ref, mix_ref, wr_ref, wk_ref, wv_ref, w1_ref, w2_ref, w0_ref,
         a1_ref, a2_ref, a0_ref, g1_ref, g2_ref, v1_ref, v2_ref, v0_ref, kk_ref, ka_ref, bd_ref,
         r_out, lw_out, k_out, v_out, kn_out, b_out, g_out) = refs
    else:
        (x_ref, x8_ref, xtail_ref, mix_ref, wr_ref, wk_ref, wv_ref, w1_ref, w2_ref, w0_ref,
         a1_ref, a2_ref, a0_ref, g1_ref, g2_ref, kk_ref, ka_ref, bd_ref,
         r_out, lw_out, k_out, v_out, kn_out, b_out, g_out) = refs
    x = x_ref[...]
    i = pl.program_id(0)
    first = lax.broadcasted_iota(jnp.int32, x.shape, 0) == 0
    above = jnp.where(i == 0, 0.0, x8_ref[7:8, :])
    shifted = jnp.where(first, above, pltpu.roll(x, 1, 0))
    xx = jnp.where(i >= n_prompt_tiles, xtail_ref[...], shifted) - x

    def mixed(i):
        return (x + xx * mix_ref[i:i + 1, :]).astype(BF16)

    xr, xw, xk, xv, xa, xg = (mixed(i) for i in range(6))
    r = _dot(xr, wr_ref[...])
    k = _dot(xk, wk_ref[...])
    v = _dot(xv, wv_ref[...])
    zw = w0_ref[...] + _dot(jnp.tanh(_dot(xw, w1_ref[...])).astype(BF16), w2_ref[...])
    lw = (-math.exp(-0.5)) * jax.nn.sigmoid(zw)
    alpha = jax.nn.sigmoid(a0_ref[...] + _dot(_dot(xa, a1_ref[...]).astype(BF16), a2_ref[...]))
    g = _dot(jax.nn.sigmoid(_dot(xg, g1_ref[...])).astype(BF16), g2_ref[...])
    if has_vres:
        gate_v = jax.nn.sigmoid(v0_ref[...] + _dot(_dot(xv, v1_ref[...]).astype(BF16), v2_ref[...]))
        v = v + (vf_ref[...] - v) * gate_v
    kkr = k * kk_ref[...]
    norm = jnp.sqrt(_head_sum(kkr * kkr, bd_ref))
    kn = kkr / jnp.maximum(norm, 1e-12)
    k2 = k * (1.0 + (alpha - 1.0) * ka_ref[...])
    r_out[...] = r
    lw_out[...] = lw
    k_out[...] = k2
    v_out[...] = v
    kn_out[...] = kn
    b_out[...] = kn * alpha
    g_out[...] = g


def _rwkv_pre(x, x_tail_prev, vfirst, w, tm):
    n, d = x.shape
    t_tail = x_tail_prev.shape[0]
    assert t_tail % tm == 0 and (n - t_tail) % tm == 0
    n_prompt_tiles = (n - t_tail) // tm
    has_vres = vfirst is not None
    row = pl.BlockSpec((tm, d), lambda i: (i, 0))
    above = pl.BlockSpec((8, d), lambda i: (jnp.maximum(i * (tm // 8) - 1, 0), 0))
    tail = pl.BlockSpec((tm, d), lambda i: (jnp.maximum(i - n_prompt_tiles, 0), 0))
    ins = [x, x, x_tail_prev] + ([vfirst] if has_vres else [])
    specs = [row, above, tail] + ([row] if has_vres else [])
    names = ['mix', 'wr', 'wk', 'wv', 'w1', 'w2', 'w0', 'a1', 'a2', 'a0', 'g1', 'g2']
    if has_vres:
        names += ['v1', 'v2', 'v0']
    names += ['k_k', 'k_a', 'bd']
    for nm in names:
        ins.append(w[nm])
        specs.append(_const_spec(w[nm].shape))
    out = jax.ShapeDtypeStruct((n, d), F32)
    return pl.pallas_call(
        functools.partial(_rwkv_pre_kernel, has_vres, n_prompt_tiles),
        grid=(n // tm,),
        in_specs=specs,
        out_specs=[row] * 7,
        out_shape=[out] * 7,
        compiler_params=_params(("parallel",), 56),
        name="rwkv_pre",
    )(*ins)


def _wkv_chunk_kernel(C, n_pairs, r_ref, lw_ref, k_ref, v_ref, kn_ref, b_ref,
                      rp_out, yp_out, m_out, n_out):
    C2 = 2 * C
    row = lax.broadcasted_iota(jnp.int32, (C2, C2), 0)
    col = lax.broadcasted_iota(jnp.int32, (C2, C2), 1)
    same = (row >= C) == (col >= C)
    strict = jnp.logical_and(same, col < row)
    incl = jnp.logical_and(same, col <= row)
    tri_incl = jnp.where(incl, 1.0, 0.0).astype(BF16)
    eye_c = jnp.where(row == col, 1.0, 0.0).astype(F32)
    r128 = lax.broadcasted_iota(jnp.int32, (PAIR, PAIR), 0)
    c128 = lax.broadcasted_iota(jnp.int32, (PAIR, PAIR), 1)
    eye_p = r128 == c128
    head0 = lax.broadcasted_iota(jnp.int32, (1, PAIR), 1) < HEAD_DIM

    def stack(t):
        return jnp.concatenate([jnp.where(head0, t, 0.0), jnp.where(head0, 0.0, t)], axis=0)

    n_sq = int(math.log2(C)) - 1
    sls = [slice(PAIR * p, PAIR * (p + 1)) for p in range(n_pairs)]

    def each(f, *lists):
        return [f(*a) for a in zip(*lists)]

    def load(ref):
        return [stack(ref[:, sl]) for sl in sls]

    lws = load(lw_ref)
    L = each(lambda x: _dot_exact_rhs_left(tri_incl, x), lws)
    lc = each(lambda l: l[C - 1:C, :] + l[C2 - 1:C2, :], L)
    kn, bs, ks = load(kn_ref), load(b_ref), load(k_ref)
    at = each(lambda n, l, w: _split(-n * jnp.exp(l - w)), kn, L, lws)
    rt_f = each(lambda r, l: r * jnp.exp(l), load(r_ref), L)
    rt = each(_split, rt_f)
    e_nl = each(lambda l: jnp.exp(-l), L)
    bb = each(lambda b, e: _split(b * e), bs, e_nl)
    kb = each(lambda k, e: _split(k * e), ks, e_nl)
    e_lc = each(lambda c, l: jnp.exp(c - l), lc, L)
    bh = each(lambda b, e: _split(b * e), bs, e_lc)
    kh = each(lambda k, e: _split(k * e), ks, e_lc)
    vs = each(_split, load(v_ref))

    g_a = each(lambda a, b, k: _dot3_multi(a, [b, k], _dot_nt), at, bb, kb)
    g_r = each(lambda a, b, k: _dot3_multi(a, [b, k], _dot_nt), rt, bb, kb)
    a_ab = [jnp.where(strict, g[0], 0.0) for g in g_a]
    a_ak = [_split(jnp.where(strict, g[1], 0.0)) for g in g_a]
    a_rb = [_split(jnp.where(incl, g[0], 0.0)) for g in g_r]
    a_rk = [_split(jnp.where(incl, g[1], 0.0)) for g in g_r]

    pws = each(lambda a: _split(_dot3(a, a)), each(_split, a_ab))
    tm = each(lambda a: eye_c + a, a_ab)
    for step in range(n_sq):
        if step < n_sq - 1:
            res = each(lambda s, t: _dot3_multi(s, [_split(t), s]), pws, tm)
            tm = [t + r[0] for t, r in zip(tm, res)]
            pws = [_split(r[1]) for r in res]
        else:
            tm = each(lambda t, s: t + _dot3(s, _split(t)), tm, pws)
    tms = each(_split, tm)
    w1 = each(lambda a, v: _split(_dot3(a, v)), a_ak, vs)
    tx = each(lambda t, a, w: _dot3_multi(t, [a, w]), tms, at, w1)
    at2 = [_split(r[0]) for r in tx]
    v2 = [_split(r[1]) for r in tx]
    ax = each(lambda a, x, v: _dot3_multi(a, [x, v]), a_rb, at2, v2)
    r2 = [r + x[0] for r, x in zip(rt_f, ax)]
    y2 = [x[1] + _dot3(c, v) for x, c, v in zip(ax, a_rk, vs)]
    bx = each(lambda b, x, v: _dot3_multi(b, [x, v], _dot_tn), bh, at2, v2)
    mm = [jnp.where(eye_p, jnp.exp(c), 0.0) + x[0] for c, x in zip(lc, bx)]
    nn = [x[1] + _dot3(k, v, _dot_tn) for x, k, v in zip(bx, kh, vs)]
    rp_out[...] = jnp.concatenate([x[:C, :] + x[C:, :] for x in r2], axis=1)
    yp_out[...] = jnp.concatenate([x[:C, :] + x[C:, :] for x in y2], axis=1)
    m_out[0] = jnp.stack(mm, axis=0)
    n_out[0] = jnp.stack(nn, axis=0)


def _dot_exact_rhs_left(m, x):
    hi = x.astype(BF16)
    r1 = x - hi.astype(F32)
    mid = r1.astype(BF16)
    lo = (r1 - mid.astype(F32)).astype(BF16)
    return _dot(m, hi) + (_dot(m, mid) + _dot(m, lo))


def _wkv_chunks(arrs, C, row0, n_chunks, pairs_per_step):
    d = arrs[0].shape[1]
    n_pairs = d // PAIR
    assert row0 % C == 0 and n_pairs % pairs_per_step == 0
    blk0 = row0 // C
    lanes = pairs_per_step * PAIR
    in_spec = pl.BlockSpec((C, lanes), lambda c, q: (blk0 + c, q))
    out_row = pl.BlockSpec((C, lanes), lambda c, q: (c, q))
    out_mat = pl.BlockSpec((1, pairs_per_step, PAIR, PAIR), lambda c, q: (c, q, 0, 0))
    t = n_chunks * C
    return pl.pallas_call(
        functools.partial(_wkv_chunk_kernel, C, pairs_per_step),
        grid=(n_chunks, n_pairs // pairs_per_step),
        in_specs=[in_spec] * 6,
        out_specs=[out_row, out_row, out_mat, out_mat],
        out_shape=[jax.ShapeDtypeStruct((t, d), F32), jax.ShapeDtypeStruct((t, d), F32),
                   jax.ShapeDtypeStruct((n_chunks, n_pairs, PAIR, PAIR), F32),
                   jax.ShapeDtypeStruct((n_chunks, n_pairs, PAIR, PAIR), F32)],
        compiler_params=_params(("parallel", "parallel"), 32),
        name=f"wkv_chunks_c{C}",
    )(*arrs)


def _wkv_seq_kernel(n_pairs, n_steps, rp_ref, yp_ref, m_ref, n_ref, s0_ref, y_out, s_out, s_scr):
    j = pl.program_id(1)

    @pl.when(j == 0)
    def _():
        s_scr[...] = s0_ref[0]

    ys, new_s = [], []
    for p in range(n_pairs):
        sl = slice(PAIR * p, PAIR * (p + 1))
        ss = _split(s_scr[p])
        ys.append(_dot3(_split(rp_ref[:, sl]), ss) + yp_ref[:, sl])
        new_s.append(_dot3(_split(m_ref[0, p]), ss) + n_ref[0, p])
    y_out[...] = jnp.concatenate(ys, axis=1)
    s_scr[...] = jnp.stack(new_s, axis=0)

    @pl.when(j == n_steps - 1)
    def _():
        s_out[0] = s_scr[...]


def _wkv_seq(rp, yp, m, nn, s0, C, n_seq, n_steps):
    t, d = rp.shape
    n_pairs = d // PAIR
    row = pl.BlockSpec((C, d), lambda s, j: (s * n_steps + j, 0))
    mat = pl.BlockSpec((1, n_pairs, PAIR, PAIR), lambda s, j: (s * n_steps + j, 0, 0, 0))
    st = pl.BlockSpec((1, n_pairs, PAIR, PAIR), lambda s, j: (s, 0, 0, 0))
    return pl.pallas_call(
        functools.partial(_wkv_seq_kernel, n_pairs, n_steps),
        grid=(n_seq, n_steps),
        in_specs=[row, row, mat, mat, st],
        out_specs=[row, st],
        out_shape=[jax.ShapeDtypeStruct((t, d), F32),
                   jax.ShapeDtypeStruct((n_seq, n_pairs, PAIR, PAIR), F32)],
        scratch_shapes=[pltpu.VMEM((n_pairs, PAIR, PAIR), F32)],
        compiler_params=_params(("arbitrary", "arbitrary"), 32),
        name=f"wkv_seq_c{C}",
    )(rp, yp, m, nn, s0)


def _state_to_blockdiag(s):
    b, h, n, _ = s.shape
    st = jnp.swapaxes(s, -1, -2).reshape(b, h // 2, 2, n, n)
    z = jnp.zeros_like(st[:, :, 0])
    top = jnp.concatenate([st[:, :, 0], z], axis=-1)
    bot = jnp.concatenate([z, st[:, :, 1]], axis=-1)
    return jnp.concatenate([top, bot], axis=-2)


def _blockdiag_to_state(bd):
    b, hp, _, _ = bd.shape
    n = HEAD_DIM
    s0 = bd[:, :, :n, :n]
    s1 = bd[:, :, n:, n:]
    st = jnp.stack([s0, s1], axis=2).reshape(b, hp * 2, n, n)
    return jnp.swapaxes(st, -1, -2)


def _route(x1, rw_ref, rb_ref):
    logits = _dot3(_split(rw_ref[...]), _split(x1), _dot_nt)
    mx = jnp.max(logits, axis=0, keepdims=True)
    ex = jnp.exp(logits - mx)
    scores = ex / jnp.sum(ex, axis=0, keepdims=True)
    sel = scores + rb_ref[...]
    rows = [sel[e:e + 1, :] for e in range(N_EXPERTS)]
    srow = [scores[e:e + 1, :] for e in range(N_EXPERTS)]

    def top2(vals):
        m1 = jnp.maximum(jnp.maximum(vals[0], vals[1]), jnp.maximum(vals[2], vals[3]))
        i1 = jnp.where(vals[0] == m1, 0, jnp.where(vals[1] == m1, 1, jnp.where(vals[2] == m1, 2, 3)))
        rest = [jnp.where(i1 == j, -jnp.inf, vals[j]) for j in range(4)]
        m2 = jnp.maximum(jnp.maximum(rest[0], rest[1]), jnp.maximum(rest[2], rest[3]))
        i2 = jnp.where(rest[0] == m2, 0, jnp.where(rest[1] == m2, 1, jnp.where(rest[2] == m2, 2, 3)))
        return m1, i1, m2, i2

    gscore = []
    for gidx in range(N_GROUPS):
        m1, _, m2, _ = top2(rows[4 * gidx:4 * gidx + 4])
        gscore.append(m1 + m2)
    gm = jnp.maximum(jnp.maximum(gscore[0], gscore[1]), jnp.maximum(gscore[2], gscore[3]))
    gi = jnp.where(gscore[0] == gm, 0, jnp.where(gscore[1] == gm, 1, jnp.where(gscore[2] == gm, 2, 3)))

    def pick(rws, j):
        return jnp.where(gi == 0, rws[j], jnp.where(gi == 1, rws[4 + j],
                                                    jnp.where(gi == 2, rws[8 + j], rws[12 + j])))

    in_grp = [pick(rows, j) for j in range(4)]
    sc_grp = [pick(srow, j) for j in range(4)]
    _, i1, _, i2 = top2(in_grp)

    def at(vals, idx):
        return jnp.where(idx == 0, vals[0], jnp.where(idx == 1, vals[1],
                                                      jnp.where(idx == 2, vals[2], vals[3])))

    ga = at(sc_grp, i1)
    gb = at(sc_grp, i2)
    tot = ga + gb
    ga = ga / tot
    gb = gb / tot
    lo = jnp.minimum(i1, i2)
    hi = jnp.maximum(i1, i2)
    g_lo = jnp.where(i1 < i2, ga, gb)
    g_hi = jnp.where(i1 < i2, gb, ga)
    pair = jnp.where(lo == 0, hi - 1, jnp.where(lo == 1, hi + 1, 5))
    cls = gi * 6 + pair
    return cls.astype(jnp.int32), g_lo, g_hi


def _mix_post_kernel(is_rwkv, alpha_dn, *refs):
    if is_rwkv:
        (x_ref, y_ref, r_ref, k_ref, v_ref, g_ref, lg_ref, lb_ref, rk_ref, bd_ref,
         wo_ref, n1g_ref, n1b_ref, rw_ref, rb_ref, x1_out, cls_out) = refs
        y = y_ref[...]
        mu = _head_sum(y, bd_ref) * (1.0 / HEAD_DIM)
        yc = y - mu
        var = _head_sum(yc * yc, bd_ref) * (1.0 / HEAD_DIM)
        yn = yc * lax.rsqrt(var + GN_EPS) * lg_ref[...] + lb_ref[...]
        v = v_ref[...]
        bonus = _head_sum(r_ref[...] * k_ref[...] * rk_ref[...], bd_ref)
        z = (yn + bonus * v) * g_ref[...]
    else:
        (x_ref, o_ref, gt_ref, wo_ref, n1g_ref, n1b_ref, rw_ref, rb_ref,
         x1_out, cls_out) = refs
        z = o_ref[...] * gt_ref[...]
    h = _dot(z.astype(BF16), wo_ref[...])
    x1 = _layer_norm(alpha_dn * x_ref[...] + h, n1g_ref[...], n1b_ref[...])
    cls, g_lo, g_hi = _route(x1, rw_ref, rb_ref)
    tm, d = x1.shape
    cls_out[...] = jnp.broadcast_to(cls, (8, tm))
    pieces = [x.astype(F32) for x in _split3(g_lo) + _split3(g_hi)]
    gs = jnp.concatenate(pieces + [jnp.zeros((16 - len(pieces), tm), F32)], axis=0).astype(BF16)
    rr = lax.broadcasted_iota(jnp.int32, (16, GATE_LANES), 0)
    cc = lax.broadcasted_iota(jnp.int32, (16, GATE_LANES), 1)
    place = jnp.where(jnp.logical_and(rr < 6, cc == jnp.where(rr < 3, 0, 1)), 1.0, 0.0).astype(BF16)
    x1_out[:, :d] = x1
    x1_out[:, d:] = _dot_tn(gs, place)


def _mix_post(is_rwkv, alpha_dn, acts, consts, tm):
    n, d = acts[0].shape
    row = pl.BlockSpec((tm, d), lambda i: (i, 0))
    specs = [row] * len(acts) + [_const_spec(c.shape) for c in consts]
    lane = pl.BlockSpec((8, tm), lambda i: (0, i))
    wide = pl.BlockSpec((tm, d + GATE_LANES), lambda i: (i, 0))
    return pl.pallas_call(
        functools.partial(_mix_post_kernel, is_rwkv, alpha_dn),
        grid=(n // tm,),
        in_specs=specs,
        out_specs=[wide, lane],
        out_shape=[jax.ShapeDtypeStruct((n, d + GATE_LANES), F32), jax.ShapeDtypeStruct((8, n), jnp.int32)],
        compiler_params=_params(("parallel",), 48),
        name="mix_post_rwkv" if is_rwkv else "mix_post_fox",
    )(*acts, *consts)


def _moe_kernel(n_tok, d, d_exp, n_blk, e1_ref, e2_ref, off_ref, nv_ref, order_ref, x_hbm,
                wgu1_ref, wd1_ref, wgu2_ref, wd2_ref, out_hbm, xbuf, obuf, gsem, ssem):
    b = pl.program_id(0)
    slot = lax.rem(b, 2)
    nv_b = nv_ref[b]
    unroll = 8

    def gather_start(blk, sl):
        base = off_ref[blk]

        def body(c, carry):
            for u in range(unroll):
                i = c * unroll + u
                tok = order_ref[jnp.minimum(base + i, n_tok - 1)]
                pltpu.make_async_copy(x_hbm.at[pl.ds(tok, 1)], xbuf.at[sl, pl.ds(i, 1)], gsem.at[sl]).start()
            return carry

        lax.fori_loop(0, E_BLOCK // unroll, body, 0)

    def gather_wait(sl):
        pltpu.make_async_copy(x_hbm.at[pl.ds(0, E_BLOCK)], xbuf.at[sl], gsem.at[sl]).wait()

    def scatter_start(blk, sl):
        base = off_ref[blk]

        def body(i, carry):
            tok = order_ref[base + i]
            pltpu.make_async_copy(obuf.at[sl, pl.ds(i, 1)], out_hbm.at[pl.ds(tok, 1)], ssem.at[sl]).start()
            return carry

        lax.fori_loop(0, nv_ref[blk], body, 0)

    def scatter_wait(blk, sl):
        nv = nv_ref[blk]

        @pl.when(nv == E_BLOCK)
        def _():
            pltpu.make_async_copy(obuf.at[sl], out_hbm.at[pl.ds(0, E_BLOCK)], ssem.at[sl]).wait()

        @pl.when(nv < E_BLOCK)
        def _():
            def body(i, carry):
                pltpu.make_async_copy(obuf.at[sl, pl.ds(0, 1)], out_hbm.at[pl.ds(0, 1)], ssem.at[sl]).wait()
                return carry

            lax.fori_loop(0, nv, body, 0)

    @pl.when(jnp.logical_and(b == 0, nv_b > 0))
    def _():
        gather_start(0, 0)

    nxt = jnp.minimum(b + 1, n_blk - 1)

    @pl.when(jnp.logical_and(b + 1 < n_blk, nv_ref[nxt] > 0))
    def _():
        gather_start(nxt, 1 - slot)

    @pl.when(nv_b > 0)
    def _():
        gather_wait(slot)
        rows = xbuf[slot]
        xb = rows[:, :d].astype(BF16)

        def expert(wgu_ref, wd_ref):
            gu = _dot(xb, wgu_ref[0])
            hmid = jax.nn.silu(gu[:, :d_exp]) * gu[:, d_exp:]
            return _dot(hmid.astype(BF16), wd_ref[0])

        y1 = expert(wgu1_ref, wd1_ref)
        y2 = expert(wgu2_ref, wd2_ref)
        obuf[slot] = y1 * rows[:, d:d + 1] + y2 * rows[:, d + 1:d + 2]

    prev = jnp.maximum(b - 1, 0)

    @pl.when(jnp.logical_and(b > 0, nv_ref[prev] > 0))
    def _():
        scatter_wait(prev, 1 - slot)

    @pl.when(nv_b > 0)
    def _():
        scatter_start(b, slot)

    @pl.when(jnp.logical_and(b == n_blk - 1, nv_b > 0))
    def _():
        scatter_wait(b, slot)


def _moe(xg, cls, w_gu, w_down):
    n = xg.shape[0]
    d, d_exp = w_down.shape[2], w_down.shape[1]
    n_blk = (n + N_CLASSES * (E_BLOCK - 1) + E_BLOCK - 1) // E_BLOCK
    counts = jnp.sum((cls[:, None] == jnp.arange(N_CLASSES, dtype=jnp.int32)[None, :]).astype(jnp.int32), axis=0)
    padded = (counts + E_BLOCK - 1) // E_BLOCK * E_BLOCK
    pad_end = jnp.cumsum(padded)
    pad_start = pad_end - padded
    seg_start = jnp.cumsum(counts) - counts
    order = jnp.argsort(cls).astype(jnp.int32)
    blk_start = jnp.arange(n_blk, dtype=jnp.int32) * E_BLOCK
    blk_cls = jnp.minimum(jnp.sum((pad_end[None, :] <= blk_start[:, None]).astype(jnp.int32), axis=1),
                          N_CLASSES - 1)
    classes = jnp.arange(N_CLASSES, dtype=jnp.int32)

    def lookup(table, idx):
        return jnp.sum(jnp.where(idx[:, None] == classes[None, :], table[None, :], 0), axis=1)

    rank0 = blk_start - lookup(pad_start, blk_cls)
    blk_nv = jnp.clip(lookup(counts, blk_cls) - rank0, 0, E_BLOCK).astype(jnp.int32)
    blk_off = (lookup(seg_start, blk_cls) + rank0).astype(jnp.int32)
    last_cls = jnp.max(jnp.where(blk_nv > 0, blk_cls, 0))
    blk_cls = jnp.where(blk_nv > 0, blk_cls, last_cls)
    e_lo = jnp.asarray([(c // 6) * EXPERTS_PER_GROUP + _PAIR_LO[c % 6] for c in range(N_CLASSES)], jnp.int32)
    e_hi = jnp.asarray([(c // 6) * EXPERTS_PER_GROUP + _PAIR_HI[c % 6] for c in range(N_CLASSES)], jnp.int32)
    blk_e1 = lookup(e_lo, blk_cls)
    blk_e2 = lookup(e_hi, blk_cls)

    gu1 = pl.BlockSpec((1, d, 2 * d_exp), lambda b, e1, e2, of, nv, od: (e1[b], 0, 0))
    dn1 = pl.BlockSpec((1, d_exp, d), lambda b, e1, e2, of, nv, od: (e1[b], 0, 0))
    gu2 = pl.BlockSpec((1, d, 2 * d_exp), lambda b, e1, e2, of, nv, od: (e2[b], 0, 0))
    dn2 = pl.BlockSpec((1, d_exp, d), lambda b, e1, e2, of, nv, od: (e2[b], 0, 0))
    grid_spec = pltpu.PrefetchScalarGridSpec(
        num_scalar_prefetch=5,
        grid=(n_blk,),
        in_specs=[pl.BlockSpec(memory_space=pl.ANY), gu1, dn1, gu2, dn2],
        out_specs=pl.BlockSpec(memory_space=pl.ANY),
        scratch_shapes=[pltpu.VMEM((2, E_BLOCK, d + GATE_LANES), F32), pltpu.VMEM((2, E_BLOCK, d), F32),
                        pltpu.SemaphoreType.DMA((2,)), pltpu.SemaphoreType.DMA((2,))],
    )
    return pl.pallas_call(
        functools.partial(_moe_kernel, n, d, d_exp, n_blk),
        grid_spec=grid_spec,
        out_shape=jax.ShapeDtypeStruct((n, d), F32),
        compiler_params=_params(("arbitrary",), 40),
        name="moe_experts",
    )(blk_e1, blk_e2, blk_off, blk_nv, order, xg, w_gu, w_down, w_gu, w_down)


def _bias_placement(nh, lead_ones):
    p = np.zeros((4 * nh, nh * PAIR), np.float32)
    for h in range(nh):
        lane0 = PAIR * h + HEAD_DIM
        if lead_ones:
            p[h, lane0:lane0 + 3] = 1.0
            for piece in range(3):
                p[(piece + 1) * nh + h, lane0 + 3 + piece] = 1.0
        else:
            for piece in range(3):
                p[piece * nh + h, lane0 + piece] = 1.0
            p[3 * nh + h, lane0 + 3:lane0 + 6] = 1.0
    return jnp.asarray(p, BF16)


def _pad_heads(w, width):
    d, hd = w.shape
    nh = hd // HEAD_DIM
    w3 = w.reshape(d, nh, HEAD_DIM)
    return jnp.concatenate([w3, jnp.zeros((d, nh, width - HEAD_DIM), w.dtype)], axis=-1).reshape(d, nh * width)


def _bias_rows(rem, lead_ones):
    parts = [x.astype(F32) for x in _split3(rem)]
    ones = [jnp.ones_like(rem)]
    return jnp.concatenate(ones + parts if lead_ones else parts + ones, axis=0).astype(BF16)


def _ln2_kernel(alpha_dn, with_kv, tiles_per_blk, *refs):
    if with_kv:
        (x1_ref, m_ref, g_ref, b_ref, wk_ref, wv_ref, wka_ref, wva_ref, wf_ref, bf_ref, tri_ref,
         pk_ref, eye_ref, ones_ref,
         x2_out, k_out, v_out, ka_out, vt_out, lf_out, cum_out, rem_out, carry, base_scr) = refs
    else:
        x1_ref, m_ref, g_ref, b_ref, x2_out = refs
    x2 = _layer_norm(alpha_dn * x1_ref[...] + m_ref[...], g_ref[...], b_ref[...])
    x2_out[...] = x2
    if with_kv:
        i = pl.program_id(0)
        xb = x2.astype(BF16)
        k_out[...] = _dot(xb, wk_ref[...])
        v_out[...] = _dot(xb, wv_ref[...])
        z = _dot3(_split(wf_ref[...]), _split(x2), _dot_nt) + bf_ref[...]
        lf = jnp.minimum(z, 0.0) - jnp.log1p(jnp.exp(-jnp.abs(z)))
        lf_out[...] = lf

        @pl.when(i == 0)
        def _():
            carry[...] = jnp.zeros_like(carry)

        cum = _dot_exact_rhs(lf, tri_ref[...]) + carry[...]
        cum_out[...] = cum
        carry[...] = cum[:, -1:]

        cum2 = cum * LOG2E

        @pl.when(lax.rem(i, tiles_per_blk) == 0)
        def _():
            base_scr[...] = cum2[:, :1]

        rem = cum2 - base_scr[...]
        rem_out[...] = rem
        ka_out[...] = (_dot(xb, wka_ref[...]) + _dot_tn(_bias_rows(-rem, False), pk_ref[...])).astype(BF16)
        vw = _dot(xb, wva_ref[...]).astype(BF16)
        vt_out[...] = (_dot_tn(vw, eye_ref[...]) + ones_ref[...]).astype(BF16)


def _ln2(alpha_dn, x1, m, g, b, kv, tm):
    n, d = m.shape
    row = pl.BlockSpec((tm, d), lambda i: (i, 0))
    with_kv = kv is not None
    ins = [x1, m, g, b]
    specs = [row, row, _const_spec(g.shape), _const_spec(b.shape)]
    outs = [jax.ShapeDtypeStruct((n, d), F32)]
    ospecs = [row]
    scratch = []
    if with_kv:
        nh = kv['wf_t'].shape[0]
        tri = (jnp.arange(tm)[:, None] <= jnp.arange(tm)[None, :]).astype(BF16)
        eye = jnp.eye(tm, dtype=BF16)
        ones_col = (jnp.arange(nh * V_ROWS) % V_ROWS == HEAD_DIM).astype(F32).reshape(-1, 1)
        extra = [kv['wk'], kv['wv'], kv['wk_aug'], kv['wv_aug'], kv['wf_t'], kv['bf'], tri,
                 _bias_placement(nh, False), eye, ones_col]
        ins += extra
        specs += [_const_spec(e.shape) for e in extra]
        lane = pl.BlockSpec((nh, tm), lambda i: (0, i))
        outs += [jax.ShapeDtypeStruct((n, d), F32), jax.ShapeDtypeStruct((n, d), F32),
                 jax.ShapeDtypeStruct((n, nh * PAIR), BF16), jax.ShapeDtypeStruct((nh * V_ROWS, n), BF16),
                 jax.ShapeDtypeStruct((nh, n), F32), jax.ShapeDtypeStruct((nh, n), F32),
                 jax.ShapeDtypeStruct((nh, n), F32)]
        ospecs += [row, row, pl.BlockSpec((tm, nh * PAIR), lambda i: (i, 0)),
                   pl.BlockSpec((nh * V_ROWS, tm), lambda i: (0, i)), lane, lane, lane]
        scratch = [pltpu.VMEM((nh, 1), F32), pltpu.VMEM((nh, 1), F32)]
    res = pl.pallas_call(
        functools.partial(_ln2_kernel, alpha_dn, with_kv, kv['tiles_per_blk'] if with_kv else 1),
        grid=(n // tm,),
        in_specs=specs,
        out_specs=ospecs,
        out_shape=outs,
        scratch_shapes=scratch,
        compiler_params=_params(("arbitrary",), 48),
        name="ln2_kv" if with_kv else "ln2",
    )(*ins)
    return res


def _fox_qg_kernel(x_ref, rem_ref, wq_ref, wqa_ref, wg_ref, pq_ref, q_out, qa_out, gate_out):
    xb = x_ref[...].astype(BF16)
    scale = LOG2E * HEAD_DIM ** -0.5
    q_out[...] = (_dot(xb, wq_ref[...]) * scale).astype(BF16)
    qa_out[...] = (_dot(xb, wqa_ref[...]) * scale
                   + _dot_tn(_bias_rows(rem_ref[...], True), pq_ref[...])).astype(BF16)
    gate_out[...] = jax.nn.sigmoid(_dot(xb, wg_ref[...]))


def _fox_qg(x, rem_r, wq, wq_aug, wg, tm):
    n, d = x.shape
    nh = rem_r.shape[0]
    row = pl.BlockSpec((tm, d), lambda i: (i, 0))
    pq = _bias_placement(nh, True)
    return pl.pallas_call(
        _fox_qg_kernel,
        grid=(n // tm,),
        in_specs=[row, pl.BlockSpec((nh, tm), lambda i: (0, i)), _const_spec(wq.shape),
                  _const_spec(wq_aug.shape), _const_spec(wg.shape), _const_spec(pq.shape)],
        out_specs=[row, pl.BlockSpec((tm, nh * PAIR), lambda i: (i, 0)), row],
        out_shape=[jax.ShapeDtypeStruct((n, d), BF16), jax.ShapeDtypeStruct((n, nh * PAIR), BF16),
                   jax.ShapeDtypeStruct((n, d), F32)],
        compiler_params=_params(("parallel",), 48),
        name="fox_qg",
    )(x, rem_r, wq, wq_aug, wg, pq)


V_ROWS = 80


def _fox_prompt_kernel(tq, hps, nh, base_ref, jlo_ref, q_ref, k_ref, vt_ref, o_out, s_scr, p_scr):
    g = pl.program_id(0)
    i = pl.program_id(1)
    j_lo = jlo_ref[g * pl.num_programs(1) + i]
    rr = lax.broadcasted_iota(jnp.int32, (tq, tq), 0)
    cc = lax.broadcasted_iota(jnp.int32, (tq, tq), 1)
    causal = rr <= cc

    def scores(j, h):
        k0 = pl.multiple_of(j * tq, tq)
        lanes = slice(PAIR * h, PAIR * (h + 1))
        return _dot_nt(k_ref[pl.ds(k0, tq), lanes], q_ref[:, lanes])

    def base_gap(j, h):
        head = g * hps + h
        return base_ref[i * nh + head] - base_ref[j * nh + head]

    def softmax(s, m, gap):
        m_new = jnp.maximum(m, jnp.max(s, axis=0, keepdims=True) + gap)
        return jnp.exp2(s - (m_new - gap)).astype(BF16), jnp.exp2(m - m_new), m_new

    def accum(acc, corr, p, j, h):
        k0 = pl.multiple_of(j * tq, tq)
        return acc * corr + _dot(vt_ref[h, :, pl.ds(k0, tq)], p)

    init = []
    for h in range(hps):
        s = jnp.where(causal, scores(i, h), NEG_INF)
        m0 = jnp.max(s, axis=0, keepdims=True)
        p_scr[h] = jnp.exp2(s - m0).astype(BF16)
        s_scr[h] = scores(jnp.maximum(i - 1, 0), h)
        init.append((m0, jnp.zeros((V_ROWS, tq), F32), jnp.ones((1, tq), F32)))

    def body(t, carry):
        j = i - 1 - t
        new = []
        for h in range(hps):
            m, acc, corr_prev = carry[h]
            acc = accum(acc, corr_prev, p_scr[h], j + 1, h)
            p, corr, m = softmax(s_scr[h], m, base_gap(j, h))
            p_scr[h] = p
            s_scr[h] = scores(jnp.maximum(j - 1, 0), h)
            new.append((m, acc, corr))
        return tuple(new)

    carry = lax.fori_loop(0, i - j_lo, body, tuple(init))
    outs = []
    for h in range(hps):
        m, acc, corr_prev = carry[h]
        acc = accum(acc, corr_prev, p_scr[h], j_lo, h)
        outs.append(acc[:HEAD_DIM, :] / acc[HEAD_DIM:HEAD_DIM + 1, :])
    o_out[...] = jnp.concatenate(outs, axis=0)


UNDERFLOW_LOG2 = -136.0


def _oldest_needed_block(qb, k, rem_r, base, t, tq, hps):
    d = qb.shape[1]
    nh = d // HEAD_DIM
    nb = t // tq
    slack = 2.0
    q = qb[:t].astype(F32).reshape(nb, tq, nh, HEAD_DIM)
    kk = k[:t].astype(BF16).astype(F32).reshape(nb, tq, nh, HEAD_DIM)
    rem = rem_r[:, :t].T.reshape(nb, tq, nh)
    q_norm = jnp.sqrt(jnp.max(jnp.sum(q * q, axis=-1), axis=1))
    k_norm = jnp.sqrt(jnp.max(jnp.sum(kk * kk, axis=-1), axis=(0, 1)))
    self_min = jnp.min(jnp.sum(q * kk, axis=-1), axis=1)
    s_max = q_norm * k_norm + jnp.max(rem, axis=1) + jnp.max(-rem, axis=(0, 1)) + slack
    m_low = self_min - slack
    b = base.reshape(nb, nh)
    gap = b[:, None, :] - b[None, :, :]
    older = jnp.arange(nb)[None, :, None] < jnp.arange(nb)[:, None, None]
    alive = jnp.logical_and(older, gap + (s_max - m_low)[:, None, :] >= UNDERFLOW_LOG2)
    idx = jnp.arange(nb, dtype=jnp.int32)
    j_lo = jnp.min(jnp.where(alive, idx[None, :, None], idx[:, None, None]), axis=1)
    return jnp.min(j_lo.reshape(nb, nh // hps, hps), axis=-1).T.reshape(-1).astype(jnp.int32)


def _fox_prompt(base, j_lo, q_aug, k_aug, vt_aug, t, tq, hps):
    nh = vt_aug.shape[0]
    return pl.pallas_call(
        functools.partial(_fox_prompt_kernel, tq, hps, nh),
        grid=(nh // hps, t // tq),
        in_specs=[pl.BlockSpec(memory_space=pltpu.SMEM), pl.BlockSpec(memory_space=pltpu.SMEM),
                  pl.BlockSpec((tq, hps * PAIR), lambda g, i: (i, g)),
                  pl.BlockSpec((t, hps * PAIR), lambda g, i: (0, g), pipeline_mode=pl.Buffered(1)),
                  pl.BlockSpec((hps, V_ROWS, t), lambda g, i: (g, 0, 0), pipeline_mode=pl.Buffered(1))],
        out_specs=pl.BlockSpec((hps * HEAD_DIM, tq), lambda g, i: (g, i)),
        out_shape=jax.ShapeDtypeStruct((nh * HEAD_DIM, t), F32),
        scratch_shapes=[pltpu.VMEM((hps, tq, tq), F32), pltpu.VMEM((hps, tq, tq), BF16)],
        compiler_params=_params(("parallel", "parallel"), 48),
        name="fox_prompt_attn",
    )(base, j_lo, q_aug, k_aug, vt_aug)


def _fox_sample_kernel(n_pairs, q_ref, kn_ref, vn_ref, lfn_ref, kc_ref, vc_ref, lfc_ref,
                       triu_ref, o_out):
    t = q_ref.shape[0]
    past = kc_ref.shape[1]
    lfc = lfc_ref[0]
    lfn = lfn_ref[0]
    nh = lfc.shape[1]
    lfc_s = _split3(lfc)
    cum_c_r = _dot_tn3(lfc_s, triu_ref[...])
    ones_row = jnp.ones((1, past), BF16)
    tot_r = _dot_m3(ones_row, lfc_s)
    tot_c = cum_c_r[:, past - 1:past]
    rr = lax.broadcasted_iota(jnp.int32, (t, t), 0)
    cc = lax.broadcasted_iota(jnp.int32, (t, t), 1)
    causal = rr >= cc
    tril = jnp.where(causal, 1.0, 0.0).astype(BF16)
    lfn_s = _split3(lfn)
    cum_n_c = (_dot_m3(tril, lfn_s) + tot_r) * LOG2E
    cum_n_r = (_dot_tn3(lfn_s, jnp.where(rr <= cc, 1.0, 0.0).astype(BF16)) + tot_c) * LOG2E
    cum_c_r = cum_c_r * LOG2E
    lane = lax.broadcasted_iota(jnp.int32, (1, PAIR), 1)
    head0 = lane < HEAD_DIM
    hl = lax.broadcasted_iota(jnp.int32, (1, nh), 1)
    for p in range(n_pairs):
        sl = slice(PAIR * p, PAIR * (p + 1))
        q = q_ref[:, sl]
        zero = jnp.zeros_like(q)
        kc = kc_ref[0, :, sl].astype(BF16)
        vc = vc_ref[0, :, sl].astype(BF16)
        kn = kn_ref[:, sl].astype(BF16)
        vn = vn_ref[:, sl].astype(BF16)
        outs = []
        for h in range(2):
            hh = 2 * p + h
            qh = jnp.where(head0, q, zero) if h == 0 else jnp.where(head0, zero, q)
            cq = jnp.sum(jnp.where(hl == hh, cum_n_c, 0.0), axis=1, keepdims=True)
            s_c = _dot_nt(qh, kc) + cq - cum_c_r[hh:hh + 1, :]
            s_n = _dot_nt(qh, kn) + cq - cum_n_r[hh:hh + 1, :]
            s_n = jnp.where(causal, s_n, NEG_INF)
            m = jnp.maximum(jnp.max(s_c, axis=1, keepdims=True), jnp.max(s_n, axis=1, keepdims=True))
            p_c = jnp.exp2(s_c - m)
            p_n = jnp.exp2(s_n - m)
            l = jnp.sum(p_c, axis=1, keepdims=True) + jnp.sum(p_n, axis=1, keepdims=True)
            acc = _dot(p_c.astype(BF16), vc) + _dot(p_n.astype(BF16), vn)
            outs.append(acc / l)
        o_out[:, sl] = jnp.where(head0, outs[0], outs[1])


def _split3(x):
    hi = x.astype(BF16)
    r1 = x - hi.astype(F32)
    mid = r1.astype(BF16)
    lo = (r1 - mid.astype(F32)).astype(BF16)
    return hi, mid, lo


def _dot_tn3(xs, m):
    return _dot_tn(xs[0], m) + (_dot_tn(xs[1], m) + _dot_tn(xs[2], m))


def _dot_m3(m, xs):
    return _dot(m, xs[0]) + (_dot(m, xs[1]) + _dot(m, xs[2]))


def _fox_sample(qb, k, v, lf_new, cache_k, cache_v, cache_lf, row0, n_stream, t):
    d = qb.shape[1]
    n_pairs = d // PAIR
    past = cache_k.shape[1]
    nh = cache_lf.shape[2]
    blk0 = row0 // t
    triu = (jnp.arange(past)[:, None] <= jnp.arange(past)[None, :]).astype(BF16)
    row = pl.BlockSpec((t, d), lambda b: (blk0 + b, 0))
    return pl.pallas_call(
        functools.partial(_fox_sample_kernel, n_pairs),
        grid=(n_stream,),
        in_specs=[row, row, row,
                  pl.BlockSpec((1, t, nh), lambda b: (b, 0, 0)),
                  pl.BlockSpec((1, past, d), lambda b: (b, 0, 0)),
                  pl.BlockSpec((1, past, d), lambda b: (b, 0, 0)),
                  pl.BlockSpec((1, past, nh), lambda b: (b, 0, 0)),
                  _const_spec(triu.shape)],
        out_specs=pl.BlockSpec((t, d), lambda b: (b, 0)),
        out_shape=jax.ShapeDtypeStruct((n_stream * t, d), F32),
        compiler_params=_params(("parallel",), 48),
        name="fox_sample_attn",
    )(qb, k, v, lf_new, cache_k, cache_v, cache_lf, triu)


def kernel(x_prompt, x_sample, state_shift, state_wkv, cache_k, cache_v, cache_logf, ln1_g, ln1_b, ln2_g, ln2_b, a_mix, a_w_rkv, a_w0, a_w1, a_w2, a_a0, a_a1, a_a2, a_v0, a_v1, a_v2, a_g1, a_g2, a_k_k, a_k_a, a_r_k, a_lnx_g, a_lnx_b, a_w_o, kv_w, kv_bf, b_w_qg, b_w_o, router_w, router_b, moe_w_gu, moe_w_down):
    nb, seq, d = x_prompt.shape
    db, dt, _ = x_sample.shape
    assert nb == 1
    n_heads = d // HEAD_DIM
    depth = ln1_g.shape[0]
    n_a = a_mix.shape[0]
    past = cache_k.shape[1]
    t_p = nb * seq
    t_s = db * dt
    n = t_p + t_s
    tm = _row_tile(n)
    c_p = 64
    c_s = dt
    tq_attn = 512 if t_p % 512 == 0 else 256
    assert seq % c_p == 0 and t_p % c_s == 0 and (c_s & (c_s - 1)) == 0
    alpha_dn = (2 * depth) ** 0.25
    row2 = lambda a: a.reshape(1, -1)

    bd = (jnp.arange(256)[:, None] // HEAD_DIM == jnp.arange(256)[None, :] // HEAD_DIM).astype(BF16)
    rw_t = router_w.T
    rb_c = router_b.reshape(-1, 1)

    x = jnp.concatenate([x_prompt.reshape(t_p, d), x_sample.reshape(t_s, d)], axis=0)
    new_shift_p, new_shift_s, new_wkv_p, new_wkv_s = [], [], [], []
    v_first = None
    kv = None
    for l in range(depth):
        if l < n_a:
            xs = x[t_p:].reshape(db, dt, d)
            new_shift_p.append(x[t_p - 1:t_p].reshape(nb, d))
            new_shift_s.append(xs[:, -1])
            xs_prev = jnp.concatenate([state_shift[l][:, None, :], xs[:, :-1]], axis=1).reshape(t_s, d)
            w = dict(mix=jnp.concatenate([a_mix[l], jnp.zeros((2, d), F32)], axis=0),
                     wr=a_w_rkv[l, 0].astype(BF16), wk=a_w_rkv[l, 1].astype(BF16), wv=a_w_rkv[l, 2].astype(BF16),
                     w1=a_w1[l].astype(BF16), w2=a_w2[l].astype(BF16), w0=row2(a_w0[l]),
                     a1=a_a1[l].astype(BF16), a2=a_a2[l].astype(BF16), a0=row2(a_a0[l]),
                     g1=a_g1[l].astype(BF16), g2=a_g2[l].astype(BF16),
                     k_k=row2(a_k_k[l]), k_a=row2(a_k_a[l]), bd=bd)
            if l > 0:
                w.update(v1=a_v1[l - 1].astype(BF16), v2=a_v2[l - 1].astype(BF16), v0=row2(a_v0[l - 1]))
            r, lw, k, v, kn, b, g = _rwkv_pre(x, xs_prev, v_first if l > 0 else None, w, tm)
            if l == 0:
                v_first = v
            scan_in = (r, lw, k, v, kn, b)
            rp, yp, mm, nn = _wkv_chunks(scan_in, c_p, 0, t_p // c_p, n_heads // 2)
            s0 = jnp.zeros((1, n_heads // 2, PAIR, PAIR), F32)
            y_p, sf_p = _wkv_seq(rp, yp, mm, nn, s0, c_p, 1, t_p // c_p)
            rp, yp, mm, nn = _wkv_chunks(scan_in, c_s, t_p, db, n_heads // 2)
            y_s, sf_s = _wkv_seq(rp, yp, mm, nn, _state_to_blockdiag(state_wkv[l]), c_s, db, 1)
            new_wkv_p.append(_blockdiag_to_state(sf_p))
            new_wkv_s.append(_blockdiag_to_state(sf_s))
            y = jnp.concatenate([y_p, y_s], axis=0)
            consts = [row2(a_lnx_g[l]), row2(a_lnx_b[l]), row2(a_r_k[l]), bd, a_w_o[l].astype(BF16),
                      row2(ln1_g[l]), row2(ln1_b[l]), rw_t, rb_c]
            x1, cls = _mix_post(True, alpha_dn, [x, y, r, k, v, g], consts, tm)
        else:
            lb = l - n_a
            wq = b_w_qg[lb][:, :d].astype(BF16)
            qb, q_aug, gate = _fox_qg(x, kv['rem_r'], wq, _pad_heads(wq, PAIR), b_w_qg[lb][:, d:].astype(BF16), tm)
            j_lo = _oldest_needed_block(qb, kv['k'], kv['rem_r'], kv['base'], t_p, tq_attn, 4)
            o_p = _fox_prompt(kv['base'], j_lo, q_aug, kv['k_aug'], kv['vt_aug'], t_p, tq_attn, 4).T
            o_s = _fox_sample(qb, kv['k'], kv['v'], kv['lf_s'], cache_k.reshape(db, past, d),
                              cache_v.reshape(db, past, d), cache_logf, t_p, db, dt)
            o = jnp.concatenate([o_p, o_s], axis=0)
            consts = [b_w_o[lb].astype(BF16), row2(ln1_g[l]), row2(ln1_b[l]), rw_t, rb_c]
            x1, cls = _mix_post(False, alpha_dn, [x, o, gate], consts, tm)
        m = _moe(x1, cls[0], moe_w_gu[l].astype(BF16), moe_w_down[l].astype(BF16))
        if l == n_a - 1:
            wk, wv = kv_w[:, :d].astype(BF16), kv_w[:, d:2 * d].astype(BF16)
            kvw = dict(wk=wk, wv=wv, wk_aug=_pad_heads(wk, PAIR), wv_aug=_pad_heads(wv, V_ROWS),
                       wf_t=kv_w[:, 2 * d:].T, bf=kv_bf.reshape(-1, 1), tiles_per_blk=tq_attn // tm)
            x, k_all, v_all, k_aug, vt_aug, lf_r, cum_r, rem_r = _ln2(
                alpha_dn, x1, m, row2(ln2_g[l]), row2(ln2_b[l]), kvw, tm)
            base = (cum_r[:, :t_p:tq_attn] * LOG2E).T.reshape(-1)
            kv = dict(k=k_all, v=v_all, base=base, rem_r=rem_r, k_aug=k_aug,
                      vt_aug=vt_aug.reshape(n_heads, V_ROWS, n), lf_p=lf_r[:, :t_p].T,
                      lf_s=lf_r[:, t_p:].T.reshape(db, dt, n_heads))
        else:
            x = _ln2(alpha_dn, x1, m, row2(ln2_g[l]), row2(ln2_b[l]), None, tm)[0]

    y_prompt = x[:t_p].reshape(nb, seq, d)
    y_sample = x[t_p:].reshape(db, dt, d)
    p_k = kv['k'][:t_p].reshape(nb, seq, n_heads, HEAD_DIM)
    p_v = kv['v'][:t_p].reshape(nb, seq, n_heads, HEAD_DIM)
    s_k = kv['k'][t_p:].reshape(db, dt, n_heads, HEAD_DIM)
    s_v = kv['v'][t_p:].reshape(db, dt, n_heads, HEAD_DIM)
    return (y_prompt, y_sample, jnp.stack(new_shift_p), jnp.stack(new_wkv_p), p_k, p_v,
            kv['lf_p'].reshape(nb, seq, n_heads), jnp.stack(new_shift_s), jnp.stack(new_wkv_s),
            s_k, s_v, kv['lf_s'])
```

```python
import functools
import math

import jax
import jax.numpy as jnp
import numpy as np
from jax import lax
from jax.experimental import pallas as pl
from jax.experimental.pallas import tpu as pltpu

F32 = jnp.float32
BF16 = jnp.bfloat16

HEAD_DIM = 64
PAIR = 2 * HEAD_DIM
N_EXPERTS = 16
N_GROUPS = 4
EXPERTS_PER_GROUP = 4
N_CLASSES = 24
E_BLOCK = 128
GATE_LANES = 128
LN_EPS = 1e-5
GN_EPS = 64e-5
NEG_INF = -1e30
LOG2E = 1.4426950408889634
MIB = 2 ** 20

_PAIR_LO = (0, 0, 0, 1, 1, 2)
_PAIR_HI = (1, 2, 3, 2, 3, 3)


def _params(sem, vmem_mib):
    return pltpu.CompilerParams(dimension_semantics=sem, vmem_limit_bytes=vmem_mib * MIB)


def _dot(a, b):
    return jnp.dot(a, b, preferred_element_type=F32)


def _dot_nt(a, b):
    return lax.dot_general(a, b, (((1,), (1,)), ((), ())), preferred_element_type=F32)


def _dot_tn(a, b):
    return lax.dot_general(a, b, (((0,), (0,)), ((), ())), preferred_element_type=F32)


def _split(x):
    hi = x.astype(BF16)
    lo = (x - hi.astype(F32)).astype(BF16)
    return hi, lo


def _dot3(a, b, dot=_dot):
    return _dot3_multi(a, [b], dot)[0]


def _dot3_multi(a, bs, dot=_dot):
    ah, al = a
    ax = 0 if dot is _dot_nt else 1
    ns = [b[0].shape[ax] for b in bs]
    if any(n % 128 for n in ns):
        return [dot(ah, bh) + (dot(ah, bl) + dot(al, bh)) for bh, bl in bs]
    kax = 0 if dot is _dot_tn else 1
    if ah.shape[kax] == 128:
        lhs = jnp.concatenate([ah, al], axis=kax)
        if dot is _dot_nt:
            rhs = jnp.concatenate([jnp.concatenate([x, y], axis=1) for bh, bl in bs
                                   for x, y in ((bh, bh), (bl, jnp.zeros_like(bl)))], axis=0)
        else:
            top = jnp.concatenate([x for b in bs for x in b], axis=1)
            bot = jnp.concatenate([x for bh, bl in bs for x in (bh, jnp.zeros_like(bl))], axis=1)
            rhs = jnp.concatenate([top, bot], axis=0)
        r = dot(lhs, rhs)
        outs, o = [], 0
        for n in ns:
            outs.append(r[:, o:o + n] + r[:, o + n:o + 2 * n])
            o += 2 * n
        return outs
    r1 = dot(ah, jnp.concatenate([x for b in bs for x in b], axis=ax))
    r2 = dot(al, jnp.concatenate([b[0] for b in bs], axis=ax)) if len(bs) > 1 else dot(al, bs[0][0])
    outs, o1, o2 = [], 0, 0
    for n in ns:
        outs.append((r1[:, o1:o1 + n] + r1[:, o1 + n:o1 + 2 * n]) + r2[:, o2:o2 + n])
        o1 += 2 * n
        o2 += n
    return outs


def _dot_exact_rhs(x, m, dot=_dot):
    hi = x.astype(BF16)
    r1 = x - hi.astype(F32)
    mid = r1.astype(BF16)
    lo = (r1 - mid.astype(F32)).astype(BF16)
    return dot(hi, m) + (dot(mid, m) + dot(lo, m))


def _head_sum(x, bd_ref):
    bd = bd_ref[...]
    parts = []
    for j in range(x.shape[1] // 256):
        parts.append(_dot_exact_rhs(x[:, 256 * j:256 * (j + 1)], bd))
    return jnp.concatenate(parts, axis=1)


def _layer_norm(z, g, b):
    mu = jnp.mean(z, axis=-1, keepdims=True)
    zc = z - mu
    var = jnp.mean(zc * zc, axis=-1, keepdims=True)
    return zc * lax.rsqrt(var + LN_EPS) * g + b


def _row_tile(n):
    for t in (256, 128, 64, 32, 16, 8):
        if n % t == 0:
            return t
    raise ValueError(n)


def _const_spec(shape):
    nd = len(shape)
    return pl.BlockSpec(shape, lambda *_: (0,) * nd)


def _rwkv_pre_kernel(has_vres, n_prompt_tiles, *refs):
    if has_vres:
        (x_ref, x8_ref, xtail_ref, vf_ref, mix_ref, wr_ref, wk_ref, wv_ref, w1_ref, w2_ref, w0_ref,
         a1_ref, a2_ref, a0_ref, g1_ref, g2_ref, v1_ref, v2_ref, v0_ref, kk_ref, ka_ref, bd_ref,
         r_out, lw_out, k_out, v_out, kn_out, b_out, g_out) = refs
    else:
        (x_ref, x8_ref, xtail_ref, mix_ref, wr_ref, wk_ref, wv_ref, w1_ref, w2_ref, w0_ref,
         a1_ref, a2_ref, a0_ref, g1_ref, g2_ref, kk_ref, ka_ref, bd_ref,
         r_out, lw_out, k_out, v_out, kn_out, b_out, g_out) = refs
    x = x_ref[...]
    i = pl.program_id(0)
    first = lax.broadcasted_iota(jnp.int32, x.shape, 0) == 0
    above = jnp.where(i == 0, 0.0, x8_ref[7:8, :])
    shifted = jnp.where(first, above, pltpu.roll(x, 1, 0))
    xx = jnp.where(i >= n_prompt_tiles, xtail_ref[...], shifted) - x

    def mixed(i):
        return (x + xx * mix_ref[i:i + 1, :]).astype(BF16)

    xr, xw, xk, xv, xa, xg = (mixed(i) for i in range(6))
    r = _dot(xr, wr_ref[...])
    k = _dot(xk, wk_ref[...])
    v = _dot(xv, wv_ref[...])
    zw = w0_ref[...] + _dot(jnp.tanh(_dot(xw, w1_ref[...])).astype(BF16), w2_ref[...])
    lw = (-math.exp(-0.5)) * jax.nn.sigmoid(zw)
    alpha = jax.nn.sigmoid(a0_ref[...] + _dot(_dot(xa, a1_ref[...]).astype(BF16), a2_ref[...]))
    g = _dot(jax.nn.sigmoid(_dot(xg, g1_ref[...])).astype(BF16), g2_ref[...])
    if has_vres:
        gate_v = jax.nn.sigmoid(v0_ref[...] + _dot(_dot(xv, v1_ref[...]).astype(BF16), v2_ref[...]))
        v = v + (vf_ref[...] - v) * gate_v
    kkr = k * kk_ref[...]
    norm = jnp.sqrt(_head_sum(kkr * kkr, bd_ref))
    kn = kkr / jnp.maximum(norm, 1e-12)
    k2 = k * (1.0 + (alpha - 1.0) * ka_ref[...])
    r_out[...] = r
    lw_out[...] = lw
    k_out[...] = k2
    v_out[...] = v
    kn_out[...] = kn
    b_out[...] = kn * alpha
    g_out[...] = g


def _rwkv_pre(x, x_tail_prev, vfirst, w, tm):
    n, d = x.shape
    t_tail = x_tail_prev.shape[0]
    assert t_tail % tm == 0 and (n - t_tail) % tm == 0
    n_prompt_tiles = (n - t_tail) // tm
    has_vres = vfirst is not None
    row = pl.BlockSpec((tm, d), lambda i: (i, 0))
    above = pl.BlockSpec((8, d), lambda i: (jnp.maximum(i * (tm // 8) - 1, 0), 0))
    tail = pl.BlockSpec((tm, d), lambda i: (jnp.maximum(i - n_prompt_tiles, 0), 0))
    ins = [x, x, x_tail_prev] + ([vfirst] if has_vres else [])
    specs = [row, above, tail] + ([row] if has_vres else [])
    names = ['mix', 'wr', 'wk', 'wv', 'w1', 'w2', 'w0', 'a1', 'a2', 'a0', 'g1', 'g2']
    if has_vres:
        names += ['v1', 'v2', 'v0']
    names += ['k_k', 'k_a', 'bd']
    for nm in names:
        ins.append(w[nm])
        specs.append(_const_spec(w[nm].shape))
    out = jax.ShapeDtypeStruct((n, d), F32)
    return pl.pallas_call(
        functools.partial(_rwkv_pre_kernel, has_vres, n_prompt_tiles),
        grid=(n // tm,),
        in_specs=specs,
        out_specs=[row] * 7,
        out_shape=[out] * 7,
        compiler_params=_params(("parallel",), 56),
        name="rwkv_pre",
    )(*ins)


def _wkv_chunk_kernel(C, n_pairs, r_ref, lw_ref, k_ref, v_ref, kn_ref, b_ref,
                      rp_out, yp_out, m_out, n_out):
    C2 = 2 * C
    row = lax.broadcasted_iota(jnp.int32, (C2, C2), 0)
    col = lax.broadcasted_iota(jnp.int32, (C2, C2), 1)
    same = (row >= C) == (col >= C)
    strict = jnp.logical_and(same, col < row)
    incl = jnp.logical_and(same, col <= row)
    tri_incl = jnp.where(incl, 1.0, 0.0).astype(BF16)
    eye_c = jnp.where(row == col, 1.0, 0.0).astype(F32)
    r128 = lax.broadcasted_iota(jnp.int32, (PAIR, PAIR), 0)
    c128 = lax.broadcasted_iota(jnp.int32, (PAIR, PAIR), 1)
    eye_p = r128 == c128
    head0 = lax.broadcasted_iota(jnp.int32, (1, PAIR), 1) < HEAD_DIM

    def stack(t):
        return jnp.concatenate([jnp.where(head0, t, 0.0), jnp.where(head0, 0.0, t)], axis=0)

    n_sq = int(math.log2(C)) - 1
    sls = [slice(PAIR * p, PAIR * (p + 1)) for p in range(n_pairs)]

    def each(f, *lists):
        return [f(*a) for a in zip(*lists)]

    def load(ref):
        return [stack(ref[:, sl]) for sl in sls]

    lws = load(lw_ref)
    L = each(lambda x: _dot_exact_rhs_left(tri_incl, x), lws)
    lc = each(lambda l: l[C - 1:C, :] + l[C2 - 1:C2, :], L)
    kn, bs, ks = load(kn_ref), load(b_ref), load(k_ref)
    at = each(lambda n, l, w: _split(-n * jnp.exp(l - w)), kn, L, lws)
    rt_f = each(lambda r, l: r * jnp.exp(l), load(r_ref), L)
    rt = each(_split, rt_f)
    e_nl = each(lambda l: jnp.exp(-l), L)
    bb = each(lambda b, e: _split(b * e), bs, e_nl)
    kb = each(lambda k, e: _split(k * e), ks, e_nl)
    e_lc = each(lambda c, l: jnp.exp(c - l), lc, L)
    bh = each(lambda b, e: _split(b * e), bs, e_lc)
    kh = each(lambda k, e: _split(k * e), ks, e_lc)
    vs = each(_split, load(v_ref))

    g_a = each(lambda a, b, k: _dot3_multi(a, [b, k], _dot_nt), at, bb, kb)
    g_r = each(lambda a, b, k: _dot3_multi(a, [b, k], _dot_nt), rt, bb, kb)
    a_ab = [jnp.where(strict, g[0], 0.0) for g in g_a]
    a_ak = [_split(jnp.where(strict, g[1], 0.0)) for g in g_a]
    a_rb = [_split(jnp.where(incl, g[0], 0.0)) for g in g_r]
    a_rk = [_split(jnp.where(incl, g[1], 0.0)) for g in g_r]

    pws = each(lambda a: _split(_dot3(a, a)), each(_split, a_ab))
    tm = each(lambda a: eye_c + a, a_ab)
    for step in range(n_sq):
        if step < n_sq - 1:
            res = each(lambda s, t: _dot3_multi(s, [_split(t), s]), pws, tm)
            tm = [t + r[0] for t, r in zip(tm, res)]
            pws = [_split(r[1]) for r in res]
        else:
            tm = each(lambda t, s: t + _dot3(s, _split(t)), tm, pws)
    tms = each(_split, tm)
    w1 = each(lambda a, v: _split(_dot3(a, v)), a_ak, vs)
    tx = each(lambda t, a, w: _dot3_multi(t, [a, w]), tms, at, w1)
    at2 = [_split(r[0]) for r in tx]
    v2 = [_split(r[1]) for r in tx]
    ax = each(lambda a, x, v: _dot3_multi(a, [x, v]), a_rb, at2, v2)
    r2 = [r + x[0] for r, x in zip(rt_f, ax)]
    y2 = [x[1] + _dot3(c, v) for x, c, v in zip(ax, a_rk, vs)]
    bx = each(lambda b, x, v: _dot3_multi(b, [x, v], _dot_tn), bh, at2, v2)
    mm = [jnp.where(eye_p, jnp.exp(c), 0.0) + x[0] for c, x in zip(lc, bx)]
    nn = [x[1] + _dot3(k, v, _dot_tn) for x, k, v in zip(bx, kh, vs)]
    rp_out[...] = jnp.concatenate([x[:C, :] + x[C:, :] for x in r2], axis=1)
    yp_out[...] = jnp.concatenate([x[:C, :] + x[C:, :] for x in y2], axis=1)
    m_out[0] = jnp.stack(mm, axis=0)
    n_out[0] = jnp.stack(nn, axis=0)


def _dot_exact_rhs_left(m, x):
    hi = x.astype(BF16)
    r1 = x - hi.astype(F32)
    mid = r1.astype(BF16)
    lo = (r1 - mid.astype(F32)).astype(BF16)
    return _dot(m, hi) + (_dot(m, mid) + _dot(m, lo))


def _wkv_chunks(arrs, C, row0, n_chunks, pairs_per_step):
    d = arrs[0].shape[1]
    n_pairs = d // PAIR
    assert row0 % C == 0 and n_pairs % pairs_per_step == 0
    blk0 = row0 // C
    lanes = pairs_per_step * PAIR
    in_spec = pl.BlockSpec((C, lanes), lambda c, q: (blk0 + c, q))
    out_row = pl.BlockSpec((C, lanes), lambda c, q: (c, q))
    out_mat = pl.BlockSpec((1, pairs_per_step, PAIR, PAIR), lambda c, q: (c, q, 0, 0))
    t = n_chunks * C
    return pl.pallas_call(
        functools.partial(_wkv_chunk_kernel, C, pairs_per_step),
        grid=(n_chunks, n_pairs // pairs_per_step),
        in_specs=[in_spec] * 6,
        out_specs=[out_row, out_row, out_mat, out_mat],
        out_shape=[jax.ShapeDtypeStruct((t, d), F32), jax.ShapeDtypeStruct((t, d), F32),
                   jax.ShapeDtypeStruct((n_chunks, n_pairs, PAIR, PAIR), F32),
                   jax.ShapeDtypeStruct((n_chunks, n_pairs, PAIR, PAIR), F32)],
        compiler_params=_params(("parallel", "parallel"), 32),
        name=f"wkv_chunks_c{C}",
    )(*arrs)


def _wkv_seq_kernel(n_pairs, n_steps, rp_ref, yp_ref, m_ref, n_ref, s0_ref, y_out, s_out, s_scr):
    j = pl.program_id(1)

    @pl.when(j == 0)
    def _():
        s_scr[...] = s0_ref[0]

    ys, new_s = [], []
    for p in range(n_pairs):
        sl = slice(PAIR * p, PAIR * (p + 1))
        ss = _split(s_scr[p])
        ys.append(_dot3(_split(rp_ref[:, sl]), ss) + yp_ref[:, sl])
        new_s.append(_dot3(_split(m_ref[0, p]), ss) + n_ref[0, p])
    y_out[...] = jnp.concatenate(ys, axis=1)
    s_scr[...] = jnp.stack(new_s, axis=0)

    @pl.when(j == n_steps - 1)
    def _():
        s_out[0] = s_scr[...]


def _wkv_seq(rp, yp, m, nn, s0, C, n_seq, n_steps):
    t, d = rp.shape
    n_pairs = d // PAIR
    row = pl.BlockSpec((C, d), lambda s, j: (s * n_steps + j, 0))
    mat = pl.BlockSpec((1, n_pairs, PAIR, PAIR), lambda s, j: (s * n_steps + j, 0, 0, 0))
    st = pl.BlockSpec((1, n_pairs, PAIR, PAIR), lambda s, j: (s, 0, 0, 0))
    return pl.pallas_call(
        functools.partial(_wkv_seq_kernel, n_pairs, n_steps),
        grid=(n_seq, n_steps),
        in_specs=[row, row, mat, mat, st],
        out_specs=[row, st],
        out_shape=[jax.ShapeDtypeStruct((t, d), F32),
                   jax.ShapeDtypeStruct((n_seq, n_pairs, PAIR, PAIR), F32)],
        scratch_shapes=[pltpu.VMEM((n_pairs, PAIR, PAIR), F32)],
        compiler_params=_params(("arbitrary", "arbitrary"), 32),
        name=f"wkv_seq_c{C}",
    )(rp, yp, m, nn, s0)


def _state_to_blockdiag(s):
    b, h, n, _ = s.shape
    st = jnp.swapaxes(s, -1, -2).reshape(b, h // 2, 2, n, n)
    z = jnp.zeros_like(st[:, :, 0])
    top = jnp.concatenate([st[:, :, 0], z], axis=-1)
    bot = jnp.concatenate([z, st[:, :, 1]], axis=-1)
    return jnp.concatenate([top, bot], axis=-2)


def _blockdiag_to_state(bd):
    b, hp, _, _ = bd.shape
    n = HEAD_DIM
    s0 = bd[:, :, :n, :n]
    s1 = bd[:, :, n:, n:]
    st = jnp.stack([s0, s1], axis=2).reshape(b, hp * 2, n, n)
    return jnp.swapaxes(st, -1, -2)


def _route(x1, rw_ref, rb_ref):
    logits = _dot3(_split(rw_ref[...]), _split(x1), _dot_nt)
    mx = jnp.max(logits, axis=0, keepdims=True)
    ex = jnp.exp(logits - mx)
    scores = ex / jnp.sum(ex, axis=0, keepdims=True)
    sel = scores + rb_ref[...]
    rows = [sel[e:e + 1, :] for e in range(N_EXPERTS)]
    srow = [scores[e:e + 1, :] for e in range(N_EXPERTS)]

    def top2(vals):
        m1 = jnp.maximum(jnp.maximum(vals[0], vals[1]), jnp.maximum(vals[2], vals[3]))
        i1 = jnp.where(vals[0] == m1, 0, jnp.where(vals[1] == m1, 1, jnp.where(vals[2] == m1, 2, 3)))
        rest = [jnp.where(i1 == j, -jnp.inf, vals[j]) for j in range(4)]
        m2 = jnp.maximum(jnp.maximum(rest[0], rest[1]), jnp.maximum(rest[2], rest[3]))
        i2 = jnp.where(rest[0] == m2, 0, jnp.where(rest[1] == m2, 1, jnp.where(rest[2] == m2, 2, 3)))
        return m1, i1, m2, i2

    gscore = []
    for gidx in range(N_GROUPS):
        m1, _, m2, _ = top2(rows[4 * gidx:4 * gidx + 4])
        gscore.append(m1 + m2)
    gm = jnp.maximum(jnp.maximum(gscore[0], gscore[1]), jnp.maximum(gscore[2], gscore[3]))
    gi = jnp.where(gscore[0] == gm, 0, jnp.where(gscore[1] == gm, 1, jnp.where(gscore[2] == gm, 2, 3)))

    def pick(rws, j):
        return jnp.where(gi == 0, rws[j], jnp.where(gi == 1, rws[4 + j],
                                                    jnp.where(gi == 2, rws[8 + j], rws[12 + j])))

    in_grp = [pick(rows, j) for j in range(4)]
    sc_grp = [pick(srow, j) for j in range(4)]
    _, i1, _, i2 = top2(in_grp)

    def at(vals, idx):
        return jnp.where(idx == 0, vals[0], jnp.where(idx == 1, vals[1],
                                                      jnp.where(idx == 2, vals[2], vals[3])))

    ga = at(sc_grp, i1)
    gb = at(sc_grp, i2)
    tot = ga + gb
    ga = ga / tot
    gb = gb / tot
    lo = jnp.minimum(i1, i2)
    hi = jnp.maximum(i1, i2)
    g_lo = jnp.where(i1 < i2, ga, gb)
    g_hi = jnp.where(i1 < i2, gb, ga)
    pair = jnp.where(lo == 0, hi - 1, jnp.where(lo == 1, hi + 1, 5))
    cls = gi * 6 + pair
    return cls.astype(jnp.int32), g_lo, g_hi


def _mix_post_kernel(is_rwkv, alpha_dn, *refs):
    if is_rwkv:
        (x_ref, y_ref, r_ref, k_ref, v_ref, g_ref, lg_ref, lb_ref, rk_ref, bd_ref,
         wo_ref, n1g_ref, n1b_ref, rw_ref, rb_ref, x1_out, cls_out) = refs
        y = y_ref[...]
        mu = _head_sum(y, bd_ref) * (1.0 / HEAD_DIM)
        yc = y - mu
        var = _head_sum(yc * yc, bd_ref) * (1.0 / HEAD_DIM)
        yn = yc * lax.rsqrt(var + GN_EPS) * lg_ref[...] + lb_ref[...]
        v = v_ref[...]
        bonus = _head_sum(r_ref[...] * k_ref[...] * rk_ref[...], bd_ref)
        z = (yn + bonus * v) * g_ref[...]
    else:
        (x_ref, o_ref, gt_ref, wo_ref, n1g_ref, n1b_ref, rw_ref, rb_ref,
         x1_out, cls_out) = refs
        z = o_ref[...] * gt_ref[...]
    h = _dot(z.astype(BF16), wo_ref[...])
    x1 = _layer_norm(alpha_dn * x_ref[...] + h, n1g_ref[...], n1b_ref[...])
    cls, g_lo, g_hi = _route(x1, rw_ref, rb_ref)
    tm, d = x1.shape
    cls_out[...] = jnp.broadcast_to(cls, (8, tm))
    pieces = [x.astype(F32) for x in _split3(g_lo) + _split3(g_hi)]
    gs = jnp.concatenate(pieces + [jnp.zeros((16 - len(pieces), tm), F32)], axis=0).astype(BF16)
    rr = lax.broadcasted_iota(jnp.int32, (16, GATE_LANES), 0)
    cc = lax.broadcasted_iota(jnp.int32, (16, GATE_LANES), 1)
    place = jnp.where(jnp.logical_and(rr < 6, cc == jnp.where(rr < 3, 0, 1)), 1.0, 0.0).astype(BF16)
    x1_out[:, :d] = x1
    x1_out[:, d:] = _dot_tn(gs, place)


def _mix_post(is_rwkv, alpha_dn, acts, consts, tm):
    n, d = acts[0].shape
    row = pl.BlockSpec((tm, d), lambda i: (i, 0))
    specs = [row] * len(acts) + [_const_spec(c.shape) for c in consts]
    lane = pl.BlockSpec((8, tm), lambda i: (0, i))
    wide = pl.BlockSpec((tm, d + GATE_LANES), lambda i: (i, 0))
    return pl.pallas_call(
        functools.partial(_mix_post_kernel, is_rwkv, alpha_dn),
        grid=(n // tm,),
        in_specs=specs,
        out_specs=[wide, lane],
        out_shape=[jax.ShapeDtypeStruct((n, d + GATE_LANES), F32), jax.ShapeDtypeStruct((8, n), jnp.int32)],
        compiler_params=_params(("parallel",), 48),
        name="mix_post_rwkv" if is_rwkv else "mix_post_fox",
    )(*acts, *consts)


def _moe_kernel(n_tok, d, d_exp, n_blk, e1_ref, e2_ref, off_ref, nv_ref, order_ref, x_hbm,
                wgu1_ref, wd1_ref, wgu2_ref, wd2_ref, out_hbm, xbuf, obuf, gsem, ssem):
    b = pl.program_id(0)
    slot = lax.rem(b, 2)
    nv_b = nv_ref[b]
    unroll = 8

    def gather_start(blk, sl):
        base = off_ref[blk]

        def body(c, carry):
            for u in range(unroll):
                i = c * unroll + u
                tok = order_ref[jnp.minimum(base + i, n_tok - 1)]
                pltpu.make_async_copy(x_hbm.at[pl.ds(tok, 1)], xbuf.at[sl, pl.ds(i, 1)], gsem.at[sl]).start()
            return carry

        lax.fori_loop(0, E_BLOCK // unroll, body, 0)

    def gather_wait(sl):
        pltpu.make_async_copy(x_hbm.at[pl.ds(0, E_BLOCK)], xbuf.at[sl], gsem.at[sl]).wait()

    def scatter_start(blk, sl):
        base = off_ref[blk]

        nv = nv_ref[blk]

        def row(i, priority):
            tok = order_ref[base + i]
            pltpu.make_async_copy(obuf.at[sl, pl.ds(i, 1)], out_hbm.at[pl.ds(tok, 1)],
                                  ssem.at[sl]).start(priority=priority)

        def body(c, carry):
            row(2 * c, 0)
            row(2 * c + 1, 1)
            return carry

        lax.fori_loop(0, lax.shift_right_logical(nv, 1), body, 0)

        @pl.when(lax.rem(nv, 2) == 1)
        def _():
            row(nv - 1, 0)

    def scatter_wait(blk, sl):
        nv = nv_ref[blk]

        @pl.when(nv == E_BLOCK)
        def _():
            pltpu.make_async_copy(obuf.at[sl], out_hbm.at[pl.ds(0, E_BLOCK)], ssem.at[sl]).wait()

        @pl.when(nv < E_BLOCK)
        def _():
            def body(i, carry):
                pltpu.make_async_copy(obuf.at[sl, pl.ds(0, 1)], out_hbm.at[pl.ds(0, 1)], ssem.at[sl]).wait()
                return carry

            lax.fori_loop(0, nv, body, 0)

    @pl.when(jnp.logical_and(b == 0, nv_b > 0))
    def _():
        gather_start(0, 0)

    nxt = jnp.minimum(b + 1, n_blk - 1)

    @pl.when(jnp.logical_and(b + 1 < n_blk, nv_ref[nxt] > 0))
    def _():
        gather_start(nxt, 1 - slot)

    @pl.when(nv_b > 0)
    def _():
        gather_wait(slot)
        rows = xbuf[slot]
        xb = rows[:, :d].astype(BF16)

        def expert(wgu_ref, wd_ref):
            gu = _dot(xb, wgu_ref[0])
            hmid = jax.nn.silu(gu[:, :d_exp]) * gu[:, d_exp:]
            return _dot(hmid.astype(BF16), wd_ref[0])

        y1 = expert(wgu1_ref, wd1_ref)
        y2 = expert(wgu2_ref, wd2_ref)
        obuf[slot] = y1 * rows[:, d:d + 1] + y2 * rows[:, d + 1:d + 2]

    prev = jnp.maximum(b - 1, 0)

    @pl.when(jnp.logical_and(b > 0, nv_ref[prev] > 0))
    def _():
        scatter_wait(prev, 1 - slot)

    @pl.when(nv_b > 0)
    def _():
        scatter_start(b, slot)

    @pl.when(jnp.logical_and(b == n_blk - 1, nv_b > 0))
    def _():
        scatter_wait(b, slot)


def _moe(xg, cls, w_gu, w_down):
    n = xg.shape[0]
    d, d_exp = w_down.shape[2], w_down.shape[1]
    n_blk = (n + N_CLASSES * (E_BLOCK - 1) + E_BLOCK - 1) // E_BLOCK
    counts = jnp.sum((cls[:, None] == jnp.arange(N_CLASSES, dtype=jnp.int32)[None, :]).astype(jnp.int32), axis=0)
    padded = (counts + E_BLOCK - 1) // E_BLOCK * E_BLOCK
    pad_end = jnp.cumsum(padded)
    pad_start = pad_end - padded
    seg_start = jnp.cumsum(counts) - counts
    order = jnp.argsort(cls).astype(jnp.int32)
    blk_start = jnp.arange(n_blk, dtype=jnp.int32) * E_BLOCK
    blk_cls = jnp.minimum(jnp.sum((pad_end[None, :] <= blk_start[:, None]).astype(jnp.int32), axis=1),
                          N_CLASSES - 1)
    classes = jnp.arange(N_CLASSES, dtype=jnp.int32)

    def lookup(table, idx):
        return jnp.sum(jnp.where(idx[:, None] == classes[None, :], table[None, :], 0), axis=1)

    rank0 = blk_start - lookup(pad_start, blk_cls)
    blk_nv = jnp.clip(lookup(counts, blk_cls) - rank0, 0, E_BLOCK).astype(jnp.int32)
    blk_off = (lookup(seg_start, blk_cls) + rank0).astype(jnp.int32)
    last_cls = jnp.max(jnp.where(blk_nv > 0, blk_cls, 0))
    blk_cls = jnp.where(blk_nv > 0, blk_cls, last_cls)
    e_lo = jnp.asarray([(c // 6) * EXPERTS_PER_GROUP + _PAIR_LO[c % 6] for c in range(N_CLASSES)], jnp.int32)
    e_hi = jnp.asarray([(c // 6) * EXPERTS_PER_GROUP + _PAIR_HI[c % 6] for c in range(N_CLASSES)], jnp.int32)
    blk_e1 = lookup(e_lo, blk_cls)
    blk_e2 = lookup(e_hi, blk_cls)

    gu1 = pl.BlockSpec((1, d, 2 * d_exp), lambda b, e1, e2, of, nv, od: (e1[b], 0, 0))
    dn1 = pl.BlockSpec((1, d_exp, d), lambda b, e1, e2, of, nv, od: (e1[b], 0, 0))
    gu2 = pl.BlockSpec((1, d, 2 * d_exp), lambda b, e1, e2, of, nv, od: (e2[b], 0, 0))
    dn2 = pl.BlockSpec((1, d_exp, d), lambda b, e1, e2, of, nv, od: (e2[b], 0, 0))
    grid_spec = pltpu.PrefetchScalarGridSpec(
        num_scalar_prefetch=5,
        grid=(n_blk,),
        in_specs=[pl.BlockSpec(memory_space=pl.ANY), gu1, dn1, gu2, dn2],
        out_specs=pl.BlockSpec(memory_space=pl.ANY),
        scratch_shapes=[pltpu.VMEM((2, E_BLOCK, d + GATE_LANES), F32), pltpu.VMEM((2, E_BLOCK, d), F32),
                        pltpu.SemaphoreType.DMA((2,)), pltpu.SemaphoreType.DMA((2,))],
    )
    return pl.pallas_call(
        functools.partial(_moe_kernel, n, d, d_exp, n_blk),
        grid_spec=grid_spec,
        out_shape=jax.ShapeDtypeStruct((n, d), F32),
        compiler_params=_params(("arbitrary",), 40),
        name="moe_experts",
    )(blk_e1, blk_e2, blk_off, blk_nv, order, xg, w_gu, w_down, w_gu, w_down)


def _bias_placement(nh, lead_ones):
    p = np.zeros((4 * nh, nh * PAIR), np.float32)
    for h in range(nh):
        lane0 = PAIR * h + HEAD_DIM
        if lead_ones:
            p[h, lane0:lane0 + 3] = 1.0
            for piece in range(3):
                p[(piece + 1) * nh + h, lane0 + 3 + piece] = 1.0
        else:
            for piece in range(3):
                p[piece * nh + h, lane0 + piece] = 1.0
            p[3 * nh + h, lane0 + 3:lane0 + 6] = 1.0
    return jnp.asarray(p, BF16)


def _pad_heads(w, width):
    d, hd = w.shape
    nh = hd // HEAD_DIM
    w3 = w.reshape(d, nh, HEAD_DIM)
    return jnp.concatenate([w3, jnp.zeros((d, nh, width - HEAD_DIM), w.dtype)], axis=-1).reshape(d, nh * width)


def _bias_rows(rem, lead_ones):
    parts = [x.astype(F32) for x in _split3(rem)]
    ones = [jnp.ones_like(rem)]
    return jnp.concatenate(ones + parts if lead_ones else parts + ones, axis=0).astype(BF16)


def _ln2_kernel(alpha_dn, with_kv, tiles_per_blk, *refs):
    if with_kv:
        (x1_ref, m_ref, g_ref, b_ref, wk_ref, wv_ref, wka_ref, wva_ref, wf_ref, bf_ref, tri_ref,
         pk_ref, eye_ref, ones_ref,
         x2_out, k_out, v_out, ka_out, vt_out, lf_out, cum_out, rem_out, carry, base_scr) = refs
    else:
        x1_ref, m_ref, g_ref, b_ref, x2_out = refs
    x2 = _layer_norm(alpha_dn * x1_ref[...] + m_ref[...], g_ref[...], b_ref[...])
    x2_out[...] = x2
    if with_kv:
        i = pl.program_id(0)
        xb = x2.astype(BF16)
        k_out[...] = _dot(xb, wk_ref[...])
        v_out[...] = _dot(xb, wv_ref[...])
        z = _dot3(_split(wf_ref[...]), _split(x2), _dot_nt) + bf_ref[...]
        lf = jnp.minimum(z, 0.0) - jnp.log1p(jnp.exp(-jnp.abs(z)))
        lf_out[...] = lf

        @pl.when(i == 0)
        def _():
            carry[...] = jnp.zeros_like(carry)

        cum = _dot_exact_rhs(lf, tri_ref[...]) + carry[...]
        cum_out[...] = cum
        carry[...] = cum[:, -1:]

        cum2 = cum * LOG2E

        @pl.when(lax.rem(i, tiles_per_blk) == 0)
        def _():
            base_scr[...] = cum2[:, :1]

        rem = cum2 - base_scr[...]
        rem_out[...] = rem
        ka_out[...] = (_dot(xb, wka_ref[...]) + _dot_tn(_bias_rows(-rem, False), pk_ref[...])).astype(BF16)
        vw = _dot(xb, wva_ref[...]).astype(BF16)
        vt_out[...] = (_dot_tn(vw, eye_ref[...]) + ones_ref[...]).astype(BF16)


def _ln2(alpha_dn, x1, m, g, b, kv, tm):
    n, d = m.shape
    row = pl.BlockSpec((tm, d), lambda i: (i, 0))
    with_kv = kv is not None
    ins = [x1, m, g, b]
    specs = [row, row, _const_spec(g.shape), _const_spec(b.shape)]
    outs = [jax.ShapeDtypeStruct((n, d), F32)]
    ospecs = [row]
    scratch = []
    if with_kv:
        nh = kv['wf_t'].shape[0]
        tri = (jnp.arange(tm)[:, None] <= jnp.arange(tm)[None, :]).astype(BF16)
        eye = jnp.eye(tm, dtype=BF16)
        ones_col = (jnp.arange(nh * V_ROWS) % V_ROWS == HEAD_DIM).astype(F32).reshape(-1, 1)
        extra = [kv['wk'], kv['wv'], kv['wk_aug'], kv['wv_aug'], kv['wf_t'], kv['bf'], tri,
                 _bias_placement(nh, False), eye, ones_col]
        ins += extra
        specs += [_const_spec(e.shape) for e in extra]
        lane = pl.BlockSpec((nh, tm), lambda i: (0, i))
        outs += [jax.ShapeDtypeStruct((n, d), F32), jax.ShapeDtypeStruct((n, d), F32),
                 jax.ShapeDtypeStruct((n, nh * PAIR), BF16), jax.ShapeDtypeStruct((nh * V_ROWS, n), BF16),
                 jax.ShapeDtypeStruct((nh, n), F32), jax.ShapeDtypeStruct((nh, n), F32),
                 jax.ShapeDtypeStruct((nh, n), F32)]
        ospecs += [row, row, pl.BlockSpec((tm, nh * PAIR), lambda i: (i, 0)),
                   pl.BlockSpec((nh * V_ROWS, tm), lambda i: (0, i)), lane, lane, lane]
        scratch = [pltpu.VMEM((nh, 1), F32), pltpu.VMEM((nh, 1), F32)]
    res = pl.pallas_call(
        functools.partial(_ln2_kernel, alpha_dn, with_kv, kv['tiles_per_blk'] if with_kv else 1),
        grid=(n // tm,),
        in_specs=specs,
        out_specs=ospecs,
        out_shape=outs,
        scratch_shapes=scratch,
        compiler_params=_params(("arbitrary",), 48),
        name="ln2_kv" if with_kv else "ln2",
    )(*ins)
    return res


def _fox_qg_kernel(x_ref, rem_ref, wq_ref, wqa_ref, wg_ref, pq_ref, q_out, qa_out, gate_out):
    xb = x_ref[...].astype(BF16)
    scale = LOG2E * HEAD_DIM ** -0.5
    q_out[...] = (_dot(xb, wq_ref[...]) * scale).astype(BF16)
    qa_out[...] = (_dot(xb, wqa_ref[...]) * scale
                   + _dot_tn(_bias_rows(rem_ref[...], True), pq_ref[...])).astype(BF16)
    gate_out[...] = jax.nn.sigmoid(_dot(xb, wg_ref[...]))


def _fox_qg(x, rem_r, wq, wq_aug, wg, tm):
    n, d = x.shape
    nh = rem_r.shape[0]
    row = pl.BlockSpec((tm, d), lambda i: (i, 0))
    pq = _bias_placement(nh, True)
    return pl.pallas_call(
        _fox_qg_kernel,
        grid=(n // tm,),
        in_specs=[row, pl.BlockSpec((nh, tm), lambda i: (0, i)), _const_spec(wq.shape),
                  _const_spec(wq_aug.shape), _const_spec(wg.shape), _const_spec(pq.shape)],
        out_specs=[row, pl.BlockSpec((tm, nh * PAIR), lambda i: (i, 0)), row],
        out_shape=[jax.ShapeDtypeStruct((n, d), BF16), jax.ShapeDtypeStruct((n, nh * PAIR), BF16),
                   jax.ShapeDtypeStruct((n, d), F32)],
        compiler_params=_params(("parallel",), 48),
        name="fox_qg",
    )(x, rem_r, wq, wq_aug, wg, pq)


V_ROWS = 80


def _fox_prompt_kernel(tq, hps, nh, base_ref, jlo_ref, q_ref, k_ref, vt_ref, o_out, s_scr, p_scr):
    g = pl.program_id(0)
    i = pl.program_id(1)
    j_lo = jlo_ref[g * pl.num_programs(1) + i]
    rr = lax.broadcasted_iota(jnp.int32, (tq, tq), 0)
    cc = lax.broadcasted_iota(jnp.int32, (tq, tq), 1)
    causal = rr <= cc

    def scores(j, h):
        k0 = pl.multiple_of(j * tq, tq)
        lanes = slice(PAIR * h, PAIR * (h + 1))
        return _dot_nt(k_ref[pl.ds(k0, tq), lanes], q_ref[:, lanes])

    def base_gap(j, h):
        head = g * hps + h
        return base_ref[i * nh + head] - base_ref[j * nh + head]

    def softmax(s, m, gap):
        m_new = jnp.maximum(m, jnp.max(s, axis=0, keepdims=True) + gap)
        return jnp.exp2(s - (m_new - gap)).astype(BF16), jnp.exp2(m - m_new), m_new

    def accum(acc, corr, p, j, h):
        k0 = pl.multiple_of(j * tq, tq)
        return acc * corr + _dot(vt_ref[h, :, pl.ds(k0, tq)], p)

    init = []
    for h in range(hps):
        s = jnp.where(causal, scores(i, h), NEG_INF)
        m0 = jnp.max(s, axis=0, keepdims=True)
        p_scr[h] = jnp.exp2(s - m0).astype(BF16)
        s_scr[h] = scores(jnp.maximum(i - 1, 0), h)
        init.append((m0, jnp.zeros((V_ROWS, tq), F32), jnp.ones((1, tq), F32)))

    def body(t, carry):
        j = i - 1 - t
        new = []
        for h in range(hps):
            m, acc, corr_prev = carry[h]
            acc = accum(acc, corr_prev, p_scr[h], j + 1, h)
            p, corr, m = softmax(s_scr[h], m, base_gap(j, h))
            p_scr[h] = p
            s_scr[h] = scores(jnp.maximum(j - 1, 0), h)
            new.append((m, acc, corr))
        return tuple(new)

    carry = lax.fori_loop(0, i - j_lo, body, tuple(init))
    outs = []
    for h in range(hps):
        m, acc, corr_prev = carry[h]
        acc = accum(acc, corr_prev, p_scr[h], j_lo, h)
        outs.append(acc[:HEAD_DIM, :] / acc[HEAD_DIM:HEAD_DIM + 1, :])
    o_out[...] = jnp.concatenate(outs, axis=0)


UNDERFLOW_LOG2 = -136.0


def _oldest_needed_block(qb, k, rem_r, base, t, tq, hps):
    d = qb.shape[1]
    nh = d // HEAD_DIM
    nb = t // tq
    slack = 2.0
    q = qb[:t].astype(F32).reshape(nb, tq, nh, HEAD_DIM)
    kk = k[:t].astype(BF16).astype(F32).reshape(nb, tq, nh, HEAD_DIM)
    rem = rem_r[:, :t].T.reshape(nb, tq, nh)
    q_norm = jnp.sqrt(jnp.max(jnp.sum(q * q, axis=-1), axis=1))
    k_norm = jnp.sqrt(jnp.max(jnp.sum(kk * kk, axis=-1), axis=(0, 1)))
    self_min = jnp.min(jnp.sum(q * kk, axis=-1), axis=1)
    s_max = q_norm * k_norm + jnp.max(rem, axis=1) + jnp.max(-rem, axis=(0, 1)) + slack
    m_low = self_min - slack
    b = base.reshape(nb, nh)
    gap = b[:, None, :] - b[None, :, :]
    older = jnp.arange(nb)[None, :, None] < jnp.arange(nb)[:, None, None]
    alive = jnp.logical_and(older, gap + (s_max - m_low)[:, None, :] >= UNDERFLOW_LOG2)
    idx = jnp.arange(nb, dtype=jnp.int32)
    j_lo = jnp.min(jnp.where(alive, idx[None, :, None], idx[:, None, None]), axis=1)
    return jnp.min(j_lo.reshape(nb, nh // hps, hps), axis=-1).T.reshape(-1).astype(jnp.int32)


def _fox_prompt(base, j_lo, q_aug, k_aug, vt_aug, t, tq, hps):
    nh = vt_aug.shape[0]
    return pl.pallas_call(
        functools.partial(_fox_prompt_kernel, tq, hps, nh),
        grid=(nh // hps, t // tq),
        in_specs=[pl.BlockSpec(memory_space=pltpu.SMEM), pl.BlockSpec(memory_space=pltpu.SMEM),
                  pl.BlockSpec((tq, hps * PAIR), lambda g, i: (i, g)),
                  pl.BlockSpec((t, hps * PAIR), lambda g, i: (0, g), pipeline_mode=pl.Buffered(1)),
                  pl.BlockSpec((hps, V_ROWS, t), lambda g, i: (g, 0, 0), pipeline_mode=pl.Buffered(1))],
        out_specs=pl.BlockSpec((hps * HEAD_DIM, tq), lambda g, i: (g, i)),
        out_shape=jax.ShapeDtypeStruct((nh * HEAD_DIM, t), F32),
        scratch_shapes=[pltpu.VMEM((hps, tq, tq), F32), pltpu.VMEM((hps, tq, tq), BF16)],
        compiler_params=_params(("parallel", "parallel"), 48),
        name="fox_prompt_attn",
    )(base, j_lo, q_aug, k_aug, vt_aug)


def _fox_sample_kernel(n_pairs, q_ref, kn_ref, vn_ref, lfn_ref, kc_ref, vc_ref, lfc_ref,
                       triu_ref, o_out):
    t = q_ref.shape[0]
    past = kc_ref.shape[1]
    lfc = lfc_ref[0]
    lfn = lfn_ref[0]
    nh = lfc.shape[1]
    lfc_s = _split3(lfc)
    cum_c_r = _dot_tn3(lfc_s, triu_ref[...])
    ones_row = jnp.ones((1, past), BF16)
    tot_r = _dot_m3(ones_row, lfc_s)
    tot_c = cum_c_r[:, past - 1:past]
    rr = lax.broadcasted_iota(jnp.int32, (t, t), 0)
    cc = lax.broadcasted_iota(jnp.int32, (t, t), 1)
    causal = rr >= cc
    tril = jnp.where(causal, 1.0, 0.0).astype(BF16)
    lfn_s = _split3(lfn)
    cum_n_c = (_dot_m3(tril, lfn_s) + tot_r) * LOG2E
    cum_n_r = (_dot_tn3(lfn_s, jnp.where(rr <= cc, 1.0, 0.0).astype(BF16)) + tot_c) * LOG2E
    cum_c_r = cum_c_r * LOG2E
    lane = lax.broadcasted_iota(jnp.int32, (1, PAIR), 1)
    head0 = lane < HEAD_DIM
    hl = lax.broadcasted_iota(jnp.int32, (1, nh), 1)
    for p in range(n_pairs):
        sl = slice(PAIR * p, PAIR * (p + 1))
        q = q_ref[:, sl]
        zero = jnp.zeros_like(q)
        kc = kc_ref[0, :, sl].astype(BF16)
        vc = vc_ref[0, :, sl].astype(BF16)
        kn = kn_ref[:, sl].astype(BF16)
        vn = vn_ref[:, sl].astype(BF16)
        outs = []
        for h in range(2):
            hh = 2 * p + h
            qh = jnp.where(head0, q, zero) if h == 0 else jnp.where(head0, zero, q)
            cq = jnp.sum(jnp.where(hl == hh, cum_n_c, 0.0), axis=1, keepdims=True)
            s_c = _dot_nt(qh, kc) + cq - cum_c_r[hh:hh + 1, :]
            s_n = _dot_nt(qh, kn) + cq - cum_n_r[hh:hh + 1, :]
            s_n = jnp.where(causal, s_n, NEG_INF)
            m = jnp.maximum(jnp.max(s_c, axis=1, keepdims=True), jnp.max(s_n, axis=1, keepdims=True))
            p_c = jnp.exp2(s_c - m)
            p_n = jnp.exp2(s_n - m)
            l = jnp.sum(p_c, axis=1, keepdims=True) + jnp.sum(p_n, axis=1, keepdims=True)
            acc = _dot(p_c.astype(BF16), vc) + _dot(p_n.astype(BF16), vn)
            outs.append(acc / l)
        o_out[:, sl] = jnp.where(head0, outs[0], outs[1])


def _split3(x):
    hi = x.astype(BF16)
    r1 = x - hi.astype(F32)
    mid = r1.astype(BF16)
    lo = (r1 - mid.astype(F32)).astype(BF16)
    return hi, mid, lo


def _dot_tn3(xs, m):
    return _dot_tn(xs[0], m) + (_dot_tn(xs[1], m) + _dot_tn(xs[2], m))


def _dot_m3(m, xs):
    return _dot(m, xs[0]) + (_dot(m, xs[1]) + _dot(m, xs[2]))


def _fox_sample(qb, k, v, lf_new, cache_k, cache_v, cache_lf, row0, n_stream, t):
    d = qb.shape[1]
    n_pairs = d // PAIR
    past = cache_k.shape[1]
    nh = cache_lf.shape[2]
    blk0 = row0 // t
    triu = (jnp.arange(past)[:, None] <= jnp.arange(past)[None, :]).astype(BF16)
    row = pl.BlockSpec((t, d), lambda b: (blk0 + b, 0))
    return pl.pallas_call(
        functools.partial(_fox_sample_kernel, n_pairs),
        grid=(n_stream,),
        in_specs=[row, row, row,
                  pl.BlockSpec((1, t, nh), lambda b: (b, 0, 0)),
                  pl.BlockSpec((1, past, d), lambda b: (b, 0, 0)),
                  pl.BlockSpec((1, past, d), lambda b: (b, 0, 0)),
                  pl.BlockSpec((1, past, nh), lambda b: (b, 0, 0)),
                  _const_spec(triu.shape)],
        out_specs=pl.BlockSpec((t, d), lambda b: (b, 0)),
        out_shape=jax.ShapeDtypeStruct((n_stream * t, d), F32),
        compiler_params=_params(("parallel",), 48),
        name="fox_sample_attn",
    )(qb, k, v, lf_new, cache_k, cache_v, cache_lf, triu)


def kernel(x_prompt, x_sample, state_shift, state_wkv, cache_k, cache_v, cache_logf, ln1_g, ln1_b, ln2_g, ln2_b, a_mix, a_w_rkv, a_w0, a_w1, a_w2, a_a0, a_a1, a_a2, a_v0, a_v1, a_v2, a_g1, a_g2, a_k_k, a_k_a, a_r_k, a_lnx_g, a_lnx_b, a_w_o, kv_w, kv_bf, b_w_qg, b_w_o, router_w, router_b, moe_w_gu, moe_w_down):
    nb, seq, d = x_prompt.shape
    db, dt, _ = x_sample.shape
    assert nb == 1
    n_heads = d // HEAD_DIM
    depth = ln1_g.shape[0]
    n_a = a_mix.shape[0]
    past = cache_k.shape[1]
    t_p = nb * seq
    t_s = db * dt
    n = t_p + t_s
    tm = _row_tile(n)
    c_p = 64
    c_s = dt
    tq_attn = 512 if t_p % 512 == 0 else 256
    assert seq % c_p == 0 and t_p % c_s == 0 and (c_s & (c_s - 1)) == 0
    alpha_dn = (2 * depth) ** 0.25
    row2 = lambda a: a.reshape(1, -1)

    bd = (jnp.arange(256)[:, None] // HEAD_DIM == jnp.arange(256)[None, :] // HEAD_DIM).astype(BF16)
    rw_t = router_w.T
    rb_c = router_b.reshape(-1, 1)

    x = jnp.concatenate([x_prompt.reshape(t_p, d), x_sample.reshape(t_s, d)], axis=0)
    new_shift_p, new_shift_s, new_wkv_p, new_wkv_s = [], [], [], []
    v_first = None
    kv = None
    for l in range(depth):
        if l < n_a:
            xs = x[t_p:].reshape(db, dt, d)
            new_shift_p.append(x[t_p - 1:t_p].reshape(nb, d))
            new_shift_s.append(xs[:, -1])
            xs_prev = jnp.concatenate([state_shift[l][:, None, :], xs[:, :-1]], axis=1).reshape(t_s, d)
            w = dict(mix=jnp.concatenate([a_mix[l], jnp.zeros((2, d), F32)], axis=0),
                     wr=a_w_rkv[l, 0].astype(BF16), wk=a_w_rkv[l, 1].astype(BF16), wv=a_w_rkv[l, 2].astype(BF16),
                     w1=a_w1[l].astype(BF16), w2=a_w2[l].astype(BF16), w0=row2(a_w0[l]),
                     a1=a_a1[l].astype(BF16), a2=a_a2[l].astype(BF16), a0=row2(a_a0[l]),
                     g1=a_g1[l].astype(BF16), g2=a_g2[l].astype(BF16),
                     k_k=row2(a_k_k[l]), k_a=row2(a_k_a[l]), bd=bd)
            if l > 0:
                w.update(v1=a_v1[l - 1].astype(BF16), v2=a_v2[l - 1].astype(BF16), v0=row2(a_v0[l - 1]))
            r, lw, k, v, kn, b, g = _rwkv_pre(x, xs_prev, v_first if l > 0 else None, w, tm)
            if l == 0:
                v_first = v
            scan_in = (r, lw, k, v, kn, b)
            rp, yp, mm, nn = _wkv_chunks(scan_in, c_p, 0, t_p // c_p, n_heads // 2)
            s0 = jnp.zeros((1, n_heads // 2, PAIR, PAIR), F32)
            y_p, sf_p = _wkv_seq(rp, yp, mm, nn, s0, c_p, 1, t_p // c_p)
            rp, yp, mm, nn = _wkv_chunks(scan_in, c_s, t_p, db, n_heads // 2)
            y_s, sf_s = _wkv_seq(rp, yp, mm, nn, _state_to_blockdiag(state_wkv[l]), c_s, db, 1)
            new_wkv_p.append(_blockdiag_to_state(sf_p))
            new_wkv_s.append(_blockdiag_to_state(sf_s))
            y = jnp.concatenate([y_p, y_s], axis=0)
            consts = [row2(a_lnx_g[l]), row2(a_lnx_b[l]), row2(a_r_k[l]), bd, a_w_o[l].astype(BF16),
                      row2(ln1_g[l]), row2(ln1_b[l]), rw_t, rb_c]
            x1, cls = _mix_post(True, alpha_dn, [x, y, r, k, v, g], consts, tm)
        else:
            lb = l - n_a
            wq = b_w_qg[lb][:, :d].astype(BF16)
            qb, q_aug, gate = _fox_qg(x, kv['rem_r'], wq, _pad_heads(wq, PAIR), b_w_qg[lb][:, d:].astype(BF16), tm)
            j_lo = _oldest_needed_block(qb, kv['k'], kv['rem_r'], kv['base'], t_p, tq_attn, 4)
            o_p = _fox_prompt(kv['base'], j_lo, q_aug, kv['k_aug'], kv['vt_aug'], t_p, tq_attn, 4).T
            o_s = _fox_sample(qb, kv['k'], kv['v'], kv['lf_s'], cache_k.reshape(db, past, d),
                              cache_v.reshape(db, past, d), cache_logf, t_p, db, dt)
            o = jnp.concatenate([o_p, o_s], axis=0)
            consts = [b_w_o[lb].astype(BF16), row2(ln1_g[l]), row2(ln1_b[l]), rw_t, rb_c]
            x1, cls = _mix_post(False, alpha_dn, [x, o, gate], consts, tm)
        m = _moe(x1, cls[0], moe_w_gu[l].astype(BF16), moe_w_down[l].astype(BF16))
        if l == n_a - 1:
            wk, wv = kv_w[:, :d].astype(BF16), kv_w[:, d:2 * d].astype(BF16)
            kvw = dict(wk=wk, wv=wv, wk_aug=_pad_heads(wk, PAIR), wv_aug=_pad_heads(wv, V_ROWS),
                       wf_t=kv_w[:, 2 * d:].T, bf=kv_bf.reshape(-1, 1), tiles_per_blk=tq_attn // tm)
            x, k_all, v_all, k_aug, vt_aug, lf_r, cum_r, rem_r = _ln2(
                alpha_dn, x1, m, row2(ln2_g[l]), row2(ln2_b[l]), kvw, tm)
            base = (cum_r[:, :t_p:tq_attn] * LOG2E).T.reshape(-1)
            kv = dict(k=k_all, v=v_all, base=base, rem_r=rem_r, k_aug=k_aug,
                      vt_aug=vt_aug.reshape(n_heads, V_ROWS, n), lf_p=lf_r[:, :t_p].T,
                      lf_s=lf_r[:, t_p:].T.reshape(db, dt, n_heads))
        else:
            x = _ln2(alpha_dn, x1, m, row2(ln2_g[l]), row2(ln2_b[l]), None, tm)[0]

    y_prompt = x[:t_p].reshape(nb, seq, d)
    y_sample = x[t_p:].reshape(db, dt, d)
    p_k = kv['k'][:t_p].reshape(nb, seq, n_heads, HEAD_DIM)
    p_v = kv['v'][:t_p].reshape(nb, seq, n_heads, HEAD_DIM)
    s_k = kv['k'][t_p:].reshape(db, dt, n_heads, HEAD_DIM)
    s_v = kv['v'][t_p:].reshape(db, dt, n_heads, HEAD_DIM)
    return (y_prompt, y_sample, jnp.stack(new_shift_p), jnp.stack(new_wkv_p), p_k, p_v,
            kv['lf_p'].reshape(nb, seq, n_heads), jnp.stack(new_shift_s), jnp.stack(new_wkv_s),
            s_k, s_v, kv['lf_s'])
```
